```python
import math
import jax, jax.numpy as jnp
from jax import lax
import numpy as np

D_MODEL = 1024
BATCH = 8
SEQ = 2048
DEPTH = 2

CTX_LEN = 256
GRID_W = 64
EPS = 1e-6

MLA_HEADS = 8
MLA_NOPE = 64
MLA_ROPE = 32
MLA_V = 64
MLA_Q_RANK = 256
MLA_KV_RANK = 128
ROPE_BASE = 10000.0
Q_BLOCK = 128
ATTN_SCALE = (MLA_NOPE + MLA_ROPE) ** -0.5

SSM_HEADS = 8
SSM_HEADDIM = 64
SSM_INNER = SSM_HEADS * SSM_HEADDIM
SSM_GROUPS = 2
SSM_STATE = 128
SSM_CONV = 5
SSM_CHUNK = 128
XBC_DIM = SSM_INNER + 2 * SSM_GROUPS * SSM_STATE

GM_GROUPS = 8
GM_WIDTH = 512
GM_GDIM = GM_WIDTH // GM_GROUPS
GM_CHUNK = 128

D_FF = 2816
N_BRANCH = 3
N_MOD = 9

KV_SIZES = (MLA_KV_RANK, MLA_ROPE, XBC_DIM, SSM_HEADS, SSM_HEADS)
Q_SIZES = (MLA_Q_RANK, SSM_INNER, 2 * GM_WIDTH, N_BRANCH * D_MODEL)
KV_SIDE = MLA_KV_RANK + MLA_ROPE + XBC_DIM + 2 * SSM_HEADS
D_IN = KV_SIDE + MLA_Q_RANK + SSM_INNER + 2 * GM_WIDTH + N_BRANCH * D_MODEL

kernel_name = "hybrid_mla_ssd_gmlp_prefix_dit"


def _split(t, sizes):
    offs = []
    acc = 0
    for s in sizes[:-1]:
        acc += s
        offs.append(acc)
    return jnp.split(t, offs, axis=-1)


def rmsnorm(x, g):
    xf = x.astype(jnp.float32)
    y = xf * lax.rsqrt(jnp.mean(xf * xf, axis=-1, keepdims=True) + EPS)
    return (y * g.astype(jnp.float32)).astype(x.dtype)


def layernorm(x, g):
    xf = x.astype(jnp.float32)
    xf = xf - jnp.mean(xf, axis=-1, keepdims=True)
    y = xf * lax.rsqrt(jnp.mean(xf * xf, axis=-1, keepdims=True) + EPS)
    return (y * g.astype(jnp.float32)).astype(x.dtype)


def pre(h, g, shift, scale):
    return rmsnorm(h, g) * (1.0 + scale) + shift


def swiglu(h, w_i, w_o):
    gate, up = jnp.split(h @ w_i, 2, axis=-1)
    return (jax.nn.silu(gate) * up) @ w_o


def axial_rope_tables(rows):
    row = jnp.repeat(jnp.arange(rows, dtype=jnp.float32), GRID_W)
    col = jnp.tile(jnp.arange(GRID_W, dtype=jnp.float32), rows)
    n_freq = MLA_ROPE // 4
    inv = jnp.power(ROPE_BASE, -jnp.arange(n_freq, dtype=jnp.float32) / n_freq)
    ang = jnp.concatenate([row[:, None] * inv, col[:, None] * inv], axis=-1)
    return jnp.cos(ang), jnp.sin(ang)


def apply_rope(x, cos, sin):
    half = x.shape[-1] // 2
    x1, x2 = x[..., :half], x[..., half:]
    return jnp.concatenate([x1 * cos - x2 * sin, x1 * sin + x2 * cos], axis=-1).astype(x.dtype)


def mla_keys(kv_lat, p):
    B, L, _ = kv_lat.shape
    kv = (rmsnorm(kv_lat, p["mla_kv_norm"]) @ p["mla_w_ukv"]).reshape(B, L, MLA_HEADS, MLA_NOPE + MLA_V)
    return kv[..., :MLA_NOPE], kv[..., MLA_NOPE:]


def mla_queries(q_lat, p):
    B, L, _ = q_lat.shape
    q = (rmsnorm(q_lat, p["mla_q_norm"]) @ p["mla_w_uq"]).reshape(B, L, MLA_HEADS, MLA_NOPE + MLA_ROPE)
    return q[..., :MLA_NOPE], q[..., MLA_NOPE:]


def block_attention(q_nope, q_rope, k_nope, k_rope, v):
    B, Lq, H, _ = q_nope.shape
    nb = Lq // Q_BLOCK

    def blocks(t):
        return jnp.moveaxis(t.reshape((B, nb, Q_BLOCK) + t.shape[2:]), 1, 0)

    def one(qs):
        qn, qr = qs
        s = jnp.einsum('bqhd,bkhd->bhqk', qn, k_nope) + jnp.einsum('bqhr,bkr->bhqk', qr, k_rope)
        pr = jax.nn.softmax(s.astype(jnp.float32) * ATTN_SCALE, axis=-1).astype(v.dtype)
        return jnp.einsum('bhqk,bkhd->bqhd', pr, v)

    o = lax.map(one, (blocks(q_nope), blocks(q_rope)))
    return jnp.moveaxis(o, 0, 1).reshape(B, Lq, H * MLA_V)


def dwconv_centred(x, w, b):
    pad = SSM_CONV // 2
    out = lax.conv_general_dilated(x, w.T[:, None, :].astype(x.dtype), window_strides=(1,),
                                   padding=[(pad, pad)], dimension_numbers=('NWC', 'WIO', 'NWC'),
                                   feature_group_count=x.shape[-1])
    return out + b


def ssm_prep(xbc, dtf, dtb, p):
    B, L, _ = xbc.shape
    xbc = jax.nn.silu(dwconv_centred(xbc, p["ssm_conv_w"], p["ssm_conv_b"]))
    xs, Bm, Cm = _split(xbc, (SSM_INNER, SSM_GROUPS * SSM_STATE, SSM_GROUPS * SSM_STATE))
    xs = xs.reshape(B, L, SSM_HEADS, SSM_HEADDIM)
    Bm = Bm.reshape(B, L, SSM_GROUPS, SSM_STATE)
    Cm = Cm.reshape(B, L, SSM_GROUPS, SSM_STATE)
    dt_bias = p["ssm_dt_bias"].astype(jnp.float32)
    dt_f = jax.nn.softplus(dtf.astype(jnp.float32) + dt_bias[0])
    dt_b = jax.nn.softplus(dtb.astype(jnp.float32) + dt_bias[1])
    return xs, Bm, Cm, dt_f, dt_b


def ssd(x, dt, A, Bm, Cm, h0, need_y):
    Bsz, L, H, P = x.shape
    N = Bm.shape[-1]
    Q = SSM_CHUNK
    nc = L // Q
    rep = H // Bm.shape[2]
    Bc = jnp.repeat(Bm, rep, axis=2).astype(jnp.float32).reshape(Bsz, nc, Q, H, N)
    Cc = jnp.repeat(Cm, rep, axis=2).astype(jnp.float32).reshape(Bsz, nc, Q, H, N)
    xs = x.astype(jnp.float32).reshape(Bsz, nc, Q, H, P)
    dtc = dt.reshape(Bsz, nc, Q, H)
    a_cum = jnp.cumsum(dtc * A, axis=2)
    a_tot = a_cum[:, :, -1]
    w_state = jnp.exp(a_tot[:, :, None] - a_cum) * dtc
    S = jnp.einsum('bcqh,bcqhn,bcqhp->bchpn', w_state, Bc, xs)

    def step(h, inp):
        decay, s = inp
        return decay[:, :, None, None] * h + s, h

    h_last, h_prev = lax.scan(step, h0, (jnp.moveaxis(jnp.exp(a_tot), 1, 0), jnp.moveaxis(S, 1, 0)))
    if not need_y:
        return None, h_last
    h_prev = jnp.moveaxis(h_prev, 0, 1)
    y_inter = jnp.einsum('bcqhn,bchpn->bcqhp', Cc, h_prev) * jnp.exp(a_cum)[..., None]
    mask = jnp.tril(jnp.ones((Q, Q), dtype=bool))
    diff = a_cum[:, :, :, None, :] - a_cum[:, :, None, :, :]
    decay_mat = jnp.exp(jnp.where(mask[None, None, :, :, None], diff, -jnp.inf))
    scores = jnp.einsum('bcihn,bcjhn->bcijh', Cc, Bc) * decay_mat * dtc[:, :, None, :, :]
    y_intra = jnp.einsum('bcijh,bcjhp->bcihp', scores, xs)
    y = (y_intra + y_inter).reshape(Bsz, L, H, P)
    return y.astype(x.dtype), h_last


def ssm_bidir(lat, ctx, p, with_ctx):
    xl, Bl, Cl, dfl, dbl = lat
    xc, Bc, Cc, dfc, dbc = ctx
    A = -jnp.exp(p["ssm_a_log"].astype(jnp.float32))
    Bsz = xl.shape[0]
    h0 = jnp.zeros((Bsz, SSM_HEADS, SSM_HEADDIM, SSM_STATE), jnp.float32)
    fl = lambda t: jnp.flip(t, axis=1)
    yc_f, hc_f = ssd(xc, dfc, A[0], Bc, Cc, h0, with_ctx)
    yl_f, _ = ssd(xl, dfl, A[0], Bl, Cl, hc_f, True)
    yc_b, hc_b = ssd(fl(xc), fl(dbc), A[1], fl(Bc), fl(Cc), h0, with_ctx)
    yl_b, _ = ssd(fl(xl), fl(dbl), A[1], fl(Bl), fl(Cl), hc_b, True)
    d = p["ssm_d"][None, None, :, None]
    y_l = yl_f + fl(yl_b) + d * xl
    y_c = (yc_f + fl(yc_b) + d * xc) if with_ctx else None
    return y_l, y_c


def ssm_out(y, z, p):
    B, L = y.shape[:2]
    y = y.reshape(B, L, SSM_INNER) * jax.nn.silu(z)
    return rmsnorm(y, p["ssm_norm"]) @ p["ssm_w_o"]


def spatial_gating(uv, p):
    B, L, _ = uv.shape
    u, v = jnp.split(jax.nn.gelu(uv), 2, axis=-1)
    v = layernorm(v, p["gm_norm"])
    nc = L // GM_CHUNK
    vg = v.reshape(B, nc, GM_CHUNK, GM_GROUPS, GM_GDIM)
    mixed = jnp.einsum('gij,bcjgd->bcigd', p["gm_w_s"], vg) + p["gm_b_s"].T[:, :, None]
    return (u * mixed.reshape(B, L, GM_WIDTH)) @ p["gm_w_o"]


def merge(o_mla, o_ssm, o_gm, gate_raw, p):
    g_a, g_b, g_c = jnp.split(jax.nn.sigmoid(gate_raw + p["b_gate"]), N_BRANCH, axis=-1)
    return (g_a * o_mla + g_b * o_ssm + g_c * o_gm) @ p["w_out"]


def token_mixer(h_lat, h_ctx, cos, sin, p, with_ctx):
    B, L, _ = h_lat.shape
    w_in = p["w_in"]
    proj_lat = h_lat @ w_in
    proj_ctx = h_ctx @ (w_in if with_ctx else w_in[:, :KV_SIDE])
    kv_l, kr_l, xbc_l, dtf_l, dtb_l = _split(proj_lat[..., :KV_SIDE], KV_SIZES)
    kv_c, kr_c, xbc_c, dtf_c, dtb_c = _split(proj_ctx[..., :KV_SIDE], KV_SIZES)
    q_l, z_l, uv_l, gr_l = _split(proj_lat[..., KV_SIDE:], Q_SIZES)

    kn_l, v_l = mla_keys(kv_l, p)
    kn_c, v_c = mla_keys(kv_c, p)
    kr_l = apply_rope(kr_l, cos, sin)
    qn_l, qr_l = mla_queries(q_l, p)
    qr_l = apply_rope(qr_l, cos[:, None, :], sin[:, None, :])
    a_l = block_attention(qn_l, qr_l, jnp.concatenate([kn_l, kn_c], axis=1),
                          jnp.concatenate([kr_l, kr_c], axis=1), jnp.concatenate([v_l, v_c], axis=1))
    o_mla_l = a_l @ p["mla_w_o"]

    y_ssm_l, y_ssm_c = ssm_bidir(ssm_prep(xbc_l, dtf_l, dtb_l, p), ssm_prep(xbc_c, dtf_c, dtb_c, p), p, with_ctx)
    o_ssm_l = ssm_out(y_ssm_l, z_l, p)

    o_gm_l = spatial_gating(uv_l, p)
    y_lat = merge(o_mla_l, o_ssm_l, o_gm_l, gr_l, p)

    if not with_ctx:
        return y_lat, None
    q_c, z_c, uv_c, gr_c = _split(proj_ctx[..., KV_SIDE:], Q_SIZES)
    qn_c, qr_c = mla_queries(q_c, p)
    o_mla_c = block_attention(qn_c, qr_c, kn_c, kr_c, v_c) @ p["mla_w_o"]
    o_ssm_c = ssm_out(y_ssm_c, z_c, p)
    o_gm_c = spatial_gating(uv_c, p)
    y_ctx = merge(o_mla_c, o_ssm_c, o_gm_c, gr_c, p)
    return y_lat, y_ctx


def setup_inputs(seed: int = 0) -> dict:
    key = jax.random.key(seed)
    ks = iter(jax.random.split(key, 48))
    f32 = jnp.float32
    L = DEPTH
    D = D_MODEL

    def dense(shape, fan_in, gain=1.0):
        return jax.random.normal(next(ks), shape, f32) * (gain * fan_in ** -0.5)

    def gains(shape):
        return 1.0 + 0.05 * jax.random.normal(next(ks), shape, f32)

    def small(shape):
        return 0.01 * jax.random.normal(next(ks), shape, f32)

    dt0 = jnp.exp(jax.random.uniform(next(ks), (L, 2, SSM_HEADS), f32, math.log(1e-3), math.log(1e-1)))
    a0 = jax.random.uniform(next(ks), (L, 2, SSM_HEADS), f32, 1.0, 16.0)
    return {
        "x": jax.random.normal(next(ks), (BATCH, SEQ, D), f32),
        "c": jax.random.normal(next(ks), (BATCH, D), f32),
        "ctx": jax.random.normal(next(ks), (BATCH, CTX_LEN, D), f32),
        "c_ctx": jax.random.normal(next(ks), (D,), f32),
        "w_ada": dense((L, D, N_MOD * D), D, 0.5),
        "b_ada": small((L, N_MOD * D)),
        "norm_g": gains((L, 3, D)),
        "ffn1_w_in": dense((L, D, 2 * D_FF), D),
        "ffn1_w_out": dense((L, D_FF, D), D_FF),
        "ffn2_w_in": dense((L, D, 2 * D_FF), D),
        "ffn2_w_out": dense((L, D_FF, D), D_FF),
        "w_in": dense((L, D, D_IN), D),
        "mla_q_norm": gains((L, MLA_Q_RANK)),
        "mla_w_uq": dense((L, MLA_Q_RANK, MLA_HEADS * (MLA_NOPE + MLA_ROPE)), MLA_Q_RANK),
        "mla_kv_norm": gains((L, MLA_KV_RANK)),
        "mla_w_ukv": dense((L, MLA_KV_RANK, MLA_HEADS * (MLA_NOPE + MLA_V)), MLA_KV_RANK),
        "mla_w_o": dense((L, MLA_HEADS * MLA_V, D), MLA_HEADS * MLA_V),
        "ssm_conv_w": dense((L, XBC_DIM, SSM_CONV), SSM_CONV),
        "ssm_conv_b": small((L, XBC_DIM)),
        "ssm_a_log": jnp.log(a0),
        "ssm_dt_bias": dt0 + jnp.log(-jnp.expm1(-dt0)),
        "ssm_d": gains((L, SSM_HEADS)),
        "ssm_norm": gains((L, SSM_INNER)),
        "ssm_w_o": dense((L, SSM_INNER, D), SSM_INNER),
        "gm_norm": gains((L, GM_WIDTH)),
        "gm_w_s": dense((L, GM_GROUPS, GM_CHUNK, GM_CHUNK), GM_CHUNK),
        "gm_b_s": gains((L, GM_GROUPS, GM_CHUNK)),
        "gm_w_o": dense((L, GM_WIDTH, D), GM_WIDTH),
        "b_gate": small((L, N_BRANCH * D)),
        "w_out": dense((L, D, D), D),
        "final_norm": gains((D,)),
    }


def reference(x, c, ctx, c_ctx, w_ada, b_ada, norm_g, ffn1_w_in, ffn1_w_out, ffn2_w_in, ffn2_w_out,
              w_in, mla_q_norm, mla_w_uq, mla_kv_norm, mla_w_ukv, mla_w_o,
              ssm_conv_w, ssm_conv_b, ssm_a_log, ssm_dt_bias, ssm_d, ssm_norm, ssm_w_o,
              gm_norm, gm_w_s, gm_b_s, gm_w_o, b_gate, w_out, final_norm):
    rows = x.shape[1] // GRID_W
    cos, sin = axial_rope_tables(rows)
    for l in range(DEPTH):
        last = l == DEPTH - 1
        p = {
            "w_in": w_in[l], "mla_q_norm": mla_q_norm[l], "mla_w_uq": mla_w_uq[l],
            "mla_kv_norm": mla_kv_norm[l], "mla_w_ukv": mla_w_ukv[l], "mla_w_o": mla_w_o[l],
            "ssm_conv_w": ssm_conv_w[l], "ssm_conv_b": ssm_conv_b[l], "ssm_a_log": ssm_a_log[l],
            "ssm_dt_bias": ssm_dt_bias[l], "ssm_d": ssm_d[l], "ssm_norm": ssm_norm[l], "ssm_w_o": ssm_w_o[l],
            "gm_norm": gm_norm[l], "gm_w_s": gm_w_s[l], "gm_b_s": gm_b_s[l], "gm_w_o": gm_w_o[l],
            "b_gate": b_gate[l], "w_out": w_out[l],
        }
        mx = jnp.split((jax.nn.silu(c) @ w_ada[l] + b_ada[l])[:, None, :], N_MOD, axis=-1)
        mc = jnp.split(jax.nn.silu(c_ctx) @ w_ada[l] + b_ada[l], N_MOD, axis=-1)

        x = x + 0.5 * mx[2] * swiglu(pre(x, norm_g[l, 0], mx[0], mx[1]), ffn1_w_in[l], ffn1_w_out[l])
        ctx = ctx + 0.5 * mc[2] * swiglu(pre(ctx, norm_g[l, 0], mc[0], mc[1]), ffn1_w_in[l], ffn1_w_out[l])

        y_x, y_c = token_mixer(pre(x, norm_g[l, 1], mx[3], mx[4]), pre(ctx, norm_g[l, 1], mc[3], mc[4]),
                               cos, sin, p, not last)
        x = x + mx[5] * y_x

        x = x + 0.5 * mx[8] * swiglu(pre(x, norm_g[l, 2], mx[6], mx[7]), ffn2_w_in[l], ffn2_w_out[l])
        if not last:
            ctx = ctx + mc[5] * y_c
            ctx = ctx + 0.5 * mc[8] * swiglu(pre(ctx, norm_g[l, 2], mc[6], mc[7]), ffn2_w_in[l], ffn2_w_out[l])
    return rmsnorm(x, final_norm)
```

```python
import functools
import math

import jax
import jax.numpy as jnp
from jax import lax
from jax.experimental import pallas as pl
from jax.experimental.pallas import tpu as pltpu

F32 = jnp.float32
BF16 = jnp.bfloat16

D = 1024
BATCH = 8
SEQ = 2048
DEPTH = 2
CTX = 256
T = CTX + SEQ
GRID_W = 64
EPS = 1e-6

HEADS = 8
NOPE = 64
ROPE = 32
VDIM = 64
Q_RANK = 256
KV_RANK = 128
ROPE_BASE = 10000.0
ATTN_SCALE = (NOPE + ROPE) ** -0.5
HEAD_PAD = 128

SSM_HEADS = 8
SSM_P = 64
SSM_INNER = SSM_HEADS * SSM_P
SSM_GROUPS = 2
SSM_N = 128
SSM_CONV = 5
CHUNK = 128
XBC = SSM_INNER + 2 * SSM_GROUPS * SSM_N
N_CHUNKS = T // CHUNK
CTX_CHUNKS = CTX // CHUNK

GM_GROUPS = 8
GM_WIDTH = 512
GM_GDIM = GM_WIDTH // GM_GROUPS

D_FF = 2816
N_BRANCH = 3
N_MOD = 9
MOD_ROWS = 16
CTX_MOD_ROW = BATCH

KV_SIDE = KV_RANK + ROPE + XBC + 2 * SSM_HEADS

C_KV = 0
C_KR = 128
C_KRR = 256
C_DT = 384
C_XBC = 512
C_Q = C_XBC + XBC
C_Z = C_Q + Q_RANK
C_UV = C_Z + SSM_INNER
C_GATE = C_UV + 2 * GM_WIDTH
C_END = C_GATE + N_BRANCH * D

TM = 256
VMEM_LIMIT = 56 * 1024 * 1024


def _rms(x, g):
    y = x * lax.rsqrt(jnp.mean(x * x, axis=-1, keepdims=True) + EPS)
    return y * g


def _silu(x):
    return x * jax.nn.sigmoid(x)


def _bdot(a, b):
    return jnp.dot(a.astype(BF16), b.astype(BF16), preferred_element_type=F32)


def _split3(a):
    a1 = a.astype(BF16)
    r1 = a - a1.astype(F32)
    a2 = r1.astype(BF16)
    a3 = (r1 - a2.astype(F32)).astype(BF16)
    return a1, a2, a3


def _dot_right01(a, m01):
    return sum(jnp.dot(p, m01, preferred_element_type=F32) for p in _split3(a))


def _dot_left01(m01, a):
    return sum(jnp.dot(m01, p, preferred_element_type=F32) for p in _split3(a))


def _const_spec(shape):
    nd = len(shape)
    return pl.BlockSpec(shape, lambda *_: (0,) * nd, pipeline_mode=pl.Buffered(1))


def _mod_spec(t0):
    return pl.BlockSpec((1, N_MOD, D), lambda b, t: (jnp.where(t + t0 == 0, CTX_MOD_ROW, b), 0, 0))


def _row_spec(width, t0):
    return pl.BlockSpec((1, TM, width), lambda b, t: (b, t + t0, 0))


def _params(sem):
    return pltpu.CompilerParams(dimension_semantics=sem, vmem_limit_bytes=VMEM_LIMIT)


def _ada_kernel(c_ref, w_ref, b_ref, o_ref):
    s = _silu(c_ref[...])
    o_ref[0] = _bdot(s, w_ref[0]) + b_ref[0]


def _ada(c_all, w_ada, b_ada):
    tn = 1152
    nn = N_MOD * D // tn
    return pl.pallas_call(
        _ada_kernel,
        grid=(DEPTH, nn),
        in_specs=[
            pl.BlockSpec((MOD_ROWS, D), lambda l, j: (0, 0)),
            pl.BlockSpec((1, D, tn), lambda l, j: (l, 0, j)),
            pl.BlockSpec((1, 1, tn), lambda l, j: (l, 0, j)),
        ],
        out_specs=pl.BlockSpec((1, MOD_ROWS, tn), lambda l, j: (l, 0, j)),
        out_shape=jax.ShapeDtypeStruct((DEPTH, MOD_ROWS, N_MOD * D), F32),
        compiler_params=_params(("arbitrary", "arbitrary")),
        name="ada_mod",
    )(c_all, w_ada, b_ada.reshape(DEPTH, 1, N_MOD * D))


def _ffn_kernel(x_ref, mod_ref, g_ref, wi_ref, wo_ref, fin_ref, o_ref, *, mod0, final):
    x = x_ref[0]
    m = mod_ref[0]
    shift, scale, gate = m[mod0:mod0 + 1], m[mod0 + 1:mod0 + 2], m[mod0 + 2:mod0 + 3]
    h = _rms(x, g_ref[...]) * (1.0 + scale) + shift
    gu = jnp.dot(h.astype(BF16), wi_ref[...], preferred_element_type=F32)
    a = _silu(gu[:, :D_FF]) * gu[:, D_FF:]
    y = jnp.dot(a.astype(BF16), wo_ref[...], preferred_element_type=F32)
    out = x + (0.5 * gate) * y
    if final:
        out = _rms(out, fin_ref[...])
    o_ref[0] = out


def _ffn(xs, mods, g, wi, wo, fin, *, mod0, t0, final):
    nt = T // TM - t0
    out_rows = nt * TM
    return pl.pallas_call(
        functools.partial(_ffn_kernel, mod0=mod0, final=final),
        grid=(BATCH, nt),
        in_specs=[
            _row_spec(D, t0),
            _mod_spec(t0),
            _const_spec((1, D)),
            _const_spec((D, 2 * D_FF)),
            _const_spec((D_FF, D)),
            _const_spec((1, D)),
        ],
        out_specs=pl.BlockSpec((1, TM, D), lambda b, t: (b, t, 0)),
        out_shape=jax.ShapeDtypeStruct((BATCH, out_rows, D), F32),
        compiler_params=_params(("parallel", "parallel")),
        name="ffn",
    )(xs, mods, g, wi, wo, fin)


def _proj_kernel(x_ref, mod_ref, g_ref, win_ref, wdt_ref, kvn_ref, wkv_ref, qn_ref, wq_ref, wqr_ref,
                 cos_ref, sin_ref,
                 q_out, k_out, v_out, xbc_out, dt_out, dtt_out, z_out, uv_out, gate_out):
    x = x_ref[0]
    m = mod_ref[0]
    h = _rms(x, g_ref[...]) * (1.0 + m[4:5]) + m[3:4]
    hb = h.astype(BF16)
    cos = cos_ref[...]
    sin = sin_ref[...]

    def proj(c0, c1):
        return jnp.dot(hb, win_ref[:, c0:c1], preferred_element_type=F32)

    small = proj(C_KV, C_XBC)
    xbc_out[0] = proj(C_XBC, C_Q)
    q_lat = proj(C_Q, C_Z)
    z_out[0] = proj(C_Z, C_UV)
    uv_out[0] = proj(C_UV, C_GATE)
    gate_out[0] = proj(C_GATE, C_END)

    dt_out[0] = small[:, C_DT:C_DT + 128]
    dtt_out[0] = lax.dot_general(wdt_ref[...], hb, (((1,), (1,)), ((), ())), preferred_element_type=F32)

    kvu = _bdot(_rms(small[:, C_KV:C_KV + 128], kvn_ref[...]), wkv_ref[...])
    kr = small[:, C_KR:C_KR + 128] * cos + small[:, C_KRR:C_KRR + 128] * sin
    for hd in range(HEADS):
        sl = slice(hd * HEAD_PAD, (hd + 1) * HEAD_PAD)
        k_out[0, :, sl] = (kvu[:, sl] + kr).astype(BF16)
    v_out[0] = kvu[:, HEADS * HEAD_PAD:].astype(BF16)

    qn = _rms(q_lat, qn_ref[...]).astype(BF16)
    qa = jnp.dot(qn, wq_ref[...], preferred_element_type=F32)
    qb = jnp.dot(qn, wqr_ref[...], preferred_element_type=F32)
    for hd in range(HEADS):
        sl = slice(hd * HEAD_PAD, (hd + 1) * HEAD_PAD)
        q_out[0, :, sl] = (qa[:, sl] * cos + qb[:, sl] * sin).astype(BF16)


def _proj(xs, mods, g, lw, cos_t, sin_t):
    nt = T // TM
    widths = [(HEADS * HEAD_PAD, BF16), (HEADS * HEAD_PAD, BF16), (HEADS * VDIM, BF16), (XBC, F32), (128, F32)]
    tail = [(SSM_INNER, F32), (2 * GM_WIDTH, F32), (N_BRANCH * D, F32)]
    out_shape = [jax.ShapeDtypeStruct((BATCH, T, w), dt) for w, dt in widths]
    out_specs = [_row_spec(w, 0) for w, _ in widths]
    out_shape.append(jax.ShapeDtypeStruct((BATCH, 2 * SSM_HEADS, T), F32))
    out_specs.append(pl.BlockSpec((1, 2 * SSM_HEADS, TM), lambda b, t: (b, 0, t)))
    out_shape += [jax.ShapeDtypeStruct((BATCH, T, w), dt) for w, dt in tail]
    out_specs += [_row_spec(w, 0) for w, _ in tail]
    return pl.pallas_call(
        _proj_kernel,
        grid=(BATCH, nt),
        in_specs=[
            _row_spec(D, 0),
            _mod_spec(0),
            _const_spec((1, D)),
            _const_spec((D, C_END)),
            _const_spec((2 * SSM_HEADS, D)),
            _const_spec((1, KV_RANK)),
            _const_spec((KV_RANK, HEADS * (HEAD_PAD + VDIM))),
            _const_spec((1, Q_RANK)),
            _const_spec((Q_RANK, HEADS * HEAD_PAD)),
            _const_spec((Q_RANK, HEADS * HEAD_PAD)),
            pl.BlockSpec((TM, HEAD_PAD), lambda b, t: (t, 0)),
            pl.BlockSpec((TM, HEAD_PAD), lambda b, t: (t, 0)),
        ],
        out_specs=out_specs,
        out_shape=out_shape,
        compiler_params=_params(("parallel", "parallel")),
        name="mixer_proj",
    )(xs, mods, g, lw["w_in"], lw["w_dt_t"], lw["kv_norm"], lw["w_kv"], lw["q_norm"], lw["w_q"], lw["w_q_rot"],
      cos_t, sin_t)


def _attn_kernel(q_ref, k_ref, v_ref, o_ref, *, t0):
    def run(nk):
        for hd in range(HEADS):
            q = q_ref[0, :, hd * HEAD_PAD:(hd + 1) * HEAD_PAD]
            k = k_ref[0, :nk, hd * HEAD_PAD:(hd + 1) * HEAD_PAD]
            s = lax.dot_general(q, k, (((1,), (1,)), ((), ())), preferred_element_type=F32)
            mx = jnp.max(s, axis=-1, keepdims=True)
            p = jnp.exp((s - mx) * ATTN_SCALE)
            den = jnp.sum(p, axis=-1, keepdims=True)
            o = jnp.dot(p.astype(BF16), v_ref[0, :nk, hd * VDIM:(hd + 1) * VDIM], preferred_element_type=F32)
            o_ref[0, :, hd * VDIM:(hd + 1) * VDIM] = (o / den).astype(BF16)

    if t0 == 0:
        t = pl.program_id(1)
        pl.when(t == 0)(lambda: run(CTX))
        pl.when(t != 0)(lambda: run(T))
    else:
        run(T)


def _attn(q, k, v, *, t0):
    nt = T // TM - t0
    return pl.pallas_call(
        functools.partial(_attn_kernel, t0=t0),
        grid=(BATCH, nt),
        in_specs=[
            _row_spec(HEADS * HEAD_PAD, t0),
            pl.BlockSpec((1, T, HEADS * HEAD_PAD), lambda b, t: (b, 0, 0)),
            pl.BlockSpec((1, T, HEADS * VDIM), lambda b, t: (b, 0, 0)),
        ],
        out_specs=_row_spec(HEADS * VDIM, t0),
        out_shape=jax.ShapeDtypeStruct((BATCH, T, HEADS * VDIM), BF16),
        compiler_params=_params(("parallel", "arbitrary")),
        name="mla_attn",
    )(q, k, v)


def _ssd_kernel(xbc_ref, dt_ref, dtt_ref, cw_ref, cb_ref, bias_row_ref, bias_col_ref, alog_row_ref, alog_col_ref,
                dfull_ref, e_ref, y_ref, xc_ref, h_ref):
    def conv_body(c, carry):
        r0 = pl.multiple_of(c * CHUNK, CHUNK)
        cur = xbc_ref[0, pl.ds(r0, CHUNK), :]
        p0 = pl.multiple_of(jnp.maximum(r0 - 8, 0), 8)
        prev = xbc_ref[0, pl.ds(p0, 8), :]
        prev = jnp.where((c == 0) | (c == CTX_CHUNKS), 0.0, prev)
        n0 = pl.multiple_of(jnp.minimum(r0 + CHUNK, T - 8), 8)
        nxt = xbc_ref[0, pl.ds(n0, 8), :]
        nxt = jnp.where((c == CTX_CHUNKS - 1) | (c == N_CHUNKS - 1), 0.0, nxt)
        win = jnp.concatenate([prev, cur, nxt], axis=0)
        acc = cb_ref[...]
        for k in range(SSM_CONV):
            off = 8 + k - SSM_CONV // 2
            acc = acc + cw_ref[k:k + 1, :] * win[off:off + CHUNK]
        xc_ref[pl.ds(r0, CHUNK), :] = _silu(acc)
        return carry

    lax.fori_loop(0, N_CHUNKS, conv_body, 0)

    row = lax.broadcasted_iota(jnp.int32, (CHUNK, CHUNK), 0)
    col = lax.broadcasted_iota(jnp.int32, (CHUNK, CHUNK), 1)
    lower = (col <= row)
    upper = (col >= row)
    lower01 = jnp.where(lower, 1.0, 0.0).astype(BF16)
    upper01 = jnp.where(upper, 1.0, 0.0).astype(BF16)
    lo_half = lax.broadcasted_iota(jnp.int32, (CHUNK, 128), 1) < SSM_P
    lo_half_row = lo_half[0:1]
    a_row = -jnp.exp(alog_row_ref[...])
    a_col = -jnp.exp(alog_col_ref[...])

    def chunk_step(c, d):
        r0 = pl.multiple_of(c * CHUNK, CHUNK)
        xc = xc_ref[pl.ds(r0, CHUNK), :]
        dcol = jax.nn.softplus(dt_ref[0, pl.ds(r0, CHUNK), :] + bias_row_ref[...])
        drow = jax.nn.softplus(dtt_ref[0, :, pl.ds(r0, CHUNK)] + bias_col_ref[...])
        e01 = e_ref[d]
        tri_l, tri_r, mask = (lower01, upper01, lower) if d == 0 else (upper01, lower01, upper)
        dt_e = _dot_right01(dcol, e01)
        acum_e = _dot_left01(tri_l, _dot_right01(dcol * a_row, e01))
        acum_r = _dot_right01(drow * a_col, tri_r)
        atot_e = acum_e[CHUNK - 1:CHUNK] if d == 0 else acum_e[0:1]

        ys = []
        for g in range(SSM_GROUPS):
            bm = xc[:, SSM_INNER + g * SSM_N:SSM_INNER + (g + 1) * SSM_N]
            cm = xc[:, SSM_INNER + (SSM_GROUPS + g) * SSM_N:SSM_INNER + (SSM_GROUPS + g + 1) * SSM_N]
            bmb = bm.astype(BF16)
            cmb = cm.astype(BF16)
            cb = lax.dot_general(cmb, bmb, (((1,), (1,)), ((), ())), preferred_element_type=F32)
            bt = bm.T.astype(BF16)
            for pp in range(2):
                p = g * 2 + pp
                sc = []
                for hh in range(2):
                    hd = 2 * p + hh
                    diff = acum_e[:, hd * 128:(hd + 1) * 128] - acum_r[d * SSM_HEADS + hd:d * SSM_HEADS + hd + 1]
                    dec = jnp.exp(jnp.where(mask, diff, -jnp.inf))
                    sc.append(cb * dec * drow[d * SSM_HEADS + hd:d * SSM_HEADS + hd + 1])
                h0, h1 = 2 * p, 2 * p + 1
                dt_p = jnp.where(lo_half, dt_e[:, h0 * 128:(h0 + 1) * 128], dt_e[:, h1 * 128:(h1 + 1) * 128])
                acum_p = jnp.where(lo_half, acum_e[:, h0 * 128:(h0 + 1) * 128], acum_e[:, h1 * 128:(h1 + 1) * 128])
                atot_p = jnp.where(lo_half_row, atot_e[:, h0 * 128:(h0 + 1) * 128],
                                   atot_e[:, h1 * 128:(h1 + 1) * 128])
                x_p = xc[:, p * 128:(p + 1) * 128]
                xw = (x_p * (jnp.exp(atot_p - acum_p) * dt_p)).astype(BF16)
                s_t = jnp.dot(bt, xw, preferred_element_type=F32)
                h_prev = h_ref[p]
                y_inter = jnp.dot(cmb, h_prev.astype(BF16), preferred_element_type=F32) * jnp.exp(acum_p)
                pmat = jnp.concatenate(sc, axis=1).astype(BF16)
                xbd = jnp.concatenate([jnp.where(lo_half, x_p, 0.0), jnp.where(lo_half, 0.0, x_p)],
                                      axis=0).astype(BF16)
                ys.append(jnp.dot(pmat, xbd, preferred_element_type=F32) + y_inter)
                h_ref[p] = h_prev * jnp.exp(atot_p) + s_t
        y = jnp.concatenate(ys, axis=1)
        if d == 0:
            y_ref[0, pl.ds(r0, CHUNK), :] = y + dfull_ref[...] * xc[:, :SSM_INNER]
        else:
            y_ref[0, pl.ds(r0, CHUNK), :] += y

    h_ref[...] = jnp.zeros_like(h_ref)

    def fwd_body(i, carry):
        chunk_step(i, 0)
        return carry

    lax.fori_loop(0, N_CHUNKS, fwd_body, 0)

    h_ref[...] = jnp.zeros_like(h_ref)

    def bwd_body(i, carry):
        c = jnp.where(i < CTX_CHUNKS, CTX_CHUNKS - 1 - i, N_CHUNKS + CTX_CHUNKS - 1 - i)
        chunk_step(c, 1)
        return carry

    lax.fori_loop(0, N_CHUNKS, bwd_body, 0)


def _ssd(xbc, dt, dtt, lw, e01):
    return pl.pallas_call(
        _ssd_kernel,
        grid=(BATCH,),
        in_specs=[
            pl.BlockSpec((1, T, XBC), lambda b: (b, 0, 0)),
            pl.BlockSpec((1, T, 128), lambda b: (b, 0, 0)),
            pl.BlockSpec((1, 2 * SSM_HEADS, T), lambda b: (b, 0, 0)),
            _const_spec((8, XBC)),
            _const_spec((1, XBC)),
            _const_spec((1, 128)),
            _const_spec((2 * SSM_HEADS, 1)),
            _const_spec((1, 128)),
            _const_spec((2 * SSM_HEADS, 1)),
            _const_spec((1, SSM_INNER)),
            _const_spec((2, 128, SSM_HEADS * 128)),
        ],
        out_specs=pl.BlockSpec((1, T, SSM_INNER), lambda b: (b, 0, 0)),
        out_shape=jax.ShapeDtypeStruct((BATCH, T, SSM_INNER), F32),
        scratch_shapes=[
            pltpu.VMEM((T, XBC), F32),
            pltpu.VMEM((SSM_HEADS // 2, SSM_N, 2 * SSM_P), F32),
        ],
        compiler_params=_params(("arbitrary",)),
        name="ssd_bidir",
    )(xbc, dt, dtt, lw["conv_w"], lw["conv_b"], lw["dt_bias_row"], lw["dt_bias_col"], lw["a_log_row"],
      lw["a_log_col"], lw["d_full"], e01)


def _mixout_kernel(x_ref, mod_ref, a_ref, y_ref, z_ref, uv_ref, gate_ref,
                   wmo_ref, sn_ref, wso_ref, gn_ref, ws_ref, bs_ref, wgo_ref, bg_ref, wout_ref, o_ref):
    x = x_ref[0]
    m = mod_ref[0]
    o_mla = jnp.dot(a_ref[0], wmo_ref[...], preferred_element_type=F32)

    ys = y_ref[0] * _silu(z_ref[0])
    o_ssm = _bdot(_rms(ys, sn_ref[...]), wso_ref[...])

    uv = jax.nn.gelu(uv_ref[0])
    u = uv[:, :GM_WIDTH]
    v = uv[:, GM_WIDTH:]
    v = v - jnp.mean(v, axis=-1, keepdims=True)
    v = v * lax.rsqrt(jnp.mean(v * v, axis=-1, keepdims=True) + EPS) * gn_ref[...]
    lo_half = lax.broadcasted_iota(jnp.int32, (CHUNK, 128), 1) < GM_GDIM
    mixed = []
    for ch in range(TM // CHUNK):
        parts = []
        for p in range(GM_GROUPS // 2):
            vp = v[ch * CHUNK:(ch + 1) * CHUNK, p * 128:(p + 1) * 128]
            vbd = jnp.concatenate([jnp.where(lo_half, vp, 0.0), jnp.where(lo_half, 0.0, vp)], axis=0)
            parts.append(jnp.dot(ws_ref[p], vbd.astype(BF16), preferred_element_type=F32))
        mixed.append(jnp.concatenate(parts, axis=1) + bs_ref[...])
    mixed = jnp.concatenate(mixed, axis=0)
    o_gm = _bdot(u * mixed, wgo_ref[...])

    gts = jax.nn.sigmoid(gate_ref[0] + bg_ref[...])
    merged = gts[:, :D] * o_mla + gts[:, D:2 * D] * o_ssm + gts[:, 2 * D:] * o_gm
    y = _bdot(merged, wout_ref[...])
    o_ref[0] = x + m[5:6] * y


def _mixout(xs, mods, a, y, z, uv, gate, lw, *, t0):
    nt = T // TM - t0
    return pl.pallas_call(
        _mixout_kernel,
        grid=(BATCH, nt),
        in_specs=[
            _row_spec(D, t0),
            _mod_spec(t0),
            _row_spec(HEADS * VDIM, t0),
            _row_spec(SSM_INNER, t0),
            _row_spec(SSM_INNER, t0),
            _row_spec(2 * GM_WIDTH, t0),
            _row_spec(N_BRANCH * D, t0),
            _const_spec((HEADS * VDIM, D)),
            _const_spec((1, SSM_INNER)),
            _const_spec((SSM_INNER, D)),
            _const_spec((1, GM_WIDTH)),
            _const_spec((GM_GROUPS // 2, CHUNK, 2 * CHUNK)),
            _const_spec((CHUNK, GM_WIDTH)),
            _const_spec((GM_WIDTH, D)),
            _const_spec((1, N_BRANCH * D)),
            _const_spec((D, D)),
        ],
        out_specs=pl.BlockSpec((1, TM, D), lambda b, t: (b, t, 0)),
        out_shape=jax.ShapeDtypeStruct((BATCH, nt * TM, D), F32),
        compiler_params=_params(("parallel", "parallel")),
        name="mixer_out",
    )(xs, mods, a, y, z, uv, gate, lw["w_mla_o"], lw["ssm_norm"], lw["w_ssm_o"], lw["gm_norm"], lw["w_s"],
      lw["b_s"], lw["w_gm_o"], lw["b_gate"], lw["w_out"])


def _rot_half(w):
    half = ROPE // 2
    return jnp.concatenate([-w[..., half:], w[..., :half]], axis=-1)


def _head_pad(nope, rope):
    pad = jnp.zeros(nope.shape[:-1] + (HEAD_PAD - NOPE - ROPE,), nope.dtype)
    out = jnp.concatenate([nope, rope, pad], axis=-1)
    return out.reshape(out.shape[:-2] + (HEADS * HEAD_PAD,))


def _layer_weights(l, w_in, mla_q_norm, mla_w_uq, mla_kv_norm, mla_w_ukv, mla_w_o, ssm_conv_w, ssm_conv_b,
                   ssm_a_log, ssm_dt_bias, ssm_d, ssm_norm, ssm_w_o, gm_norm, gm_w_s, gm_b_s, gm_w_o, b_gate, w_out):
    w = w_in[l]
    o = 0
    kv = w[:, o:o + KV_RANK]; o += KV_RANK
    kr = w[:, o:o + ROPE]; o += ROPE
    xbc = w[:, o:o + XBC]; o += XBC
    dtw = w[:, o:o + 2 * SSM_HEADS]; o += 2 * SSM_HEADS
    rest = w[:, o:]

    def lanes(cols, start):
        return jnp.pad(cols, ((0, 0), (start, 128 - start - cols.shape[1])))

    w_new = jnp.concatenate([kv, lanes(kr, NOPE), lanes(_rot_half(kr), NOPE), lanes(dtw, 0), xbc, rest], axis=1)

    uq = mla_w_uq[l].reshape(Q_RANK, HEADS, NOPE + ROPE)
    uq_n, uq_r = uq[..., :NOPE], uq[..., NOPE:]
    ukv = mla_w_ukv[l].reshape(KV_RANK, HEADS, NOPE + VDIM)
    kn = _head_pad(ukv[..., :NOPE], jnp.zeros((KV_RANK, HEADS, ROPE), F32))
    vv = ukv[..., NOPE:].reshape(KV_RANK, HEADS * VDIM)

    ws = gm_w_s[l].astype(BF16)
    ws_pair = jnp.concatenate([ws[0::2], ws[1::2]], axis=2)

    def row128(vals):
        return jnp.pad(vals.reshape(1, -1), ((0, 0), (0, 128 - vals.size)))

    return {
        "w_in": w_new.astype(BF16),
        "w_dt_t": dtw.T.astype(BF16),
        "kv_norm": mla_kv_norm[l].reshape(1, KV_RANK),
        "w_kv": jnp.concatenate([kn, vv], axis=1).astype(BF16),
        "q_norm": mla_q_norm[l].reshape(1, Q_RANK),
        "w_q": _head_pad(uq_n, uq_r).astype(BF16),
        "w_q_rot": _head_pad(jnp.zeros_like(uq_n), _rot_half(uq_r)).astype(BF16),
        "conv_w": jnp.pad(ssm_conv_w[l].T, ((0, 8 - SSM_CONV), (0, 0))),
        "conv_b": ssm_conv_b[l].reshape(1, XBC),
        "dt_bias_row": row128(ssm_dt_bias[l]),
        "dt_bias_col": ssm_dt_bias[l].reshape(2 * SSM_HEADS, 1),
        "a_log_row": row128(ssm_a_log[l]),
        "a_log_col": ssm_a_log[l].reshape(2 * SSM_HEADS, 1),
        "d_full": jnp.repeat(ssm_d[l], SSM_P).reshape(1, SSM_INNER),
        "w_mla_o": mla_w_o[l].astype(BF16),
        "ssm_norm": ssm_norm[l].reshape(1, SSM_INNER),
        "w_ssm_o": ssm_w_o[l].astype(BF16),
        "gm_norm": gm_norm[l].reshape(1, GM_WIDTH),
        "w_s": ws_pair,
        "b_s": jnp.repeat(gm_b_s[l].T, GM_GDIM, axis=1),
        "w_gm_o": gm_w_o[l].astype(BF16),
        "b_gate": b_gate[l].reshape(1, N_BRANCH * D),
        "w_out": w_out[l].astype(BF16),
    }


def _rope_tables():
    rows = SEQ // GRID_W
    r = jnp.repeat(jnp.arange(rows, dtype=F32), GRID_W)
    c = jnp.tile(jnp.arange(GRID_W, dtype=F32), rows)
    n_freq = ROPE // 4
    inv = jnp.power(ROPE_BASE, -jnp.arange(n_freq, dtype=F32) / n_freq)
    ang = jnp.concatenate([r[:, None] * inv, c[:, None] * inv], axis=-1)
    cos, sin = jnp.cos(ang), jnp.sin(ang)
    cos = jnp.concatenate([jnp.ones((CTX, ROPE // 2), F32), cos], axis=0)
    sin = jnp.concatenate([jnp.zeros((CTX, ROPE // 2), F32), sin], axis=0)
    ones = jnp.ones((T, NOPE), F32)
    zeros_n = jnp.zeros((T, NOPE), F32)
    zeros_p = jnp.zeros((T, HEAD_PAD - NOPE - ROPE), F32)
    cos_t = jnp.concatenate([ones, cos, cos, zeros_p], axis=1)
    sin_t = jnp.concatenate([zeros_n, sin, sin, zeros_p], axis=1)
    return cos_t, sin_t


def _head_expand():
    k = jnp.arange(128)[None, :, None]
    hcol = (jnp.arange(SSM_HEADS * 128) // 128)[None, None, :]
    d = jnp.arange(2)[:, None, None]
    return (k == d * SSM_HEADS + hcol).astype(BF16)


def kernel(x, c, ctx, c_ctx, w_ada, b_ada, norm_g, ffn1_w_in, ffn1_w_out, ffn2_w_in, ffn2_w_out, w_in, mla_q_norm,
           mla_w_uq, mla_kv_norm, mla_w_ukv, mla_w_o, ssm_conv_w, ssm_conv_b, ssm_a_log, ssm_dt_bias, ssm_d,
           ssm_norm, ssm_w_o, gm_norm, gm_w_s, gm_b_s, gm_w_o, b_gate, w_out, final_norm):
    xs = jnp.concatenate([ctx, x], axis=1)
    c_all = jnp.concatenate([c, c_ctx[None, :], jnp.zeros((MOD_ROWS - BATCH - 1, D), F32)], axis=0)
    mods_all = _ada(c_all, w_ada, b_ada).reshape(DEPTH, MOD_ROWS, N_MOD, D)
    cos_t, sin_t = _rope_tables()
    e01 = _head_expand()
    fin = final_norm.reshape(1, D)

    for l in range(DEPTH):
        last = l == DEPTH - 1
        t0 = 1 if last else 0
        mods = mods_all[l]
        lw = _layer_weights(l, w_in, mla_q_norm, mla_w_uq, mla_kv_norm, mla_w_ukv, mla_w_o, ssm_conv_w, ssm_conv_b,
                            ssm_a_log, ssm_dt_bias, ssm_d, ssm_norm, ssm_w_o, gm_norm, gm_w_s, gm_b_s, gm_w_o,
                            b_gate, w_out)
        g = norm_g[l]
        xs = _ffn(xs, mods, g[0:1], ffn1_w_in[l].astype(BF16), ffn1_w_out[l].astype(BF16), fin,
                  mod0=0, t0=0, final=False)
        q, k, v, xbc, dt, dtt, z, uv, gate = _proj(xs, mods, g[1:2], lw, cos_t, sin_t)
        a = _attn(q, k, v, t0=t0)
        y = _ssd(xbc, dt, dtt, lw, e01)
        xm = _mixout(xs, mods, a, y, z, uv, gate, lw, t0=t0)
        if last:
            xs = _ffn_latent(xm, mods, g[2:3], ffn2_w_in[l].astype(BF16), ffn2_w_out[l].astype(BF16), fin)
        else:
            xs = _ffn(xm, mods, g[2:3], ffn2_w_in[l].astype(BF16), ffn2_w_out[l].astype(BF16), fin,
                      mod0=6, t0=0, final=False)
    return xs


def _ffn_latent(xl, mods, g, wi, wo, fin):
    nt = SEQ // TM
    return pl.pallas_call(
        functools.partial(_ffn_kernel, mod0=6, final=True),
        grid=(BATCH, nt),
        in_specs=[
            pl.BlockSpec((1, TM, D), lambda b, t: (b, t, 0)),
            pl.BlockSpec((1, N_MOD, D), lambda b, t: (b, 0, 0)),
            _const_spec((1, D)),
            _const_spec((D, 2 * D_FF)),
            _const_spec((D_FF, D)),
            _const_spec((1, D)),
        ],
        out_specs=pl.BlockSpec((1, TM, D), lambda b, t: (b, t, 0)),
        out_shape=jax.ShapeDtypeStruct((BATCH, SEQ, D), F32),
        compiler_params=_params(("parallel", "parallel")),
        name="ffn_final",
    )(xl, mods, g, wi, wo, fin)
```

```python
import functools
import math

import jax
import jax.numpy as jnp
from jax import lax
from jax.experimental import pallas as pl
from jax.experimental.pallas import tpu as pltpu

F32 = jnp.float32
BF16 = jnp.bfloat16

D = 1024
BATCH = 8
SEQ = 2048
DEPTH = 2
CTX = 256
T = CTX + SEQ
GRID_W = 64
EPS = 1e-6

HEADS = 8
NOPE = 64
ROPE = 32
VDIM = 64
Q_RANK = 256
KV_RANK = 128
ROPE_BASE = 10000.0
ATTN_SCALE = (NOPE + ROPE) ** -0.5
EXP2_SCALE = ATTN_SCALE * math.log2(math.e)
HEAD_PAD = 128
V_ROWS = 80

SSM_HEADS = 8
SSM_P = 64
SSM_INNER = SSM_HEADS * SSM_P
SSM_GROUPS = 2
SSM_N = 128
SSM_CONV = 5
CHUNK = 128
XBC = SSM_INNER + 2 * SSM_GROUPS * SSM_N
N_CHUNKS = T // CHUNK
CTX_CHUNKS = CTX // CHUNK
PAD_ROWS = 8

GM_GROUPS = 8
GM_WIDTH = 512
GM_GDIM = GM_WIDTH // GM_GROUPS

D_FF = 2816
N_BRANCH = 3
N_MOD = 9
MOD_ROWS = 16
CTX_MOD_ROW = BATCH

KV_SIDE = KV_RANK + ROPE + XBC + 2 * SSM_HEADS

C_KV = 0
C_KR = 128
C_KRR = 256
C_DT = 384
C_XBC = 512
C_Q = C_XBC + XBC
C_Z = C_Q + Q_RANK
C_UV = C_Z + SSM_INNER
C_GATE = C_UV + 2 * GM_WIDTH
C_END = C_GATE + N_BRANCH * D

TM = 256
VMEM_LIMIT = 56 * 1024 * 1024


def _rms(x, g):
    y = x * lax.rsqrt(jnp.mean(x * x, axis=-1, keepdims=True) + EPS)
    return y * g


def _silu(x):
    return x * jax.nn.sigmoid(x)


def _bdot(a, b):
    return jnp.dot(a.astype(BF16), b.astype(BF16), preferred_element_type=F32)


def _split3(a):
    a1 = a.astype(BF16)
    r1 = a - a1.astype(F32)
    a2 = r1.astype(BF16)
    a3 = (r1 - a2.astype(F32)).astype(BF16)
    return a1, a2, a3


def _dot_right01(a, m01):
    return sum(jnp.dot(p, m01, preferred_element_type=F32) for p in _split3(a))


def _dot_left01(m01, a):
    return sum(jnp.dot(m01, p, preferred_element_type=F32) for p in _split3(a))


def _const_spec(shape):
    nd = len(shape)
    return pl.BlockSpec(shape, lambda *_: (0,) * nd, pipeline_mode=pl.Buffered(1))


def _mod_spec(t0):
    return pl.BlockSpec((1, N_MOD, D), lambda b, t: (jnp.where(t + t0 == 0, CTX_MOD_ROW, b), 0, 0))


def _row_spec(width, t0):
    return pl.BlockSpec((1, TM, width), lambda b, t: (b, t + t0, 0))


def _params(sem):
    return pltpu.CompilerParams(dimension_semantics=sem, vmem_limit_bytes=VMEM_LIMIT)


def _ada_kernel(c_ref, w_ref, b_ref, o_ref):
    s = _silu(c_ref[...])
    o_ref[0] = _bdot(s, w_ref[0]) + b_ref[0]


def _ada(c_all, w_ada, b_ada):
    tn = 1152
    nn = N_MOD * D // tn
    return pl.pallas_call(
        _ada_kernel,
        grid=(DEPTH, nn),
        in_specs=[
            pl.BlockSpec((MOD_ROWS, D), lambda l, j: (0, 0)),
            pl.BlockSpec((1, D, tn), lambda l, j: (l, 0, j)),
            pl.BlockSpec((1, 1, tn), lambda l, j: (l, 0, j)),
        ],
        out_specs=pl.BlockSpec((1, MOD_ROWS, tn), lambda l, j: (l, 0, j)),
        out_shape=jax.ShapeDtypeStruct((DEPTH, MOD_ROWS, N_MOD * D), F32),
        compiler_params=_params(("arbitrary", "arbitrary")),
        name="ada_mod",
    )(c_all, w_ada, b_ada.reshape(DEPTH, 1, N_MOD * D))


def _ffn_kernel(x_ref, mod_ref, g_ref, wi_ref, wo_ref, fin_ref, o_ref, *, mod0, final):
    x = x_ref[0]
    m = mod_ref[0]
    shift, scale, gate = m[mod0:mod0 + 1], m[mod0 + 1:mod0 + 2], m[mod0 + 2:mod0 + 3]
    h = _rms(x, g_ref[...]) * (1.0 + scale) + shift
    gu = jnp.dot(h.astype(BF16), wi_ref[...], preferred_element_type=F32)
    a = _silu(gu[:, :D_FF]) * gu[:, D_FF:]
    y = jnp.dot(a.astype(BF16), wo_ref[...], preferred_element_type=F32)
    out = x + (0.5 * gate) * y
    if final:
        out = _rms(out, fin_ref[...])
    o_ref[0] = out


def _ffn(xs, mods, g, wi, wo, fin, *, mod0, t0, final):
    nt = T // TM - t0
    out_rows = nt * TM
    return pl.pallas_call(
        functools.partial(_ffn_kernel, mod0=mod0, final=final),
        grid=(BATCH, nt),
        in_specs=[
            _row_spec(D, t0),
            _mod_spec(t0),
            _const_spec((1, D)),
            _const_spec((D, 2 * D_FF)),
            _const_spec((D_FF, D)),
            _const_spec((1, D)),
        ],
        out_specs=pl.BlockSpec((1, TM, D), lambda b, t: (b, t, 0)),
        out_shape=jax.ShapeDtypeStruct((BATCH, out_rows, D), F32),
        compiler_params=_params(("parallel", "parallel")),
        name="ffn",
    )(xs, mods, g, wi, wo, fin)


def _proj_kernel(x_ref, mod_ref, g_ref, win_ref, wdt_ref, kvn_ref, wkn_ref, wvt_ref, one_ref, qn_ref, wq_ref, wqr_ref,
                 cos_ref, sin_ref,
                 q_out, k_out, vt_out, xbc_out, dt_out, dtt_out, z_out, uv_out, gate_out):
    x = x_ref[0]
    m = mod_ref[0]
    h = _rms(x, g_ref[...]) * (1.0 + m[4:5]) + m[3:4]
    hb = h.astype(BF16)
    cos = cos_ref[...]
    sin = sin_ref[...]

    def proj(c0, c1):
        return jnp.dot(hb, win_ref[:, c0:c1], preferred_element_type=F32)

    small = proj(C_KV, C_XBC)
    xbc_out[0] = proj(C_XBC, C_Q).astype(BF16)
    q_lat = proj(C_Q, C_Z)
    z_out[0] = proj(C_Z, C_UV).astype(BF16)
    uv_out[0] = proj(C_UV, C_GATE).astype(BF16)
    gate_out[0] = proj(C_GATE, C_END).astype(BF16)

    dt_out[0] = small[:, C_DT:C_DT + 128]
    dtt_out[0] = lax.dot_general(wdt_ref[...], hb, (((1,), (1,)), ((), ())), preferred_element_type=F32)

    kvn = _rms(small[:, C_KV:C_KV + 128], kvn_ref[...]).astype(BF16)
    kn = jnp.dot(kvn, wkn_ref[...], preferred_element_type=F32)
    kr = small[:, C_KR:C_KR + 128] * cos + small[:, C_KRR:C_KRR + 128] * sin
    for hd in range(HEADS):
        sl = slice(hd * HEAD_PAD, (hd + 1) * HEAD_PAD)
        k_out[0, :, sl] = (kn[:, sl] + kr).astype(BF16)
    vt = lax.dot_general(wvt_ref[...], kvn, (((1,), (1,)), ((), ())), preferred_element_type=F32)
    vt_out[0] = (vt + one_ref[...]).astype(BF16)

    qn = _rms(q_lat, qn_ref[...]).astype(BF16)
    qa = jnp.dot(qn, wq_ref[...], preferred_element_type=F32)
    qb = jnp.dot(qn, wqr_ref[...], preferred_element_type=F32)
    for hd in range(HEADS):
        sl = slice(hd * HEAD_PAD, (hd + 1) * HEAD_PAD)
        q_out[0, :, sl] = (qa[:, sl] * cos + qb[:, sl] * sin).astype(BF16)


def _proj(xs, mods, g, lw, cos_t, sin_t):
    nt = T // TM
    def rows(w, dt):
        return jax.ShapeDtypeStruct((BATCH, T, w), dt), _row_spec(w, 0)

    def cols(r, dt):
        return jax.ShapeDtypeStruct((BATCH, r, T), dt), pl.BlockSpec((1, r, TM), lambda b, t: (b, 0, t))

    outs = [rows(HEADS * HEAD_PAD, BF16), rows(HEADS * HEAD_PAD, BF16), cols(HEADS * V_ROWS, BF16),
            rows(XBC, BF16), rows(128, F32), cols(2 * SSM_HEADS, F32),
            rows(SSM_INNER, BF16), rows(2 * GM_WIDTH, BF16), rows(N_BRANCH * D, BF16)]
    out_shape = [o[0] for o in outs]
    out_specs = [o[1] for o in outs]
    return pl.pallas_call(
        _proj_kernel,
        grid=(BATCH, nt),
        in_specs=[
            _row_spec(D, 0),
            _mod_spec(0),
            _const_spec((1, D)),
            _const_spec((D, C_END)),
            _const_spec((2 * SSM_HEADS, D)),
            _const_spec((1, KV_RANK)),
            _const_spec((KV_RANK, HEADS * HEAD_PAD)),
            _const_spec((HEADS * V_ROWS, KV_RANK)),
            _const_spec((HEADS * V_ROWS, 1)),
            _const_spec((1, Q_RANK)),
            _const_spec((Q_RANK, HEADS * HEAD_PAD)),
            _const_spec((Q_RANK, HEADS * HEAD_PAD)),
            pl.BlockSpec((TM, HEAD_PAD), lambda b, t: (t, 0)),
            pl.BlockSpec((TM, HEAD_PAD), lambda b, t: (t, 0)),
        ],
        out_specs=out_specs,
        out_shape=out_shape,
        compiler_params=_params(("parallel", "parallel")),
        name="mixer_proj",
    )(xs, mods, g, lw["w_in"], lw["w_dt_t"], lw["kv_norm"], lw["w_kn"], lw["w_v_t"], lw["v_ones"], lw["q_norm"],
      lw["w_q"], lw["w_q_rot"], cos_t, sin_t)


def _attn_kernel(q_ref, k_ref, vt_ref, o_ref, *, t0):
    def run(nk):
        nblk = nk // TM

        def score_block(hd, j):
            q = q_ref[0, :, hd * HEAD_PAD:(hd + 1) * HEAD_PAD]
            k = k_ref[0, j * TM:(j + 1) * TM, hd * HEAD_PAD:(hd + 1) * HEAD_PAD]
            return lax.dot_general(k, q, (((1,), (1,)), ((), ())), preferred_element_type=F32)

        def col_max(blocks):
            mx = functools.reduce(jnp.maximum, blocks)
            mx = jnp.max(mx.reshape(TM // 8, 8, TM), axis=0)
            return jnp.max(mx, axis=0, keepdims=True)

        outs = []
        cur = [score_block(0, j) for j in range(nblk)]
        for hd in range(HEADS):
            mx = col_max(cur)
            nxt, ps = [], []
            for j in range(nblk):
                if hd + 1 < HEADS:
                    nxt.append(score_block(hd + 1, j))
                ps.append(jnp.exp2((cur[j] - mx) * EXP2_SCALE).astype(BF16))
            p = jnp.concatenate(ps, axis=0)
            ot = jnp.dot(vt_ref[0, hd * V_ROWS:(hd + 1) * V_ROWS, :nk], p, preferred_element_type=F32)
            outs.append(ot[:VDIM] / ot[VDIM:VDIM + 1])
            cur = nxt
        o_ref[0] = jnp.concatenate(outs, axis=0).T.astype(BF16)

    if t0 == 0:
        t = pl.program_id(1)
        pl.when(t == 0)(lambda: run(CTX))
        pl.when(t != 0)(lambda: run(T))
    else:
        run(T)


def _attn(q, k, v, *, t0):
    nt = T // TM - t0
    return pl.pallas_call(
        functools.partial(_attn_kernel, t0=t0),
        grid=(BATCH, nt),
        in_specs=[
            _row_spec(HEADS * HEAD_PAD, t0),
            pl.BlockSpec((1, T, HEADS * HEAD_PAD), lambda b, t: (b, 0, 0)),
            pl.BlockSpec((1, HEADS * V_ROWS, T), lambda b, t: (b, 0, 0)),
        ],
        out_specs=_row_spec(HEADS * VDIM, t0),
        out_shape=jax.ShapeDtypeStruct((BATCH, T, HEADS * VDIM), BF16),
        compiler_params=_params(("parallel", "arbitrary")),
        name="mla_attn",
    )(q, k, v)


def _ssd_kernel(xbc_ref, dt_ref, dtt_ref, cw_ref, cb_ref, bias_row_ref, bias_col_ref, alog_row_ref, alog_col_ref,
                dfull_ref, y_ref, xp_ref, xc_ref, h_ref):
    def padded_row(c):
        return pl.multiple_of(c * CHUNK + jnp.where(c < CTX_CHUNKS, PAD_ROWS, 2 * PAD_ROWS), PAD_ROWS)

    zeros_pad = jnp.zeros((PAD_ROWS, XBC), F32)
    xp_ref[0:PAD_ROWS, :] = zeros_pad
    xp_ref[PAD_ROWS + CTX:2 * PAD_ROWS + CTX, :] = zeros_pad
    xp_ref[2 * PAD_ROWS + T:3 * PAD_ROWS + T, :] = zeros_pad

    def copy_body(c, carry):
        r0 = pl.multiple_of(c * CHUNK, CHUNK)
        xp_ref[pl.ds(padded_row(c), CHUNK), :] = xbc_ref[0, pl.ds(r0, CHUNK), :].astype(F32)
        return carry

    lax.fori_loop(0, N_CHUNKS, copy_body, 0)

    def conv_body(c, carry):
        r0 = pl.multiple_of(c * CHUNK, CHUNK)
        w0 = pl.multiple_of(padded_row(c) - PAD_ROWS, PAD_ROWS)
        win = xp_ref[pl.ds(w0, CHUNK + 2 * PAD_ROWS), :]
        acc = cb_ref[...]
        for k in range(SSM_CONV):
            off = PAD_ROWS + k - SSM_CONV // 2
            acc = acc + cw_ref[k:k + 1, :] * win[off:off + CHUNK]
        xc_ref[pl.ds(r0, CHUNK), :] = _silu(acc)
        return carry

    lax.fori_loop(0, N_CHUNKS, conv_body, 0)

    row = lax.broadcasted_iota(jnp.int32, (CHUNK, CHUNK), 0)
    col = lax.broadcasted_iota(jnp.int32, (CHUNK, CHUNK), 1)
    lower = (col <= row)
    upper = (col >= row)
    lower01 = jnp.where(lower, 1.0, 0.0).astype(BF16)
    upper01 = jnp.where(upper, 1.0, 0.0).astype(BF16)
    lo_half = lax.broadcasted_iota(jnp.int32, (CHUNK, 128), 1) < SSM_P
    a_row = -jnp.exp(alog_row_ref[...])
    a_col = -jnp.exp(alog_col_ref[...])

    def chunk_step(c, d):
        r0 = pl.multiple_of(c * CHUNK, CHUNK)
        xc = xc_ref[pl.ds(r0, CHUNK), :]
        dcol = jax.nn.softplus(dt_ref[0, pl.ds(r0, CHUNK), :] + bias_row_ref[...])
        drow = jax.nn.softplus(dtt_ref[0, :, pl.ds(r0, CHUNK)] + bias_col_ref[...])
        tri_l, tri_r, mask = (lower01, upper01, lower) if d == 0 else (upper01, lower01, upper)
        acum_c = _dot_left01(tri_l, dcol * a_row)
        acum_r = _dot_right01(drow * a_col, tri_r)
        atot_c = acum_c[CHUNK - 1:CHUNK] if d == 0 else acum_c[0:1]
        w_c = jnp.exp(atot_c - acum_c) * dcol
        etot_c = jnp.exp(atot_c)

        def head_col(arr, hd):
            lane = d * SSM_HEADS + hd
            return arr[:, lane:lane + 1]

        def pair_lanes(arr, p):
            return jnp.where(lo_half[:arr.shape[0]], head_col(arr, 2 * p), head_col(arr, 2 * p + 1))

        ys = []
        for g in range(SSM_GROUPS):
            bm = xc[:, SSM_INNER + g * SSM_N:SSM_INNER + (g + 1) * SSM_N]
            cm = xc[:, SSM_INNER + (SSM_GROUPS + g) * SSM_N:SSM_INNER + (SSM_GROUPS + g + 1) * SSM_N]
            bmb = bm.astype(BF16)
            cmb = cm.astype(BF16)
            cb = lax.dot_general(cmb, bmb, (((1,), (1,)), ((), ())), preferred_element_type=F32)
            bt = bm.T.astype(BF16)
            for pp in range(2):
                p = g * 2 + pp
                sc, a_bc = [], []
                for hh in range(2):
                    hd = 2 * p + hh
                    a_bc.append(jnp.broadcast_to(head_col(acum_c, hd), (CHUNK, CHUNK)))
                    diff = a_bc[hh] - acum_r[d * SSM_HEADS + hd:d * SSM_HEADS + hd + 1]
                    dec = jnp.exp(jnp.where(mask, diff, -jnp.inf))
                    sc.append(cb * dec * drow[d * SSM_HEADS + hd:d * SSM_HEADS + hd + 1])
                x_p = xc[:, p * 128:(p + 1) * 128]
                xw = (x_p * pair_lanes(w_c, p)).astype(BF16)
                s_t = jnp.dot(bt, xw, preferred_element_type=F32)
                h_prev = h_ref[d, p]
                e_p = jnp.exp(jnp.where(lo_half, a_bc[0], a_bc[1]))
                y_inter = jnp.dot(cmb, h_prev.astype(BF16), preferred_element_type=F32) * e_p
                pmat = jnp.concatenate(sc, axis=1).astype(BF16)
                xbd = jnp.concatenate([jnp.where(lo_half, x_p, 0.0), jnp.where(lo_half, 0.0, x_p)],
                                      axis=0).astype(BF16)
                ys.append(jnp.dot(pmat, xbd, preferred_element_type=F32) + y_inter)
                h_ref[d, p] = h_prev * pair_lanes(etot_c, p) + s_t
        y = jnp.concatenate(ys, axis=1)
        if d == 0:
            y = y + dfull_ref[...] * xc[:, :SSM_INNER]
        y_ref[0, pl.ds(r0, CHUNK), :] += y

    h_ref[...] = jnp.zeros_like(h_ref)
    y_ref[...] = jnp.zeros_like(y_ref)

    def scan_body(i, carry):
        chunk_step(i, 0)
        chunk_step(jnp.where(i < CTX_CHUNKS, CTX_CHUNKS - 1 - i, N_CHUNKS + CTX_CHUNKS - 1 - i), 1)
        return carry

    lax.fori_loop(0, N_CHUNKS, scan_body, 0, unroll=2)


def _ssd(xbc, dt, dtt, lw):
    return pl.pallas_call(
        _ssd_kernel,
        grid=(BATCH,),
        in_specs=[
            pl.BlockSpec((1, T, XBC), lambda b: (b, 0, 0)),
            pl.BlockSpec((1, T, 128), lambda b: (b, 0, 0)),
            pl.BlockSpec((1, 2 * SSM_HEADS, T), lambda b: (b, 0, 0)),
            _const_spec((8, XBC)),
            _const_spec((1, XBC)),
            _const_spec((1, 128)),
            _const_spec((2 * SSM_HEADS, 1)),
            _const_spec((1, 128)),
            _const_spec((2 * SSM_HEADS, 1)),
            _const_spec((1, SSM_INNER)),
        ],
        out_specs=pl.BlockSpec((1, T, SSM_INNER), lambda b: (b, 0, 0)),
        out_shape=jax.ShapeDtypeStruct((BATCH, T, SSM_INNER), F32),
        scratch_shapes=[
            pltpu.VMEM((T + 3 * PAD_ROWS, XBC), F32),
            pltpu.VMEM((T, XBC), F32),
            pltpu.VMEM((2, SSM_HEADS // 2, SSM_N, 2 * SSM_P), F32),
        ],
        compiler_params=_params(("arbitrary",)),
        name="ssd_bidir",
    )(xbc, dt, dtt, lw["conv_w"], lw["conv_b"], lw["dt_bias_row"], lw["dt_bias_col"], lw["a_log_row"],
      lw["a_log_col"], lw["d_full"])


def _mixout_kernel(x_ref, mod_ref, a_ref, y_ref, z_ref, uv_ref, gate_ref,
                   wmo_ref, sn_ref, wso_ref, gn_ref, ws_ref, bs_ref, wgo_ref, bg_ref, wout_ref, o_ref):
    x = x_ref[0]
    m = mod_ref[0]
    o_mla = jnp.dot(a_ref[0], wmo_ref[...], preferred_element_type=F32)

    ys = y_ref[0] * _silu(z_ref[0].astype(F32))
    o_ssm = _bdot(_rms(ys, sn_ref[...]), wso_ref[...])

    uv = jax.nn.gelu(uv_ref[0].astype(F32))
    u = uv[:, :GM_WIDTH]
    v = uv[:, GM_WIDTH:]
    v = v - jnp.mean(v, axis=-1, keepdims=True)
    v = v * lax.rsqrt(jnp.mean(v * v, axis=-1, keepdims=True) + EPS) * gn_ref[...]
    lo_half = lax.broadcasted_iota(jnp.int32, (CHUNK, 128), 1) < GM_GDIM
    mixed = []
    for ch in range(TM // CHUNK):
        parts = []
        for p in range(GM_GROUPS // 2):
            vp = v[ch * CHUNK:(ch + 1) * CHUNK, p * 128:(p + 1) * 128]
            vbd = jnp.concatenate([jnp.where(lo_half, vp, 0.0), jnp.where(lo_half, 0.0, vp)], axis=0)
            parts.append(jnp.dot(ws_ref[p], vbd.astype(BF16), preferred_element_type=F32))
        mixed.append(jnp.concatenate(parts, axis=1) + bs_ref[...])
    mixed = jnp.concatenate(mixed, axis=0)
    o_gm = _bdot(u * mixed, wgo_ref[...])

    gts = jax.nn.sigmoid(gate_ref[0].astype(F32) + bg_ref[...])
    merged = gts[:, :D] * o_mla + gts[:, D:2 * D] * o_ssm + gts[:, 2 * D:] * o_gm
    y = _bdot(merged, wout_ref[...])
    o_ref[0] = x + m[5:6] * y


def _mixout(xs, mods, a, y, z, uv, gate, lw, *, t0):
    nt = T // TM - t0
    return pl.pallas_call(
        _mixout_kernel,
        grid=(BATCH, nt),
        in_specs=[
            _row_spec(D, t0),
            _mod_spec(t0),
            _row_spec(HEADS * VDIM, t0),
            _row_spec(SSM_INNER, t0),
            _row_spec(SSM_INNER, t0),
            _row_spec(2 * GM_WIDTH, t0),
            _row_spec(N_BRANCH * D, t0),
            _const_spec((HEADS * VDIM, D)),
            _const_spec((1, SSM_INNER)),
            _const_spec((SSM_INNER, D)),
            _const_spec((1, GM_WIDTH)),
            _const_spec((GM_GROUPS // 2, CHUNK, 2 * CHUNK)),
            _const_spec((CHUNK, GM_WIDTH)),
            _const_spec((GM_WIDTH, D)),
            _const_spec((1, N_BRANCH * D)),
            _const_spec((D, D)),
        ],
        out_specs=pl.BlockSpec((1, TM, D), lambda b, t: (b, t, 0)),
        out_shape=jax.ShapeDtypeStruct((BATCH, nt * TM, D), F32),
        compiler_params=_params(("parallel", "parallel")),
        name="mixer_out",
    )(xs, mods, a, y, z, uv, gate, lw["w_mla_o"], lw["ssm_norm"], lw["w_ssm_o"], lw["gm_norm"], lw["w_s"],
      lw["b_s"], lw["w_gm_o"], lw["b_gate"], lw["w_out"])


def _rot_half(w):
    half = ROPE // 2
    return jnp.concatenate([-w[..., half:], w[..., :half]], axis=-1)


def _head_pad(nope, rope):
    pad = jnp.zeros(nope.shape[:-1] + (HEAD_PAD - NOPE - ROPE,), nope.dtype)
    out = jnp.concatenate([nope, rope, pad], axis=-1)
    return out.reshape(out.shape[:-2] + (HEADS * HEAD_PAD,))


def _layer_weights(l, w_in, mla_q_norm, mla_w_uq, mla_kv_norm, mla_w_ukv, mla_w_o, ssm_conv_w, ssm_conv_b,
                   ssm_a_log, ssm_dt_bias, ssm_d, ssm_norm, ssm_w_o, gm_norm, gm_w_s, gm_b_s, gm_w_o, b_gate, w_out):
    w = w_in[l]
    o = 0
    kv = w[:, o:o + KV_RANK]; o += KV_RANK
    kr = w[:, o:o + ROPE]; o += ROPE
    xbc = w[:, o:o + XBC]; o += XBC
    dtw = w[:, o:o + 2 * SSM_HEADS]; o += 2 * SSM_HEADS
    rest = w[:, o:]

    def lanes(cols, start):
        return jnp.pad(cols, ((0, 0), (start, 128 - start - cols.shape[1])))

    w_new = jnp.concatenate([kv, lanes(kr, NOPE), lanes(_rot_half(kr), NOPE), lanes(dtw, 0), xbc, rest], axis=1)

    uq = mla_w_uq[l].reshape(Q_RANK, HEADS, NOPE + ROPE)
    uq_n, uq_r = uq[..., :NOPE], uq[..., NOPE:]
    ukv = mla_w_ukv[l].reshape(KV_RANK, HEADS, NOPE + VDIM)
    kn = _head_pad(ukv[..., :NOPE], jnp.zeros((KV_RANK, HEADS, ROPE), F32))
    v_t = jnp.pad(jnp.transpose(ukv[..., NOPE:], (1, 2, 0)), ((0, 0), (0, V_ROWS - VDIM), (0, 0)))
    v_ones = jnp.tile((jnp.arange(V_ROWS) == VDIM).astype(F32), HEADS).reshape(HEADS * V_ROWS, 1)

    ws = gm_w_s[l].astype(BF16)
    ws_pair = jnp.concatenate([ws[0::2], ws[1::2]], axis=2)

    def row128(vals, fill=0.0):
        return jnp.pad(vals.reshape(1, -1), ((0, 0), (0, 128 - vals.size)), constant_values=fill)

    return {
        "w_in": w_new.astype(BF16),
        "w_dt_t": dtw.T.astype(BF16),
        "kv_norm": mla_kv_norm[l].reshape(1, KV_RANK),
        "w_kn": kn.astype(BF16),
        "w_v_t": v_t.reshape(HEADS * V_ROWS, KV_RANK).astype(BF16),
        "v_ones": v_ones,
        "q_norm": mla_q_norm[l].reshape(1, Q_RANK),
        "w_q": _head_pad(uq_n, uq_r).astype(BF16),
        "w_q_rot": _head_pad(jnp.zeros_like(uq_n), _rot_half(uq_r)).astype(BF16),
        "conv_w": jnp.pad(ssm_conv_w[l].T, ((0, 8 - SSM_CONV), (0, 0))),
        "conv_b": ssm_conv_b[l].reshape(1, XBC),
        "dt_bias_row": row128(ssm_dt_bias[l]),
        "dt_bias_col": ssm_dt_bias[l].reshape(2 * SSM_HEADS, 1),
        "a_log_row": row128(ssm_a_log[l], fill=-80.0),
        "a_log_col": ssm_a_log[l].reshape(2 * SSM_HEADS, 1),
        "d_full": jnp.repeat(ssm_d[l], SSM_P).reshape(1, SSM_INNER),
        "w_mla_o": mla_w_o[l].astype(BF16),
        "ssm_norm": ssm_norm[l].reshape(1, SSM_INNER),
        "w_ssm_o": ssm_w_o[l].astype(BF16),
        "gm_norm": gm_norm[l].reshape(1, GM_WIDTH),
        "w_s": ws_pair,
        "b_s": jnp.repeat(gm_b_s[l].T, GM_GDIM, axis=1),
        "w_gm_o": gm_w_o[l].astype(BF16),
        "b_gate": b_gate[l].reshape(1, N_BRANCH * D),
        "w_out": w_out[l].astype(BF16),
    }


def _rope_tables():
    rows = SEQ // GRID_W
    r = jnp.repeat(jnp.arange(rows, dtype=F32), GRID_W)
    c = jnp.tile(jnp.arange(GRID_W, dtype=F32), rows)
    n_freq = ROPE // 4
    inv = jnp.power(ROPE_BASE, -jnp.arange(n_freq, dtype=F32) / n_freq)
    ang = jnp.concatenate([r[:, None] * inv, c[:, None] * inv], axis=-1)
    cos, sin = jnp.cos(ang), jnp.sin(ang)
    cos = jnp.concatenate([jnp.ones((CTX, ROPE // 2), F32), cos], axis=0)
    sin = jnp.concatenate([jnp.zeros((CTX, ROPE // 2), F32), sin], axis=0)
    ones = jnp.ones((T, NOPE), F32)
    zeros_n = jnp.zeros((T, NOPE), F32)
    zeros_p = jnp.zeros((T, HEAD_PAD - NOPE - ROPE), F32)
    cos_t = jnp.concatenate([ones, cos, cos, zeros_p], axis=1)
    sin_t = jnp.concatenate([zeros_n, sin, sin, zeros_p], axis=1)
    return cos_t, sin_t


def kernel(x, c, ctx, c_ctx, w_ada, b_ada, norm_g, ffn1_w_in, ffn1_w_out, ffn2_w_in, ffn2_w_out, w_in, mla_q_norm,
           mla_w_uq, mla_kv_norm, mla_w_ukv, mla_w_o, ssm_conv_w, ssm_conv_b, ssm_a_log, ssm_dt_bias, ssm_d,
           ssm_norm, ssm_w_o, gm_norm, gm_w_s, gm_b_s, gm_w_o, b_gate, w_out, final_norm):
    xs = jnp.concatenate([ctx, x], axis=1)
    c_all = jnp.concatenate([c, c_ctx[None, :], jnp.zeros((MOD_ROWS - BATCH - 1, D), F32)], axis=0)
    mods_all = _ada(c_all, w_ada, b_ada).reshape(DEPTH, MOD_ROWS, N_MOD, D)
    cos_t, sin_t = _rope_tables()
    fin = final_norm.reshape(1, D)

    for l in range(DEPTH):
        last = l == DEPTH - 1
        t0 = 1 if last else 0
        mods = mods_all[l]
        lw = _layer_weights(l, w_in, mla_q_norm, mla_w_uq, mla_kv_norm, mla_w_ukv, mla_w_o, ssm_conv_w, ssm_conv_b,
                            ssm_a_log, ssm_dt_bias, ssm_d, ssm_norm, ssm_w_o, gm_norm, gm_w_s, gm_b_s, gm_w_o,
                            b_gate, w_out)
        g = norm_g[l]
        xs = _ffn(xs, mods, g[0:1], ffn1_w_in[l].astype(BF16), ffn1_w_out[l].astype(BF16), fin,
                  mod0=0, t0=0, final=False)
        q, k, vt, xbc, dt, dtt, z, uv, gate = _proj(xs, mods, g[1:2], lw, cos_t, sin_t)
        a = _attn(q, k, vt, t0=t0)
        y = _ssd(xbc, dt, dtt, lw)
        xm = _mixout(xs, mods, a, y, z, uv, gate, lw, t0=t0)
        if last:
            xs = _ffn_latent(xm, mods, g[2:3], ffn2_w_in[l].astype(BF16), ffn2_w_out[l].astype(BF16), fin)
        else:
            xs = _ffn(xm, mods, g[2:3], ffn2_w_in[l].astype(BF16), ffn2_w_out[l].astype(BF16), fin,
                      mod0=6, t0=0, final=False)
    return xs


def _ffn_latent(xl, mods, g, wi, wo, fin):
    nt = SEQ // TM
    return pl.pallas_call(
        functools.partial(_ffn_kernel, mod0=6, final=True),
        grid=(BATCH, nt),
        in_specs=[
            pl.BlockSpec((1, TM, D), lambda b, t: (b, t, 0)),
            pl.BlockSpec((1, N_MOD, D), lambda b, t: (b, 0, 0)),
            _const_spec((1, D)),
            _const_spec((D, 2 * D_FF)),
            _const_spec((D_FF, D)),
            _const_spec((1, D)),
        ],
        out_specs=pl.BlockSpec((1, TM, D), lambda b, t: (b, t, 0)),
        out_shape=jax.ShapeDtypeStruct((BATCH, SEQ, D), F32),
        compiler_params=_params(("parallel", "parallel")),
        name="ffn_final",
    )(xl, mods, g, wi, wo, fin)
```

```python
import functools
import math

import jax
import jax.numpy as jnp
from jax import lax
from jax.experimental import pallas as pl
from jax.experimental.pallas import tpu as pltpu

F32 = jnp.float32
BF16 = jnp.bfloat16

D = 1024
BATCH = 8
SEQ = 2048
DEPTH = 2
CTX = 256
T = CTX + SEQ
GRID_W = 64
EPS = 1e-6

HEADS = 8
NOPE = 64
ROPE = 32
VDIM = 64
Q_RANK = 256
KV_RANK = 128
ROPE_BASE = 10000.0
ATTN_SCALE = (NOPE + ROPE) ** -0.5
EXP2_SCALE = ATTN_SCALE * math.log2(math.e)
HEAD_PAD = 128
V_ROWS = 80

SSM_HEADS = 8
SSM_P = 64
SSM_INNER = SSM_HEADS * SSM_P
SSM_GROUPS = 2
SSM_N = 128
SSM_CONV = 5
CHUNK = 128
XBC = SSM_INNER + 2 * SSM_GROUPS * SSM_N
N_CHUNKS = T // CHUNK
CTX_CHUNKS = CTX // CHUNK
PAD_ROWS = 8

GM_GROUPS = 8
GM_WIDTH = 512
GM_GDIM = GM_WIDTH // GM_GROUPS

D_FF = 2816
N_BRANCH = 3
N_MOD = 9
MOD_ROWS = 16
CTX_MOD_ROW = BATCH

KV_SIDE = KV_RANK + ROPE + XBC + 2 * SSM_HEADS

C_KV = 0
C_KR = 128
C_KRR = 256
C_DT = 384
C_XBC = 512
C_Q = C_XBC + XBC
C_Z = C_Q + Q_RANK
C_UV = C_Z + SSM_INNER
C_GATE = C_UV + 2 * GM_WIDTH
C_END = C_GATE + N_BRANCH * D

TM = 256
KEY_BLOCK = 128
VMEM_LIMIT = 56 * 1024 * 1024


def _rms(x, g):
    y = x * lax.rsqrt(jnp.mean(x * x, axis=-1, keepdims=True) + EPS)
    return y * g


def _silu(x):
    return x * jax.nn.sigmoid(x)


def _bdot(a, b):
    return jnp.dot(a.astype(BF16), b.astype(BF16), preferred_element_type=F32)


def _split3(a):
    a1 = a.astype(BF16)
    r1 = a - a1.astype(F32)
    a2 = r1.astype(BF16)
    a3 = (r1 - a2.astype(F32)).astype(BF16)
    return a1, a2, a3


def _dot_right01(a, m01):
    return sum(jnp.dot(p, m01, preferred_element_type=F32) for p in _split3(a))


def _dot_left01(m01, a):
    return sum(jnp.dot(m01, p, preferred_element_type=F32) for p in _split3(a))


def _const_spec(shape, l=None):
    nd = len(shape)
    if l is None:
        return pl.BlockSpec(shape, lambda *_: (0,) * nd, pipeline_mode=pl.Buffered(1))
    return pl.BlockSpec((None,) + shape, lambda *_: (l,) + (0,) * nd, pipeline_mode=pl.Buffered(1))


def _mod_spec(l, t0):
    return pl.BlockSpec((None, 1, N_MOD, D), lambda b, t: (l, jnp.where(t + t0 == 0, CTX_MOD_ROW, b), 0, 0))


def _row_spec(width, t0):
    return pl.BlockSpec((1, TM, width), lambda b, t: (b, t + t0, 0))


def _params(sem):
    return pltpu.CompilerParams(dimension_semantics=sem, vmem_limit_bytes=VMEM_LIMIT)


def _ada_kernel(c_ref, w_ref, b_ref, o_ref):
    s = _silu(c_ref[...])
    o_ref[0] = _bdot(s, w_ref[0]) + b_ref[0]


def _ada(c_all, w_ada, b_ada):
    tn = 1152
    nn = N_MOD * D // tn
    return pl.pallas_call(
        _ada_kernel,
        grid=(DEPTH, nn),
        in_specs=[
            pl.BlockSpec((MOD_ROWS, D), lambda l, j: (0, 0)),
            pl.BlockSpec((1, D, tn), lambda l, j: (l, 0, j)),
            pl.BlockSpec((1, 1, tn), lambda l, j: (l, 0, j)),
        ],
        out_specs=pl.BlockSpec((1, MOD_ROWS, tn), lambda l, j: (l, 0, j)),
        out_shape=jax.ShapeDtypeStruct((DEPTH, MOD_ROWS, N_MOD * D), F32),
        compiler_params=_params(("arbitrary", "arbitrary")),
        name="ada_mod",
    )(c_all, w_ada, b_ada.reshape(DEPTH, 1, N_MOD * D))


def _ffn_kernel(*refs, mod0, final, split_input):
    if split_input:
        ctx_ref, x_ref, mod_ref, g_ref, wi_ref, wo_ref, fin_ref, o_ref = refs
        x = jnp.where(pl.program_id(1) == 0, ctx_ref[0], x_ref[0])
    else:
        x_ref, mod_ref, g_ref, wi_ref, wo_ref, fin_ref, o_ref = refs
        x = x_ref[0]
    m = mod_ref[0]
    shift, scale, gate = m[mod0:mod0 + 1], m[mod0 + 1:mod0 + 2], m[mod0 + 2:mod0 + 3]
    h = _rms(x, g_ref[...]) * (1.0 + scale) + shift
    gu = jnp.dot(h.astype(BF16), wi_ref[...], preferred_element_type=F32)
    a = _silu(gu[:, :D_FF]) * gu[:, D_FF:]
    y = jnp.dot(a.astype(BF16), wo_ref[...], preferred_element_type=F32)
    out = x + (0.5 * gate) * y
    if final:
        out = _rms(out, fin_ref[...])
    o_ref[0] = out


def _ffn(xs, w, l, which, *, latent_only=False, final=False):
    split_input = isinstance(xs, tuple)
    t0 = 1 if latent_only else 0
    nt = T // TM - t0
    if split_input:
        x_specs = [pl.BlockSpec((1, TM, D), lambda b, t: (b, 0, 0)),
                   pl.BlockSpec((1, TM, D), lambda b, t: (b, jnp.maximum(t - 1, 0), 0))]
        xs_args = list(xs)
    else:
        x_specs = [pl.BlockSpec((1, TM, D), lambda b, t: (b, t, 0))]
        xs_args = [xs]
    return pl.pallas_call(
        functools.partial(_ffn_kernel, mod0=6 * which, final=final, split_input=split_input),
        grid=(BATCH, nt),
        in_specs=x_specs + [
            _mod_spec(l, t0),
            _const_spec((1, D), 3 * l + 2 * which),
            _const_spec((D, 2 * D_FF), l),
            _const_spec((D_FF, D), l),
            _const_spec((1, D)),
        ],
        out_specs=pl.BlockSpec((1, TM, D), lambda b, t: (b, t, 0)),
        out_shape=jax.ShapeDtypeStruct((BATCH, nt * TM, D), F32),
        compiler_params=_params(("parallel", "parallel")),
        name="ffn",
    )(*xs_args, w["mods"], w["norm_g"], w["ffn_w_in"][which], w["ffn_w_out"][which], w["final_norm"])


def _proj_kernel(x_ref, mod_ref, g_ref, win_ref, wdt_ref, kvn_ref, wkn_ref, wvt_ref, one_ref, qn_ref, wq_ref, wqr_ref,
                 cos_ref, sin_ref,
                 q_out, k_out, vt_out, xbc_out, dt_out, dtt_out, z_out, uv_out, gate_out):
    x = x_ref[0]
    m = mod_ref[0]
    h = _rms(x, g_ref[...]) * (1.0 + m[4:5]) + m[3:4]
    hb = h.astype(BF16)
    cos = cos_ref[...]
    sin = sin_ref[...]

    def proj(c0, c1):
        return jnp.dot(hb, win_ref[:, c0:c1], preferred_element_type=F32)

    small = proj(C_KV, C_XBC)
    xbc_out[0] = proj(C_XBC, C_Q).astype(BF16)
    q_lat = proj(C_Q, C_Z)
    z_out[0] = proj(C_Z, C_UV).astype(BF16)
    uv_out[0] = proj(C_UV, C_GATE).astype(BF16)
    gate_out[0] = proj(C_GATE, C_END).astype(BF16)

    dt_out[0] = small[:, C_DT:C_DT + 128]
    dtt_out[0] = lax.dot_general(wdt_ref[...], hb, (((1,), (1,)), ((), ())), preferred_element_type=F32)

    kvn = _rms(small[:, C_KV:C_KV + 128], kvn_ref[...]).astype(BF16)
    kn = jnp.dot(kvn, wkn_ref[...], preferred_element_type=F32)
    kr = small[:, C_KR:C_KR + 128] * cos + small[:, C_KRR:C_KRR + 128] * sin
    for hd in range(HEADS):
        sl = slice(hd * HEAD_PAD, (hd + 1) * HEAD_PAD)
        k_out[0, :, sl] = (kn[:, sl] + kr).astype(BF16)
    vt = lax.dot_general(wvt_ref[...], kvn, (((1,), (1,)), ((), ())), preferred_element_type=F32)
    vt_out[0] = (vt + one_ref[...]).astype(BF16)

    qn = _rms(q_lat, qn_ref[...]).astype(BF16)
    qa = jnp.dot(qn, wq_ref[...], preferred_element_type=F32)
    qb = jnp.dot(qn, wqr_ref[...], preferred_element_type=F32)
    for hd in range(HEADS):
        sl = slice(hd * HEAD_PAD, (hd + 1) * HEAD_PAD)
        q_out[0, :, sl] = (qa[:, sl] * cos + qb[:, sl] * sin).astype(BF16)


def _proj(xs, w, l):
    nt = T // TM

    def rows(w, dt):
        return jax.ShapeDtypeStruct((BATCH, T, w), dt), _row_spec(w, 0)

    def cols(r, dt):
        return jax.ShapeDtypeStruct((BATCH, r, T), dt), pl.BlockSpec((1, r, TM), lambda b, t: (b, 0, t))

    outs = [rows(HEADS * HEAD_PAD, BF16), rows(HEADS * HEAD_PAD, BF16), cols(HEADS * V_ROWS, BF16),
            rows(XBC, BF16), rows(128, F32), cols(2 * SSM_HEADS, F32),
            rows(SSM_INNER, BF16), rows(2 * GM_WIDTH, BF16), rows(N_BRANCH * D, BF16)]
    out_shape = [o[0] for o in outs]
    out_specs = [o[1] for o in outs]
    return pl.pallas_call(
        _proj_kernel,
        grid=(BATCH, nt),
        in_specs=[
            _row_spec(D, 0),
            _mod_spec(l, 0),
            _const_spec((1, D), 3 * l + 1),
            _const_spec((D, C_END), l),
            _const_spec((2 * SSM_HEADS, D), l),
            _const_spec((1, KV_RANK), l),
            _const_spec((KV_RANK, HEADS * HEAD_PAD), l),
            _const_spec((HEADS * V_ROWS, KV_RANK), l),
            _const_spec((HEADS * V_ROWS, 1)),
            _const_spec((1, Q_RANK), l),
            _const_spec((Q_RANK, HEADS * HEAD_PAD), l),
            _const_spec((Q_RANK, HEADS * HEAD_PAD), l),
            pl.BlockSpec((TM, HEAD_PAD), lambda b, t: (t, 0)),
            pl.BlockSpec((TM, HEAD_PAD), lambda b, t: (t, 0)),
        ],
        out_specs=out_specs,
        out_shape=out_shape,
        compiler_params=_params(("parallel", "parallel")),
        name="mixer_proj",
    )(xs, w["mods"], w["norm_g"], w["w_in"], w["w_dt_t"], w["kv_norm"], w["w_kn"], w["w_v_t"], w["v_ones"],
      w["q_norm"], w["w_q"], w["w_q_rot"], w["cos"], w["sin"])


def _attn_kernel(q_ref, k_ref, vt_ref, o_ref, *, t0):
    def run(nk):
        nblk = nk // KEY_BLOCK

        def score_block(hd, j):
            q = q_ref[0, :, hd * HEAD_PAD:(hd + 1) * HEAD_PAD]
            k = k_ref[0, j * KEY_BLOCK:(j + 1) * KEY_BLOCK, hd * HEAD_PAD:(hd + 1) * HEAD_PAD]
            return lax.dot_general(k, q, (((1,), (1,)), ((), ())), preferred_element_type=F32)

        def col_max(blocks):
            mx = functools.reduce(jnp.maximum, blocks)
            mx = jnp.max(mx.reshape(KEY_BLOCK // 8, 8, TM), axis=0)
            return jnp.max(mx, axis=0, keepdims=True)

        outs = []
        cur = [score_block(0, j) for j in range(nblk)]
        for hd in range(HEADS):
            mx = col_max(cur)
            nxt, ps = [], []
            for j in range(nblk):
                if hd + 1 < HEADS:
                    nxt.append(score_block(hd + 1, j))
                ps.append(jnp.exp2((cur[j] - mx) * EXP2_SCALE).astype(BF16))
            p = jnp.concatenate(ps, axis=0)
            ot = jnp.dot(vt_ref[0, hd * V_ROWS:(hd + 1) * V_ROWS, :nk], p, preferred_element_type=F32)
            outs.append(ot[:VDIM] / ot[VDIM:VDIM + 1])
            cur = nxt
        o_ref[0] = jnp.concatenate(outs, axis=0).T.astype(BF16)

    if t0 == 0:
        t = pl.program_id(1)
        pl.when(t == 0)(lambda: run(CTX))
        pl.when(t != 0)(lambda: run(T))
    else:
        run(T)


def _attn(q, k, v, *, t0):
    nt = T // TM - t0
    return pl.pallas_call(
        functools.partial(_attn_kernel, t0=t0),
        grid=(BATCH, nt),
        in_specs=[
            _row_spec(HEADS * HEAD_PAD, t0),
            pl.BlockSpec((1, T, HEADS * HEAD_PAD), lambda b, t: (b, 0, 0)),
            pl.BlockSpec((1, HEADS * V_ROWS, T), lambda b, t: (b, 0, 0)),
        ],
        out_specs=_row_spec(HEADS * VDIM, t0),
        out_shape=jax.ShapeDtypeStruct((BATCH, T, HEADS * VDIM), BF16),
        compiler_params=_params(("parallel", "arbitrary")),
        name="mla_attn",
    )(q, k, v)


def _ssd_kernel(xbc_ref, dt_ref, dtt_ref, cw_ref, cb_ref, bias_row_ref, bias_col_ref, alog_row_ref, alog_col_ref,
                dfull_ref, y_ref, xp_ref, xc_ref, h_ref):
    def padded_row(c):
        return pl.multiple_of(c * CHUNK + jnp.where(c < CTX_CHUNKS, PAD_ROWS, 2 * PAD_ROWS), PAD_ROWS)

    zeros_pad = jnp.zeros((PAD_ROWS, XBC), F32)
    xp_ref[0:PAD_ROWS, :] = zeros_pad
    xp_ref[PAD_ROWS + CTX:2 * PAD_ROWS + CTX, :] = zeros_pad
    xp_ref[2 * PAD_ROWS + T:3 * PAD_ROWS + T, :] = zeros_pad

    def copy_body(c, carry):
        r0 = pl.multiple_of(c * CHUNK, CHUNK)
        xp_ref[pl.ds(padded_row(c), CHUNK), :] = xbc_ref[0, pl.ds(r0, CHUNK), :].astype(F32)
        return carry

    lax.fori_loop(0, N_CHUNKS, copy_body, 0)

    def conv_body(c, carry):
        r0 = pl.multiple_of(c * CHUNK, CHUNK)
        w0 = pl.multiple_of(padded_row(c) - PAD_ROWS, PAD_ROWS)
        win = xp_ref[pl.ds(w0, CHUNK + 2 * PAD_ROWS), :]
        acc = cb_ref[...]
        for k in range(SSM_CONV):
            off = PAD_ROWS + k - SSM_CONV // 2
            acc = acc + cw_ref[k:k + 1, :] * win[off:off + CHUNK]
        xc_ref[pl.ds(r0, CHUNK), :] = _silu(acc)
        return carry

    lax.fori_loop(0, N_CHUNKS, conv_body, 0)

    row = lax.broadcasted_iota(jnp.int32, (CHUNK, CHUNK), 0)
    col = lax.broadcasted_iota(jnp.int32, (CHUNK, CHUNK), 1)
    lower = (col <= row)
    upper = (col >= row)
    lower01 = jnp.where(lower, 1.0, 0.0).astype(BF16)
    upper01 = jnp.where(upper, 1.0, 0.0).astype(BF16)
    lo_half = lax.broadcasted_iota(jnp.int32, (CHUNK, 128), 1) < SSM_P
    a_row = -jnp.exp(alog_row_ref[...])
    a_col = -jnp.exp(alog_col_ref[...])

    def chunk_step(c, d):
        r0 = pl.multiple_of(c * CHUNK, CHUNK)
        xc = xc_ref[pl.ds(r0, CHUNK), :]
        dcol = jax.nn.softplus(dt_ref[0, pl.ds(r0, CHUNK), :] + bias_row_ref[...])
        drow = jax.nn.softplus(dtt_ref[0, :, pl.ds(r0, CHUNK)] + bias_col_ref[...])
        tri_l, tri_r, mask = (lower01, upper01, lower) if d == 0 else (upper01, lower01, upper)
        acum_c = _dot_left01(tri_l, dcol * a_row)
        acum_r = _dot_right01(drow * a_col, tri_r)
        atot_c = acum_c[CHUNK - 1:CHUNK] if d == 0 else acum_c[0:1]
        w_c = jnp.exp(atot_c - acum_c) * dcol
        etot_c = jnp.exp(atot_c)

        def head_col(arr, hd):
            lane = d * SSM_HEADS + hd
            return arr[:, lane:lane + 1]

        def pair_lanes(arr, p):
            return jnp.where(lo_half[:arr.shape[0]], head_col(arr, 2 * p), head_col(arr, 2 * p + 1))

        ys = []
        for g in range(SSM_GROUPS):
            bm = xc[:, SSM_INNER + g * SSM_N:SSM_INNER + (g + 1) * SSM_N]
            cm = xc[:, SSM_INNER + (SSM_GROUPS + g) * SSM_N:SSM_INNER + (SSM_GROUPS + g + 1) * SSM_N]
            bmb = bm.astype(BF16)
            cmb = cm.astype(BF16)
            cb = lax.dot_general(cmb, bmb, (((1,), (1,)), ((), ())), preferred_element_type=F32)
            bt = bm.T.astype(BF16)
            for pp in range(2):
                p = g * 2 + pp
                sc, a_bc = [], []
                for hh in range(2):
                    hd = 2 * p + hh
                    a_bc.append(jnp.broadcast_to(head_col(acum_c, hd), (CHUNK, CHUNK)))
                    diff = a_bc[hh] - acum_r[d * SSM_HEADS + hd:d * SSM_HEADS + hd + 1]
                    dec = jnp.exp(jnp.where(mask, diff, -jnp.inf))
                    sc.append(cb * dec * drow[d * SSM_HEADS + hd:d * SSM_HEADS + hd + 1])
                x_p = xc[:, p * 128:(p + 1) * 128]
                xw = (x_p * pair_lanes(w_c, p)).astype(BF16)
                s_t = jnp.dot(bt, xw, preferred_element_type=F32)
                h_prev = h_ref[d, p]
                e_p = jnp.exp(jnp.where(lo_half, a_bc[0], a_bc[1]))
                y_inter = jnp.dot(cmb, h_prev.astype(BF16), preferred_element_type=F32) * e_p
                pmat = jnp.concatenate(sc, axis=1).astype(BF16)
                xbd = jnp.concatenate([jnp.where(lo_half, x_p, 0.0), jnp.where(lo_half, 0.0, x_p)],
                                      axis=0).astype(BF16)
                ys.append(jnp.dot(pmat, xbd, preferred_element_type=F32) + y_inter)
                h_ref[d, p] = h_prev * pair_lanes(etot_c, p) + s_t
        y = jnp.concatenate(ys, axis=1)
        if d == 0:
            y = y + dfull_ref[...] * xc[:, :SSM_INNER]
        y_ref[0, pl.ds(r0, CHUNK), :] += y

    h_ref[...] = jnp.zeros_like(h_ref)
    y_ref[...] = jnp.zeros_like(y_ref)

    def scan_body(i, carry):
        chunk_step(i, 0)
        chunk_step(jnp.where(i < CTX_CHUNKS, CTX_CHUNKS - 1 - i, N_CHUNKS + CTX_CHUNKS - 1 - i), 1)
        return carry

    lax.fori_loop(0, N_CHUNKS, scan_body, 0, unroll=2)


def _ssd(xbc, dt, dtt, w, l):
    return pl.pallas_call(
        _ssd_kernel,
        grid=(BATCH,),
        in_specs=[
            pl.BlockSpec((1, T, XBC), lambda b: (b, 0, 0)),
            pl.BlockSpec((1, T, 128), lambda b: (b, 0, 0)),
            pl.BlockSpec((1, 2 * SSM_HEADS, T), lambda b: (b, 0, 0)),
            _const_spec((8, XBC), l),
            _const_spec((1, XBC), l),
            _const_spec((1, 128), l),
            _const_spec((2 * SSM_HEADS, 1), l),
            _const_spec((1, 128), l),
            _const_spec((2 * SSM_HEADS, 1), l),
            _const_spec((1, SSM_INNER), l),
        ],
        out_specs=pl.BlockSpec((1, T, SSM_INNER), lambda b: (b, 0, 0)),
        out_shape=jax.ShapeDtypeStruct((BATCH, T, SSM_INNER), F32),
        scratch_shapes=[
            pltpu.VMEM((T + 3 * PAD_ROWS, XBC), F32),
            pltpu.VMEM((T, XBC), F32),
            pltpu.VMEM((2, SSM_HEADS // 2, SSM_N, 2 * SSM_P), F32),
        ],
        compiler_params=_params(("arbitrary",)),
        name="ssd_bidir",
    )(xbc, dt, dtt, w["conv_w"], w["conv_b"], w["dt_bias_row"], w["dt_bias_col"], w["a_log_row"],
      w["a_log_col"], w["d_full"])


def _mixout_kernel(x_ref, mod_ref, a_ref, y_ref, z_ref, uv_ref, gate_ref,
                   wmo_ref, sn_ref, wso_ref, gn_ref, ws_ref, bs_ref, wgo_ref, bg_ref, wout_ref, o_ref):
    x = x_ref[0]
    m = mod_ref[0]
    o_mla = jnp.dot(a_ref[0], wmo_ref[...], preferred_element_type=F32)

    ys = y_ref[0] * _silu(z_ref[0].astype(F32))
    o_ssm = _bdot(_rms(ys, sn_ref[...]), wso_ref[...])

    uv = jax.nn.gelu(uv_ref[0].astype(F32))
    u = uv[:, :GM_WIDTH]
    v = uv[:, GM_WIDTH:]
    v = v - jnp.mean(v, axis=-1, keepdims=True)
    v = v * lax.rsqrt(jnp.mean(v * v, axis=-1, keepdims=True) + EPS) * gn_ref[...]
    lo_half = lax.broadcasted_iota(jnp.int32, (CHUNK, 128), 1) < GM_GDIM
    mixed = []
    for ch in range(TM // CHUNK):
        parts = []
        for p in range(GM_GROUPS // 2):
            vp = v[ch * CHUNK:(ch + 1) * CHUNK, p * 128:(p + 1) * 128]
            vbd = jnp.concatenate([jnp.where(lo_half, vp, 0.0), jnp.where(lo_half, 0.0, vp)], axis=0)
            parts.append(jnp.dot(ws_ref[p], vbd.astype(BF16), preferred_element_type=F32))
        mixed.append(jnp.concatenate(parts, axis=1) + bs_ref[...])
    mixed = jnp.concatenate(mixed, axis=0)
    o_gm = _bdot(u * mixed, wgo_ref[...])

    gts = jax.nn.sigmoid(gate_ref[0].astype(F32) + bg_ref[...])
    merged = gts[:, :D] * o_mla + gts[:, D:2 * D] * o_ssm + gts[:, 2 * D:] * o_gm
    y = _bdot(merged, wout_ref[...])
    o_ref[0] = x + m[5:6] * y


def _mixout(xs, a, y, z, uv, gate, w, l, *, t0):
    nt = T // TM - t0
    return pl.pallas_call(
        _mixout_kernel,
        grid=(BATCH, nt),
        in_specs=[
            _row_spec(D, t0),
            _mod_spec(l, t0),
            _row_spec(HEADS * VDIM, t0),
            _row_spec(SSM_INNER, t0),
            _row_spec(SSM_INNER, t0),
            _row_spec(2 * GM_WIDTH, t0),
            _row_spec(N_BRANCH * D, t0),
            _const_spec((HEADS * VDIM, D), l),
            _const_spec((1, SSM_INNER), l),
            _const_spec((SSM_INNER, D), l),
            _const_spec((1, GM_WIDTH), l),
            _const_spec((GM_GROUPS // 2, CHUNK, 2 * CHUNK), l),
            _const_spec((CHUNK, GM_WIDTH), l),
            _const_spec((GM_WIDTH, D), l),
            _const_spec((1, N_BRANCH * D), l),
            _const_spec((D, D), l),
        ],
        out_specs=pl.BlockSpec((1, TM, D), lambda b, t: (b, t, 0)),
        out_shape=jax.ShapeDtypeStruct((BATCH, nt * TM, D), F32),
        compiler_params=_params(("parallel", "parallel")),
        name="mixer_out",
    )(xs, w["mods"], a, y, z, uv, gate, w["w_mla_o"], w["ssm_norm"], w["w_ssm_o"], w["gm_norm"], w["w_s"],
      w["b_s"], w["w_gm_o"], w["b_gate"], w["w_out"])


def _rot_half(w):
    half = ROPE // 2
    return jnp.concatenate([-w[..., half:], w[..., :half]], axis=-1)


def _head_pad(nope, rope):
    pad = jnp.zeros(nope.shape[:-1] + (HEAD_PAD - NOPE - ROPE,), nope.dtype)
    out = jnp.concatenate([nope, rope, pad], axis=-1)
    return out.reshape(out.shape[:-2] + (HEADS * HEAD_PAD,))


def _prep_weights(w_in, mla_q_norm, mla_w_uq, mla_kv_norm, mla_w_ukv, mla_w_o, ssm_conv_w, ssm_conv_b,
                  ssm_a_log, ssm_dt_bias, ssm_d, ssm_norm, ssm_w_o, gm_norm, gm_w_s, gm_b_s, gm_w_o, b_gate, w_out):
    w = w_in.astype(BF16)
    o = 0
    kv = w[..., o:o + KV_RANK]; o += KV_RANK
    kr = w[..., o:o + ROPE]; o += ROPE
    xbc = w[..., o:o + XBC]; o += XBC
    dtw = w[..., o:o + 2 * SSM_HEADS]; o += 2 * SSM_HEADS
    rest = w[..., o:]

    def lanes(cols, start):
        return jnp.pad(cols, ((0, 0), (0, 0), (start, 128 - start - cols.shape[-1])))

    w_new = jnp.concatenate([kv, lanes(kr, NOPE), lanes(_rot_half(kr), NOPE), lanes(dtw, 0), xbc, rest], axis=-1)

    uq = mla_w_uq.astype(BF16).reshape(DEPTH, Q_RANK, HEADS, NOPE + ROPE)
    uq_n, uq_r = uq[..., :NOPE], uq[..., NOPE:]
    ukv = mla_w_ukv.astype(BF16).reshape(DEPTH, KV_RANK, HEADS, NOPE + VDIM)
    kn = _head_pad(ukv[..., :NOPE], jnp.zeros((DEPTH, KV_RANK, HEADS, ROPE), BF16))
    v_t = jnp.pad(jnp.transpose(ukv[..., NOPE:], (0, 2, 3, 1)), ((0, 0), (0, 0), (0, V_ROWS - VDIM), (0, 0)))
    v_ones = jnp.tile((jnp.arange(V_ROWS) == VDIM).astype(F32), HEADS).reshape(HEADS * V_ROWS, 1)

    ws = gm_w_s.astype(BF16)
    ws_pair = jnp.concatenate([ws[:, 0::2], ws[:, 1::2]], axis=3)

    def row128(vals, fill=0.0):
        vals = vals.reshape(DEPTH, 1, -1)
        return jnp.pad(vals, ((0, 0), (0, 0), (0, 128 - vals.shape[-1])), constant_values=fill)

    return {
        "w_in": w_new,
        "w_dt_t": jnp.swapaxes(dtw, 1, 2),
        "kv_norm": mla_kv_norm.reshape(DEPTH, 1, KV_RANK),
        "w_kn": kn,
        "w_v_t": v_t.reshape(DEPTH, HEADS * V_ROWS, KV_RANK),
        "v_ones": v_ones,
        "q_norm": mla_q_norm.reshape(DEPTH, 1, Q_RANK),
        "w_q": _head_pad(uq_n, uq_r),
        "w_q_rot": _head_pad(jnp.zeros_like(uq_n), _rot_half(uq_r)),
        "conv_w": jnp.pad(jnp.swapaxes(ssm_conv_w, 1, 2), ((0, 0), (0, 8 - SSM_CONV), (0, 0))),
        "conv_b": ssm_conv_b.reshape(DEPTH, 1, XBC),
        "dt_bias_row": row128(ssm_dt_bias),
        "dt_bias_col": ssm_dt_bias.reshape(DEPTH, 2 * SSM_HEADS, 1),
        "a_log_row": row128(ssm_a_log, fill=-80.0),
        "a_log_col": ssm_a_log.reshape(DEPTH, 2 * SSM_HEADS, 1),
        "d_full": jnp.repeat(ssm_d, SSM_P, axis=1).reshape(DEPTH, 1, SSM_INNER),
        "w_mla_o": mla_w_o.astype(BF16),
        "ssm_norm": ssm_norm.reshape(DEPTH, 1, SSM_INNER),
        "w_ssm_o": ssm_w_o.astype(BF16),
        "gm_norm": gm_norm.reshape(DEPTH, 1, GM_WIDTH),
        "w_s": ws_pair,
        "b_s": jnp.repeat(jnp.swapaxes(gm_b_s, 1, 2), GM_GDIM, axis=2),
        "w_gm_o": gm_w_o.astype(BF16),
        "b_gate": b_gate.reshape(DEPTH, 1, N_BRANCH * D),
        "w_out": w_out.astype(BF16),
    }


def _rope_tables():
    rows = SEQ // GRID_W
    r = jnp.repeat(jnp.arange(rows, dtype=F32), GRID_W)
    c = jnp.tile(jnp.arange(GRID_W, dtype=F32), rows)
    n_freq = ROPE // 4
    inv = jnp.power(ROPE_BASE, -jnp.arange(n_freq, dtype=F32) / n_freq)
    ang = jnp.concatenate([r[:, None] * inv, c[:, None] * inv], axis=-1)
    cos, sin = jnp.cos(ang), jnp.sin(ang)
    cos = jnp.concatenate([jnp.ones((CTX, ROPE // 2), F32), cos], axis=0)
    sin = jnp.concatenate([jnp.zeros((CTX, ROPE // 2), F32), sin], axis=0)
    ones = jnp.ones((T, NOPE), F32)
    zeros_n = jnp.zeros((T, NOPE), F32)
    zeros_p = jnp.zeros((T, HEAD_PAD - NOPE - ROPE), F32)
    cos_t = jnp.concatenate([ones, cos, cos, zeros_p], axis=1)
    sin_t = jnp.concatenate([zeros_n, sin, sin, zeros_p], axis=1)
    return cos_t, sin_t


def kernel(x, c, ctx, c_ctx, w_ada, b_ada, norm_g, ffn1_w_in, ffn1_w_out, ffn2_w_in, ffn2_w_out, w_in, mla_q_norm,
           mla_w_uq, mla_kv_norm, mla_w_ukv, mla_w_o, ssm_conv_w, ssm_conv_b, ssm_a_log, ssm_dt_bias, ssm_d,
           ssm_norm, ssm_w_o, gm_norm, gm_w_s, gm_b_s, gm_w_o, b_gate, w_out, final_norm):
    c_all = jnp.concatenate([c, c_ctx[None, :], jnp.zeros((MOD_ROWS - BATCH - 1, D), F32)], axis=0)
    w = _prep_weights(w_in, mla_q_norm, mla_w_uq, mla_kv_norm, mla_w_ukv, mla_w_o, ssm_conv_w, ssm_conv_b, ssm_a_log,
                      ssm_dt_bias, ssm_d, ssm_norm, ssm_w_o, gm_norm, gm_w_s, gm_b_s, gm_w_o, b_gate, w_out)
    w["mods"] = _ada(c_all, w_ada, b_ada).reshape(DEPTH, MOD_ROWS, N_MOD, D)
    w["norm_g"] = norm_g.reshape(DEPTH * 3, 1, D)
    w["ffn_w_in"] = (ffn1_w_in.astype(BF16), ffn2_w_in.astype(BF16))
    w["ffn_w_out"] = (ffn1_w_out.astype(BF16), ffn2_w_out.astype(BF16))
    w["final_norm"] = final_norm.reshape(1, D)
    w["cos"], w["sin"] = _rope_tables()

    xs = (ctx, x)
    for l in range(DEPTH):
        last = l == DEPTH - 1
        t0 = 1 if last else 0
        xs = _ffn(xs, w, l, 0)
        q, k, vt, xbc, dt, dtt, z, uv, gate = _proj(xs, w, l)
        a = _attn(q, k, vt, t0=t0)
        y = _ssd(xbc, dt, dtt, w, l)
        xs = _mixout(xs, a, y, z, uv, gate, w, l, t0=t0)
        xs = _ffn(xs, w, l, 1, latent_only=last, final=last)
    return xs
```

```python
import functools
import math

import jax
import jax.numpy as jnp
from jax import lax
from jax.experimental import pallas as pl
from jax.experimental.pallas import tpu as pltpu

F32 = jnp.float32
BF16 = jnp.bfloat16

D = 1024
BATCH = 8
SEQ = 2048
DEPTH = 2
CTX = 256
T = CTX + SEQ
GRID_W = 64
EPS = 1e-6

HEADS = 8
NOPE = 64
ROPE = 32
VDIM = 64
Q_RANK = 256
KV_RANK = 128
ROPE_BASE = 10000.0
ATTN_SCALE = (NOPE + ROPE) ** -0.5
EXP2_SCALE = ATTN_SCALE * math.log2(math.e)
HEAD_PAD = 128
V_ROWS = 80

SSM_HEADS = 8
SSM_P = 64
SSM_INNER = SSM_HEADS * SSM_P
SSM_GROUPS = 2
SSM_N = 128
SSM_CONV = 5
CHUNK = 128
XBC = SSM_INNER + 2 * SSM_GROUPS * SSM_N
N_CHUNKS = T // CHUNK
CTX_CHUNKS = CTX // CHUNK
HALO = 8

GM_GROUPS = 8
GM_WIDTH = 512
GM_GDIM = GM_WIDTH // GM_GROUPS

D_FF = 2816
N_BRANCH = 3
N_MOD = 9
MOD_ROWS = 16
CTX_MOD_ROW = BATCH

KV_SIDE = KV_RANK + ROPE + XBC + 2 * SSM_HEADS

C_KV = 0
C_KR = 128
C_KRR = 256
C_DT = 384
C_XBC = 512
C_Q = C_XBC + XBC
C_Z = C_Q + Q_RANK
C_UV = C_Z + SSM_INNER
C_GATE = C_UV + 2 * GM_WIDTH
C_END = C_GATE + N_BRANCH * D

TM = 256
KEY_BLOCK = 128
PROJ_PIECE = 512
VMEM_LIMIT = 56 * 1024 * 1024


def _rms(x, g):
    y = x * lax.rsqrt(jnp.mean(x * x, axis=-1, keepdims=True) + EPS)
    return y * g


def _silu(x):
    return x * jax.nn.sigmoid(x)


def _bdot(a, b):
    return jnp.dot(a.astype(BF16), b.astype(BF16), preferred_element_type=F32)


def _split3(a):
    a1 = a.astype(BF16)
    r1 = a - a1.astype(F32)
    a2 = r1.astype(BF16)
    a3 = (r1 - a2.astype(F32)).astype(BF16)
    return a1, a2, a3


def _dot_right01(a, m01):
    return sum(jnp.dot(p, m01, preferred_element_type=F32) for p in _split3(a))


def _dot_left01(m01, a):
    return sum(jnp.dot(m01, p, preferred_element_type=F32) for p in _split3(a))


def _const_spec(shape, l=None):
    nd = len(shape)
    if l is None:
        return pl.BlockSpec(shape, lambda *_: (0,) * nd, pipeline_mode=pl.Buffered(1))
    return pl.BlockSpec((None,) + shape, lambda *_: (l,) + (0,) * nd, pipeline_mode=pl.Buffered(1))


def _mod_spec(l, t0):
    return pl.BlockSpec((None, 1, N_MOD, D), lambda b, t: (l, jnp.where(t + t0 == 0, CTX_MOD_ROW, b), 0, 0))


def _row_spec(width, t0):
    return pl.BlockSpec((1, TM, width), lambda b, t: (b, t + t0, 0))


def _params(sem):
    return pltpu.CompilerParams(dimension_semantics=sem, vmem_limit_bytes=VMEM_LIMIT)


def _ada_kernel(c_ref, w_ref, b_ref, o_ref):
    s = _silu(c_ref[...])
    o_ref[0] = _bdot(s, w_ref[0]) + b_ref[0]


def _ada(c_all, w_ada, b_ada):
    tn = 1152
    nn = N_MOD * D // tn
    return pl.pallas_call(
        _ada_kernel,
        grid=(DEPTH, nn),
        in_specs=[
            pl.BlockSpec((MOD_ROWS, D), lambda l, j: (0, 0)),
            pl.BlockSpec((1, D, tn), lambda l, j: (l, 0, j)),
            pl.BlockSpec((1, 1, tn), lambda l, j: (l, 0, j)),
        ],
        out_specs=pl.BlockSpec((1, MOD_ROWS, tn), lambda l, j: (l, 0, j)),
        out_shape=jax.ShapeDtypeStruct((DEPTH, MOD_ROWS, N_MOD * D), F32),
        compiler_params=_params(("arbitrary", "arbitrary")),
        name="ada_mod",
    )(c_all, w_ada, b_ada.reshape(DEPTH, 1, N_MOD * D))


def _ffn_kernel(*refs, mod0, final, split_input):
    if split_input:
        ctx_ref, x_ref, mod_ref, g_ref, wi_ref, wo_ref, fin_ref, o_ref = refs
        x = jnp.where(pl.program_id(1) == 0, ctx_ref[0], x_ref[0])
    else:
        x_ref, mod_ref, g_ref, wi_ref, wo_ref, fin_ref, o_ref = refs
        x = x_ref[0]
    m = mod_ref[0]
    shift, scale, gate = m[mod0:mod0 + 1], m[mod0 + 1:mod0 + 2], m[mod0 + 2:mod0 + 3]
    h = _rms(x, g_ref[...]) * (1.0 + scale) + shift
    gu = jnp.dot(h.astype(BF16), wi_ref[...], preferred_element_type=F32)
    a = _silu(gu[:, :D_FF]) * gu[:, D_FF:]
    y = jnp.dot(a.astype(BF16), wo_ref[...], preferred_element_type=F32)
    out = x + (0.5 * gate) * y
    if final:
        out = _rms(out, fin_ref[...])
    o_ref[0] = out


def _ffn(xs, w, l, which, *, latent_only=False, final=False):
    split_input = isinstance(xs, tuple)
    t0 = 1 if latent_only else 0
    nt = T // TM - t0
    if split_input:
        x_specs = [pl.BlockSpec((1, TM, D), lambda b, t: (b, 0, 0)),
                   pl.BlockSpec((1, TM, D), lambda b, t: (b, jnp.maximum(t - 1, 0), 0))]
        xs_args = list(xs)
    else:
        x_specs = [pl.BlockSpec((1, TM, D), lambda b, t: (b, t, 0))]
        xs_args = [xs]
    return pl.pallas_call(
        functools.partial(_ffn_kernel, mod0=6 * which, final=final, split_input=split_input),
        grid=(BATCH, nt),
        in_specs=x_specs + [
            _mod_spec(l, t0),
            _const_spec((1, D), 3 * l + 2 * which),
            _const_spec((D, 2 * D_FF), l),
            _const_spec((D_FF, D), l),
            _const_spec((1, D)),
        ],
        out_specs=pl.BlockSpec((1, TM, D), lambda b, t: (b, t, 0)),
        out_shape=jax.ShapeDtypeStruct((BATCH, nt * TM, D), F32),
        compiler_params=_params(("parallel", "parallel")),
        name="ffn",
    )(*xs_args, w["mods"], w["norm_g"], w["ffn_w_in"][which], w["ffn_w_out"][which], w["final_norm"])


def _proj_kernel(x_ref, xprev_ref, xnext_ref, mod_ref, g_ref, win_ref, cw_ref, cb_ref, kvn_ref, wkn_ref, wvt_ref,
                 one_ref, qn_ref, wq_ref, wqr_ref, cos_ref, sin_ref,
                 q_out, k_out, vt_out, xc_out, dt_out, dtt_out, z_out, uv_out, gate_out):
    t = pl.program_id(1)
    m = mod_ref[0]
    xe = jnp.concatenate([xprev_ref[0], x_ref[0], xnext_ref[0]], axis=0)
    he = _rms(xe, g_ref[...]) * (1.0 + m[4:5]) + m[3:4]
    hb = he[HALO:HALO + TM].astype(BF16)
    he = he.astype(BF16)
    cos = cos_ref[...]
    sin = sin_ref[...]

    def proj(c0, c1):
        return jnp.dot(hb, win_ref[:, c0:c1], preferred_element_type=F32)

    small = proj(C_KV, C_XBC)

    xbc = jnp.dot(he, win_ref[:, C_XBC:C_Q], preferred_element_type=F32)
    row = lax.broadcasted_iota(jnp.int32, (TM + 2 * HALO, 1), 0)
    has_prev = t >= 2
    has_next = (t >= 1) & (t < T // TM - 1)
    keep = ((row >= HALO) | has_prev) & ((row < HALO + TM) | has_next)
    def conv_lane_tile(c0):
        col = jnp.where(keep, xbc[:, c0:c0 + 128], 0.0)
        acc = cb_ref[:, c0:c0 + 128]
        for k in range(SSM_CONV):
            shift = (SSM_CONV // 2 - k) % (TM + 2 * HALO)
            tap = col if shift == 0 else pltpu.roll(col, shift, axis=0)
            acc = acc + cw_ref[k:k + 1, c0:c0 + 128] * tap[HALO:HALO + TM]
        xc_out[0, :, c0:c0 + 128] = _silu(acc)

    q_lat = proj(C_Q, C_Z)
    out_pieces = [(z_out, C_Z, SSM_INNER), (uv_out, C_UV, 2 * GM_WIDTH), (gate_out, C_GATE, N_BRANCH * D)]
    conv_tiles = list(range(0, XBC, 128))
    for out, base, width in out_pieces:
        for c0 in range(0, width, PROJ_PIECE):
            out[0, :, c0:c0 + PROJ_PIECE] = proj(base + c0, base + c0 + PROJ_PIECE).astype(BF16)
            if conv_tiles:
                conv_lane_tile(conv_tiles.pop(0))
    assert not conv_tiles

    dt = small[:, C_DT:C_DT + 128]
    dt_out[0] = dt
    dtt_out[0] = dt.T[:2 * SSM_HEADS]

    kvn = _rms(small[:, C_KV:C_KV + 128], kvn_ref[...]).astype(BF16)
    kn = jnp.dot(kvn, wkn_ref[...], preferred_element_type=F32)
    kr = small[:, C_KR:C_KR + 128] * cos + small[:, C_KRR:C_KRR + 128] * sin
    for hd in range(HEADS):
        sl = slice(hd * HEAD_PAD, (hd + 1) * HEAD_PAD)
        k_out[0, :, sl] = (kn[:, sl] + kr).astype(BF16)
    vt = lax.dot_general(wvt_ref[...], kvn, (((1,), (1,)), ((), ())), preferred_element_type=F32)
    vt_out[0] = (vt + one_ref[...]).astype(BF16)

    qn = _rms(q_lat, qn_ref[...]).astype(BF16)
    qa = jnp.dot(qn, wq_ref[...], preferred_element_type=F32)
    qb = jnp.dot(qn, wqr_ref[...], preferred_element_type=F32)
    for hd in range(HEADS):
        sl = slice(hd * HEAD_PAD, (hd + 1) * HEAD_PAD)
        q_out[0, :, sl] = (qa[:, sl] * cos + qb[:, sl] * sin).astype(BF16)


def _proj(xs, w, l):
    nt = T // TM

    def rows(w, dt):
        return jax.ShapeDtypeStruct((BATCH, T, w), dt), _row_spec(w, 0)

    def cols(r, dt):
        return jax.ShapeDtypeStruct((BATCH, r, T), dt), pl.BlockSpec((1, r, TM), lambda b, t: (b, 0, t))

    outs = [rows(HEADS * HEAD_PAD, BF16), rows(HEADS * HEAD_PAD, BF16), cols(HEADS * V_ROWS, BF16),
            rows(XBC, F32), rows(128, F32), cols(2 * SSM_HEADS, F32),
            rows(SSM_INNER, BF16), rows(2 * GM_WIDTH, BF16), rows(N_BRANCH * D, BF16)]
    out_shape = [o[0] for o in outs]
    out_specs = [o[1] for o in outs]
    return pl.pallas_call(
        _proj_kernel,
        grid=(BATCH, nt),
        in_specs=[
            _row_spec(D, 0),
            pl.BlockSpec((1, HALO, D), lambda b, t: (b, jnp.maximum(t * (TM // HALO) - 1, 0), 0)),
            pl.BlockSpec((1, HALO, D), lambda b, t: (b, jnp.minimum((t + 1) * (TM // HALO), T // HALO - 1), 0)),
            _mod_spec(l, 0),
            _const_spec((1, D), 3 * l + 1),
            _const_spec((D, C_END), l),
            _const_spec((8, XBC), l),
            _const_spec((1, XBC), l),
            _const_spec((1, KV_RANK), l),
            _const_spec((KV_RANK, HEADS * HEAD_PAD), l),
            _const_spec((HEADS * V_ROWS, KV_RANK), l),
            _const_spec((HEADS * V_ROWS, 1)),
            _const_spec((1, Q_RANK), l),
            _const_spec((Q_RANK, HEADS * HEAD_PAD), l),
            _const_spec((Q_RANK, HEADS * HEAD_PAD), l),
            pl.BlockSpec((TM, HEAD_PAD), lambda b, t: (t, 0)),
            pl.BlockSpec((TM, HEAD_PAD), lambda b, t: (t, 0)),
        ],
        out_specs=out_specs,
        out_shape=out_shape,
        compiler_params=_params(("parallel", "parallel")),
        name="mixer_proj",
    )(xs, xs, xs, w["mods"], w["norm_g"], w["w_in"], w["conv_w"], w["conv_b"], w["kv_norm"], w["w_kn"], w["w_v_t"],
      w["v_ones"],
      w["q_norm"], w["w_q"], w["w_q_rot"], w["cos"], w["sin"])


def _attn_kernel(q_ref, k_ref, vt_ref, o_ref, *, t0):
    def run(nk):
        nblk = nk // KEY_BLOCK

        def score_block(hd, j):
            q = q_ref[0, :, hd * HEAD_PAD:(hd + 1) * HEAD_PAD]
            k = k_ref[0, j * KEY_BLOCK:(j + 1) * KEY_BLOCK, hd * HEAD_PAD:(hd + 1) * HEAD_PAD]
            return lax.dot_general(k, q, (((1,), (1,)), ((), ())), preferred_element_type=F32)

        def col_max(blocks):
            mx = functools.reduce(jnp.maximum, blocks)
            mx = jnp.max(mx.reshape(KEY_BLOCK // 8, 8, TM), axis=0)
            return jnp.max(mx, axis=0, keepdims=True)

        outs = []
        cur = [score_block(0, j) for j in range(nblk)]
        for hd in range(HEADS):
            mx = col_max(cur)
            nxt, ps = [], []
            for j in range(nblk):
                if hd + 1 < HEADS:
                    nxt.append(score_block(hd + 1, j))
                ps.append(jnp.exp2((cur[j] - mx) * EXP2_SCALE).astype(BF16))
            p = jnp.concatenate(ps, axis=0)
            ot = jnp.dot(vt_ref[0, hd * V_ROWS:(hd + 1) * V_ROWS, :nk], p, preferred_element_type=F32)
            outs.append(ot[:VDIM] / ot[VDIM:VDIM + 1])
            cur = nxt
        o_ref[0] = jnp.concatenate(outs, axis=0).T.astype(BF16)

    if t0 == 0:
        t = pl.program_id(1)
        pl.when(t == 0)(lambda: run(CTX))
        pl.when(t != 0)(lambda: run(T))
    else:
        run(T)


def _attn(q, k, v, *, t0):
    nt = T // TM - t0
    return pl.pallas_call(
        functools.partial(_attn_kernel, t0=t0),
        grid=(BATCH, nt),
        in_specs=[
            _row_spec(HEADS * HEAD_PAD, t0),
            pl.BlockSpec((1, T, HEADS * HEAD_PAD), lambda b, t: (b, 0, 0)),
            pl.BlockSpec((1, HEADS * V_ROWS, T), lambda b, t: (b, 0, 0)),
        ],
        out_specs=pl.BlockSpec((1, TM, HEADS * VDIM), lambda b, t: (b, t, 0)),
        out_shape=jax.ShapeDtypeStruct((BATCH, nt * TM, HEADS * VDIM), BF16),
        compiler_params=_params(("parallel", "arbitrary")),
        name="mla_attn",
    )(q, k, v)


def _ssd_kernel(xc_ref, dt_ref, dtt_ref, bias_row_ref, bias_col_ref, alog_row_ref, alog_col_ref, dfull_ref,
                y_ref, h_ref):
    row = lax.broadcasted_iota(jnp.int32, (CHUNK, CHUNK), 0)
    col = lax.broadcasted_iota(jnp.int32, (CHUNK, CHUNK), 1)
    lower = (col <= row)
    upper = (col >= row)
    lower01 = jnp.where(lower, 1.0, 0.0).astype(BF16)
    upper01 = jnp.where(upper, 1.0, 0.0).astype(BF16)
    lo_half = lax.broadcasted_iota(jnp.int32, (CHUNK, 128), 1) < SSM_P
    a_row = -jnp.exp(alog_row_ref[...])
    a_col = -jnp.exp(alog_col_ref[...])

    def chunk_step(c, d):
        r0 = pl.multiple_of(c * CHUNK, CHUNK)
        xc = xc_ref[0, pl.ds(r0, CHUNK), :]
        dcol = jax.nn.softplus(dt_ref[0, pl.ds(r0, CHUNK), :] + bias_row_ref[...])
        drow = jax.nn.softplus(dtt_ref[0, :, pl.ds(r0, CHUNK)] + bias_col_ref[...])
        tri_l, tri_r, mask = (lower01, upper01, lower) if d == 0 else (upper01, lower01, upper)
        acum_c = _dot_left01(tri_l, dcol * a_row)
        acum_r = _dot_right01(drow * a_col, tri_r)
        atot_c = acum_c[CHUNK - 1:CHUNK] if d == 0 else acum_c[0:1]
        w_c = jnp.exp(atot_c - acum_c) * dcol
        etot_c = jnp.exp(atot_c)

        def head_col(arr, hd):
            lane = d * SSM_HEADS + hd
            return arr[:, lane:lane + 1]

        def pair_lanes(arr, p):
            return jnp.where(lo_half[:arr.shape[0]], head_col(arr, 2 * p), head_col(arr, 2 * p + 1))

        ys = []
        for g in range(SSM_GROUPS):
            bm = xc[:, SSM_INNER + g * SSM_N:SSM_INNER + (g + 1) * SSM_N]
            cm = xc[:, SSM_INNER + (SSM_GROUPS + g) * SSM_N:SSM_INNER + (SSM_GROUPS + g + 1) * SSM_N]
            bmb = bm.astype(BF16)
            cmb = cm.astype(BF16)
            cb = lax.dot_general(cmb, bmb, (((1,), (1,)), ((), ())), preferred_element_type=F32)
            bt = bm.T.astype(BF16)
            for pp in range(2):
                p = g * 2 + pp
                sc, a_bc = [], []
                for hh in range(2):
                    hd = 2 * p + hh
                    a_bc.append(jnp.broadcast_to(head_col(acum_c, hd), (CHUNK, CHUNK)))
                    diff = a_bc[hh] - acum_r[d * SSM_HEADS + hd:d * SSM_HEADS + hd + 1]
                    dec = jnp.exp(jnp.where(mask, diff, -jnp.inf))
                    sc.append(cb * dec * drow[d * SSM_HEADS + hd:d * SSM_HEADS + hd + 1])
                x_p = xc[:, p * 128:(p + 1) * 128]
                xw = (x_p * pair_lanes(w_c, p)).astype(BF16)
                s_t = jnp.dot(bt, xw, preferred_element_type=F32)
                h_prev = h_ref[d, p]
                e_p = jnp.exp(jnp.where(lo_half, a_bc[0], a_bc[1]))
                y_inter = jnp.dot(cmb, h_prev.astype(BF16), preferred_element_type=F32) * e_p
                pmat = jnp.concatenate(sc, axis=1).astype(BF16)
                xbd = jnp.concatenate([jnp.where(lo_half, x_p, 0.0), jnp.where(lo_half, 0.0, x_p)],
                                      axis=0).astype(BF16)
                ys.append(jnp.dot(pmat, xbd, preferred_element_type=F32) + y_inter)
                h_ref[d, p] = h_prev * pair_lanes(etot_c, p) + s_t
        y = jnp.concatenate(ys, axis=1)
        if d == 0:
            y = y + dfull_ref[...] * xc[:, :SSM_INNER]
        y_ref[0, pl.ds(r0, CHUNK), :] += y

    h_ref[...] = jnp.zeros_like(h_ref)
    y_ref[...] = jnp.zeros_like(y_ref)

    def scan_body(i, carry):
        chunk_step(i, 0)
        chunk_step(jnp.where(i < CTX_CHUNKS, CTX_CHUNKS - 1 - i, N_CHUNKS + CTX_CHUNKS - 1 - i), 1)
        return carry

    lax.fori_loop(0, N_CHUNKS, scan_body, 0, unroll=2)


def _ssd(xc, dt, dtt, w, l):
    return pl.pallas_call(
        _ssd_kernel,
        grid=(BATCH,),
        in_specs=[
            pl.BlockSpec((1, T, XBC), lambda b: (b, 0, 0)),
            pl.BlockSpec((1, T, 128), lambda b: (b, 0, 0)),
            pl.BlockSpec((1, 2 * SSM_HEADS, T), lambda b: (b, 0, 0)),
            _const_spec((1, 128), l),
            _const_spec((2 * SSM_HEADS, 1), l),
            _const_spec((1, 128), l),
            _const_spec((2 * SSM_HEADS, 1), l),
            _const_spec((1, SSM_INNER), l),
        ],
        out_specs=pl.BlockSpec((1, T, SSM_INNER), lambda b: (b, 0, 0)),
        out_shape=jax.ShapeDtypeStruct((BATCH, T, SSM_INNER), F32),
        scratch_shapes=[
            pltpu.VMEM((2, SSM_HEADS // 2, SSM_N, 2 * SSM_P), F32),
        ],
        compiler_params=_params(("arbitrary",)),
        name="ssd_bidir",
    )(xc, dt, dtt, w["dt_bias_row"], w["dt_bias_col"], w["a_log_row"], w["a_log_col"], w["d_full"])


def _mixout_kernel(x_ref, mod_ref, a_ref, y_ref, z_ref, uv_ref, gate_ref,
                   wmo_ref, sn_ref, wso_ref, gn_ref, ws_ref, bs_ref, wgo_ref, bg_ref, wout_ref, o_ref):
    x = x_ref[0]
    m = mod_ref[0]
    o_mla = jnp.dot(a_ref[0], wmo_ref[...], preferred_element_type=F32)

    ys = y_ref[0] * _silu(z_ref[0].astype(F32))
    o_ssm = _bdot(_rms(ys, sn_ref[...]), wso_ref[...])

    uv = jax.nn.gelu(uv_ref[0].astype(F32))
    u = uv[:, :GM_WIDTH]
    v = uv[:, GM_WIDTH:]
    v = v - jnp.mean(v, axis=-1, keepdims=True)
    v = v * lax.rsqrt(jnp.mean(v * v, axis=-1, keepdims=True) + EPS) * gn_ref[...]
    lo_half = lax.broadcasted_iota(jnp.int32, (CHUNK, 128), 1) < GM_GDIM
    mixed = []
    for ch in range(TM // CHUNK):
        parts = []
        for p in range(GM_GROUPS // 2):
            vp = v[ch * CHUNK:(ch + 1) * CHUNK, p * 128:(p + 1) * 128]
            vbd = jnp.concatenate([jnp.where(lo_half, vp, 0.0), jnp.where(lo_half, 0.0, vp)], axis=0)
            parts.append(jnp.dot(ws_ref[p], vbd.astype(BF16), preferred_element_type=F32))
        mixed.append(jnp.concatenate(parts, axis=1) + bs_ref[...])
    mixed = jnp.concatenate(mixed, axis=0)
    o_gm = _bdot(u * mixed, wgo_ref[...])

    gts = jax.nn.sigmoid(gate_ref[0].astype(F32) + bg_ref[...])
    merged = gts[:, :D] * o_mla + gts[:, D:2 * D] * o_ssm + gts[:, 2 * D:] * o_gm
    y = _bdot(merged, wout_ref[...])
    o_ref[0] = x + m[5:6] * y


def _mixout(xs, a, y, z, uv, gate, w, l, *, t0):
    nt = T // TM - t0
    return pl.pallas_call(
        _mixout_kernel,
        grid=(BATCH, nt),
        in_specs=[
            _row_spec(D, t0),
            _mod_spec(l, t0),
            pl.BlockSpec((1, TM, HEADS * VDIM), lambda b, t: (b, t, 0)),
            _row_spec(SSM_INNER, t0),
            _row_spec(SSM_INNER, t0),
            _row_spec(2 * GM_WIDTH, t0),
            _row_spec(N_BRANCH * D, t0),
            _const_spec((HEADS * VDIM, D), l),
            _const_spec((1, SSM_INNER), l),
            _const_spec((SSM_INNER, D), l),
            _const_spec((1, GM_WIDTH), l),
            _const_spec((GM_GROUPS // 2, CHUNK, 2 * CHUNK), l),
            _const_spec((CHUNK, GM_WIDTH), l),
            _const_spec((GM_WIDTH, D), l),
            _const_spec((1, N_BRANCH * D), l),
            _const_spec((D, D), l),
        ],
        out_specs=pl.BlockSpec((1, TM, D), lambda b, t: (b, t, 0)),
        out_shape=jax.ShapeDtypeStruct((BATCH, nt * TM, D), F32),
        compiler_params=_params(("parallel", "parallel")),
        name="mixer_out",
    )(xs, w["mods"], a, y, z, uv, gate, w["w_mla_o"], w["ssm_norm"], w["w_ssm_o"], w["gm_norm"], w["w_s"],
      w["b_s"], w["w_gm_o"], w["b_gate"], w["w_out"])


def _rot_half(w):
    half = ROPE // 2
    return jnp.concatenate([-w[..., half:], w[..., :half]], axis=-1)


def _head_pad(nope, rope):
    pad = jnp.zeros(nope.shape[:-1] + (HEAD_PAD - NOPE - ROPE,), nope.dtype)
    out = jnp.concatenate([nope, rope, pad], axis=-1)
    return out.reshape(out.shape[:-2] + (HEADS * HEAD_PAD,))


def _prep_weights(w_in, mla_q_norm, mla_w_uq, mla_kv_norm, mla_w_ukv, mla_w_o, ssm_conv_w, ssm_conv_b,
                  ssm_a_log, ssm_dt_bias, ssm_d, ssm_norm, ssm_w_o, gm_norm, gm_w_s, gm_b_s, gm_w_o, b_gate, w_out):
    w = w_in
    o = 0
    kv = w[..., o:o + KV_RANK]; o += KV_RANK
    kr = w[..., o:o + ROPE]; o += ROPE
    xbc = w[..., o:o + XBC]; o += XBC
    dtw = w[..., o:o + 2 * SSM_HEADS]; o += 2 * SSM_HEADS
    rest = w[..., o:]

    def lanes(cols, start):
        return jnp.pad(cols, ((0, 0), (0, 0), (start, 128 - start - cols.shape[-1])))

    pieces = [kv, lanes(kr, NOPE), lanes(_rot_half(kr), NOPE), lanes(dtw, 0), xbc, rest]
    w_new = jnp.concatenate([p.astype(BF16) for p in pieces], axis=-1)

    uq = mla_w_uq.astype(BF16).reshape(DEPTH, Q_RANK, HEADS, NOPE + ROPE)
    uq_n, uq_r = uq[..., :NOPE], uq[..., NOPE:]
    ukv = mla_w_ukv.astype(BF16).reshape(DEPTH, KV_RANK, HEADS, NOPE + VDIM)
    kn = _head_pad(ukv[..., :NOPE], jnp.zeros((DEPTH, KV_RANK, HEADS, ROPE), BF16))
    v_t = jnp.pad(jnp.transpose(ukv[..., NOPE:], (0, 2, 3, 1)), ((0, 0), (0, 0), (0, V_ROWS - VDIM), (0, 0)))
    v_ones = jnp.tile((jnp.arange(V_ROWS) == VDIM).astype(F32), HEADS).reshape(HEADS * V_ROWS, 1)

    ws = gm_w_s.astype(BF16)
    ws_pair = jnp.concatenate([ws[:, 0::2], ws[:, 1::2]], axis=3)

    def row128(vals, fill=0.0):
        vals = vals.reshape(DEPTH, 1, -1)
        return jnp.pad(vals, ((0, 0), (0, 0), (0, 128 - vals.shape[-1])), constant_values=fill)

    return {
        "w_in": w_new,
        "kv_norm": mla_kv_norm.reshape(DEPTH, 1, KV_RANK),
        "w_kn": kn,
        "w_v_t": v_t.reshape(DEPTH, HEADS * V_ROWS, KV_RANK),
        "v_ones": v_ones,
        "q_norm": mla_q_norm.reshape(DEPTH, 1, Q_RANK),
        "w_q": _head_pad(uq_n, uq_r),
        "w_q_rot": _head_pad(jnp.zeros_like(uq_n), _rot_half(uq_r)),
        "conv_w": jnp.pad(jnp.swapaxes(ssm_conv_w, 1, 2), ((0, 0), (0, 8 - SSM_CONV), (0, 0))),
        "conv_b": ssm_conv_b.reshape(DEPTH, 1, XBC),
        "dt_bias_row": row128(ssm_dt_bias),
        "dt_bias_col": ssm_dt_bias.reshape(DEPTH, 2 * SSM_HEADS, 1),
        "a_log_row": row128(ssm_a_log, fill=-80.0),
        "a_log_col": ssm_a_log.reshape(DEPTH, 2 * SSM_HEADS, 1),
        "d_full": jnp.repeat(ssm_d, SSM_P, axis=1).reshape(DEPTH, 1, SSM_INNER),
        "w_mla_o": mla_w_o.astype(BF16),
        "ssm_norm": ssm_norm.reshape(DEPTH, 1, SSM_INNER),
        "w_ssm_o": ssm_w_o.astype(BF16),
        "gm_norm": gm_norm.reshape(DEPTH, 1, GM_WIDTH),
        "w_s": ws_pair,
        "b_s": jnp.repeat(jnp.swapaxes(gm_b_s, 1, 2), GM_GDIM, axis=2),
        "w_gm_o": gm_w_o.astype(BF16),
        "b_gate": b_gate.reshape(DEPTH, 1, N_BRANCH * D),
        "w_out": w_out.astype(BF16),
    }


def _rope_tables():
    rows = SEQ // GRID_W
    r = jnp.repeat(jnp.arange(rows, dtype=F32), GRID_W)
    c = jnp.tile(jnp.arange(GRID_W, dtype=F32), rows)
    n_freq = ROPE // 4
    inv = jnp.power(ROPE_BASE, -jnp.arange(n_freq, dtype=F32) / n_freq)
    ang = jnp.concatenate([r[:, None] * inv, c[:, None] * inv], axis=-1)
    cos, sin = jnp.cos(ang), jnp.sin(ang)
    cos = jnp.concatenate([jnp.ones((CTX, ROPE // 2), F32), cos], axis=0)
    sin = jnp.concatenate([jnp.zeros((CTX, ROPE // 2), F32), sin], axis=0)
    ones = jnp.ones((T, NOPE), F32)
    zeros_n = jnp.zeros((T, NOPE), F32)
    zeros_p = jnp.zeros((T, HEAD_PAD - NOPE - ROPE), F32)
    cos_t = jnp.concatenate([ones, cos, cos, zeros_p], axis=1)
    sin_t = jnp.concatenate([zeros_n, sin, sin, zeros_p], axis=1)
    return cos_t, sin_t


def kernel(x, c, ctx, c_ctx, w_ada, b_ada, norm_g, ffn1_w_in, ffn1_w_out, ffn2_w_in, ffn2_w_out, w_in, mla_q_norm,
           mla_w_uq, mla_kv_norm, mla_w_ukv, mla_w_o, ssm_conv_w, ssm_conv_b, ssm_a_log, ssm_dt_bias, ssm_d,
           ssm_norm, ssm_w_o, gm_norm, gm_w_s, gm_b_s, gm_w_o, b_gate, w_out, final_norm):
    c_all = jnp.concatenate([c, c_ctx[None, :], jnp.zeros((MOD_ROWS - BATCH - 1, D), F32)], axis=0)
    w = _prep_weights(w_in, mla_q_norm, mla_w_uq, mla_kv_norm, mla_w_ukv, mla_w_o, ssm_conv_w, ssm_conv_b, ssm_a_log,
                      ssm_dt_bias, ssm_d, ssm_norm, ssm_w_o, gm_norm, gm_w_s, gm_b_s, gm_w_o, b_gate, w_out)
    w["mods"] = _ada(c_all, w_ada, b_ada).reshape(DEPTH, MOD_ROWS, N_MOD, D)
    w["norm_g"] = norm_g.reshape(DEPTH * 3, 1, D)
    w["ffn_w_in"] = (ffn1_w_in.astype(BF16), ffn2_w_in.astype(BF16))
    w["ffn_w_out"] = (ffn1_w_out.astype(BF16), ffn2_w_out.astype(BF16))
    w["final_norm"] = final_norm.reshape(1, D)
    w["cos"], w["sin"] = _rope_tables()

    xs = (ctx, x)
    for l in range(DEPTH):
        last = l == DEPTH - 1
        t0 = 1 if last else 0
        xs = _ffn(xs, w, l, 0)
        q, k, vt, xbc, dt, dtt, z, uv, gate = _proj(xs, w, l)
        a = _attn(q, k, vt, t0=t0)
        y = _ssd(xbc, dt, dtt, w, l)
        xs = _mixout(xs, a, y, z, uv, gate, w, l, t0=t0)
        xs = _ffn(xs, w, l, 1, latent_only=last, final=last)
    return xs
```

```python
import functools
import math

import jax
import jax.numpy as jnp
from jax import lax
from jax.experimental import pallas as pl
from jax.experimental.pallas import tpu as pltpu

F32 = jnp.float32
BF16 = jnp.bfloat16

D = 1024
BATCH = 8
SEQ = 2048
DEPTH = 2
CTX = 256
T = CTX + SEQ
GRID_W = 64
EPS = 1e-6

HEADS = 8
NOPE = 64
ROPE = 32
VDIM = 64
Q_RANK = 256
KV_RANK = 128
ROPE_BASE = 10000.0
ATTN_SCALE = (NOPE + ROPE) ** -0.5
EXP2_SCALE = ATTN_SCALE * math.log2(math.e)
HEAD_PAD = 128
V_ROWS = 80

SSM_HEADS = 8
SSM_P = 64
SSM_INNER = SSM_HEADS * SSM_P
SSM_GROUPS = 2
SSM_N = 128
SSM_CONV = 5
CHUNK = 128
XBC = SSM_INNER + 2 * SSM_GROUPS * SSM_N
N_CHUNKS = T // CHUNK
CTX_CHUNKS = CTX // CHUNK
HALO = 8

GM_GROUPS = 8
GM_WIDTH = 512
GM_GDIM = GM_WIDTH // GM_GROUPS

D_FF = 2816
N_BRANCH = 3
N_MOD = 9
MOD_ROWS = 16
CTX_MOD_ROW = BATCH

KV_SIDE = KV_RANK + ROPE + XBC + 2 * SSM_HEADS

C_KV = 0
C_KR = 128
C_KRR = 256
C_DT = 384
C_XBC = 512
C_Q = C_XBC + XBC
C_Z = C_Q + Q_RANK
C_UV = C_Z + SSM_INNER
C_GATE = C_UV + 2 * GM_WIDTH
C_END = C_GATE + N_BRANCH * D

TM = 256
TM_LATENT = 512
KEY_BLOCK = 128
PROJ_PIECE = 512
CONV_ROWS = 64
VMEM_LIMIT = 56 * 1024 * 1024


def _rms(x, g):
    y = x * lax.rsqrt(jnp.mean(x * x, axis=-1, keepdims=True) + EPS)
    return y * g


def _silu(x):
    return x * jax.nn.sigmoid(x)


def _bdot(a, b):
    return jnp.dot(a.astype(BF16), b.astype(BF16), preferred_element_type=F32)


def _split3(a):
    a1 = a.astype(BF16)
    r1 = a - a1.astype(F32)
    a2 = r1.astype(BF16)
    a3 = (r1 - a2.astype(F32)).astype(BF16)
    return a1, a2, a3


def _dot_right01(a, m01):
    return sum(jnp.dot(p, m01, preferred_element_type=F32) for p in _split3(a))


def _dot_left01(m01, a):
    return sum(jnp.dot(m01, p, preferred_element_type=F32) for p in _split3(a))


def _const_spec(shape, l=None):
    nd = len(shape)
    if l is None:
        return pl.BlockSpec(shape, lambda *_: (0,) * nd, pipeline_mode=pl.Buffered(1))
    return pl.BlockSpec((None,) + shape, lambda *_: (l,) + (0,) * nd, pipeline_mode=pl.Buffered(1))


def _mod_spec(l, t0):
    return pl.BlockSpec((None, 1, N_MOD, D), lambda b, t: (l, jnp.where(t + t0 == 0, CTX_MOD_ROW, b), 0, 0))


def _row_spec(width, t0):
    return pl.BlockSpec((1, TM, width), lambda b, t: (b, t + t0, 0))


def _params(sem):
    return pltpu.CompilerParams(dimension_semantics=sem, vmem_limit_bytes=VMEM_LIMIT)


def _ada_kernel(c_ref, w_ref, b_ref, o_ref):
    s = _silu(c_ref[...])
    o_ref[0] = _bdot(s, w_ref[0]) + b_ref[0]


def _ada(c_all, w_ada, b_ada):
    tn = 1152
    nn = N_MOD * D // tn
    return pl.pallas_call(
        _ada_kernel,
        grid=(DEPTH, nn),
        in_specs=[
            pl.BlockSpec((MOD_ROWS, D), lambda l, j: (0, 0)),
            pl.BlockSpec((1, D, tn), lambda l, j: (l, 0, j)),
            pl.BlockSpec((1, 1, tn), lambda l, j: (l, 0, j)),
        ],
        out_specs=pl.BlockSpec((1, MOD_ROWS, tn), lambda l, j: (l, 0, j)),
        out_shape=jax.ShapeDtypeStruct((DEPTH, MOD_ROWS, N_MOD * D), F32),
        compiler_params=_params(("arbitrary", "arbitrary")),
        name="ada_mod",
    )(c_all, w_ada, b_ada.reshape(DEPTH, 1, N_MOD * D))


def _ffn_kernel(*refs, mod0, final, split_input):
    if split_input:
        ctx_ref, x_ref, mod_ref, g_ref, wi_ref, wo_ref, fin_ref, o_ref = refs
        x = jnp.where(pl.program_id(1) == 0, ctx_ref[0], x_ref[0])
    else:
        x_ref, mod_ref, g_ref, wi_ref, wo_ref, fin_ref, o_ref = refs
        x = x_ref[0]
    m = mod_ref[0]
    shift, scale, gate = m[mod0:mod0 + 1], m[mod0 + 1:mod0 + 2], m[mod0 + 2:mod0 + 3]
    h = _rms(x, g_ref[...]) * (1.0 + scale) + shift
    gu = jnp.dot(h.astype(BF16), wi_ref[...], preferred_element_type=F32)
    a = _silu(gu[:, :D_FF]) * gu[:, D_FF:]
    y = jnp.dot(a.astype(BF16), wo_ref[...], preferred_element_type=F32)
    out = x + (0.5 * gate) * y
    if final:
        out = _rms(out, fin_ref[...])
    o_ref[0] = out


def _ffn(xs, w, l, which, *, latent_only=False, final=False):
    split_input = isinstance(xs, tuple)
    t0 = 1 if latent_only else 0
    tm = TM_LATENT if latent_only else TM
    nt = SEQ // tm if latent_only else T // TM
    if split_input:
        x_specs = [pl.BlockSpec((1, TM, D), lambda b, t: (b, 0, 0)),
                   pl.BlockSpec((1, TM, D), lambda b, t: (b, jnp.maximum(t - 1, 0), 0))]
        xs_args = list(xs)
    else:
        x_specs = [pl.BlockSpec((1, tm, D), lambda b, t: (b, t, 0))]
        xs_args = [xs]
    return pl.pallas_call(
        functools.partial(_ffn_kernel, mod0=6 * which, final=final, split_input=split_input),
        grid=(BATCH, nt),
        in_specs=x_specs + [
            _mod_spec(l, t0),
            _const_spec((1, D), 3 * l + 2 * which),
            _const_spec((D, 2 * D_FF), l),
            _const_spec((D_FF, D), l),
            _const_spec((1, D)),
        ],
        out_specs=pl.BlockSpec((1, tm, D), lambda b, t: (b, t, 0)),
        out_shape=jax.ShapeDtypeStruct((BATCH, nt * tm, D), F32),
        compiler_params=_params(("parallel", "parallel")),
        name="ffn",
    )(*xs_args, w["mods"], w["norm_g"], w["ffn_w_in"][which], w["ffn_w_out"][which], w["final_norm"])


def _proj_kernel(x_ref, xprev_ref, xnext_ref, mod_ref, g_ref, win_ref, cw_ref, cb_ref, kvn_ref, wkn_ref, wvt_ref,
                 one_ref, qn_ref, wq_ref, wqr_ref, cos_ref, sin_ref,
                 q_out, k_out, vt_out, xc_out, dt_out, dtt_out, z_out, uv_out, gate_out):
    t = pl.program_id(1)
    m = mod_ref[0]
    xe = jnp.concatenate([xprev_ref[0], x_ref[0], xnext_ref[0]], axis=0)
    he = _rms(xe, g_ref[...]) * (1.0 + m[4:5]) + m[3:4]
    hb = he[HALO:HALO + TM].astype(BF16)
    he = he.astype(BF16)
    cos = cos_ref[...]
    sin = sin_ref[...]

    def proj(c0, c1):
        return jnp.dot(hb, win_ref[:, c0:c1], preferred_element_type=F32)

    small = proj(C_KV, C_XBC)

    xbc = jnp.dot(he, win_ref[:, C_XBC:C_Q], preferred_element_type=F32)
    row = lax.broadcasted_iota(jnp.int32, (TM + 2 * HALO, 1), 0)
    has_prev = t >= 2
    has_next = (t >= 1) & (t < T // TM - 1)
    keep = ((row >= HALO) | has_prev) & ((row < HALO + TM) | has_next)
    def conv_lane_tile(c0):
        for r0 in range(0, TM, CONV_ROWS):
            win = xbc[r0:r0 + CONV_ROWS + 2 * HALO, c0:c0 + 128]
            if r0 == 0 or r0 + CONV_ROWS == TM:
                win = jnp.where(keep[r0:r0 + CONV_ROWS + 2 * HALO], win, 0.0)
            acc = cb_ref[:, c0:c0 + 128]
            for k in range(SSM_CONV):
                shift = (SSM_CONV // 2 - k) % (CONV_ROWS + 2 * HALO)
                tap = win if shift == 0 else pltpu.roll(win, shift, axis=0)
                acc = acc + cw_ref[k:k + 1, c0:c0 + 128] * tap[HALO:HALO + CONV_ROWS]
            xc_out[0, r0:r0 + CONV_ROWS, c0:c0 + 128] = _silu(acc)

    q_lat = proj(C_Q, C_Z)
    out_pieces = [(z_out, C_Z, SSM_INNER), (uv_out, C_UV, 2 * GM_WIDTH), (gate_out, C_GATE, N_BRANCH * D)]
    conv_tiles = list(range(0, XBC, 128))
    for out, base, width in out_pieces:
        for c0 in range(0, width, PROJ_PIECE):
            out[0, :, c0:c0 + PROJ_PIECE] = proj(base + c0, base + c0 + PROJ_PIECE).astype(out.dtype)
            if conv_tiles:
                conv_lane_tile(conv_tiles.pop(0))
    assert not conv_tiles

    dt = small[:, C_DT:C_DT + 128]
    dt_out[0] = dt
    dtt_out[0] = dt.T[:2 * SSM_HEADS]

    kvn = _rms(small[:, C_KV:C_KV + 128], kvn_ref[...]).astype(BF16)
    kn = jnp.dot(kvn, wkn_ref[...], preferred_element_type=F32)
    kr = small[:, C_KR:C_KR + 128] * cos + small[:, C_KRR:C_KRR + 128] * sin
    for hd in range(HEADS):
        sl = slice(hd * HEAD_PAD, (hd + 1) * HEAD_PAD)
        k_out[0, :, sl] = (kn[:, sl] + kr).astype(BF16)
    vt = lax.dot_general(wvt_ref[...], kvn, (((1,), (1,)), ((), ())), preferred_element_type=F32)
    vt_out[0] = (vt + one_ref[...]).astype(BF16)

    qn = _rms(q_lat, qn_ref[...]).astype(BF16)
    qa = jnp.dot(qn, wq_ref[...], preferred_element_type=F32)
    qb = jnp.dot(qn, wqr_ref[...], preferred_element_type=F32)
    for hd in range(HEADS):
        sl = slice(hd * HEAD_PAD, (hd + 1) * HEAD_PAD)
        q_out[0, :, sl] = (qa[:, sl] * cos + qb[:, sl] * sin).astype(BF16)


def _proj(xs, w, l):
    nt = T // TM

    def rows(w, dt):
        return jax.ShapeDtypeStruct((BATCH, T, w), dt), _row_spec(w, 0)

    def cols(r, dt):
        return jax.ShapeDtypeStruct((BATCH, r, T), dt), pl.BlockSpec((1, r, TM), lambda b, t: (b, 0, t))

    outs = [rows(HEADS * HEAD_PAD, BF16), rows(HEADS * HEAD_PAD, BF16), cols(HEADS * V_ROWS, BF16),
            rows(XBC, F32), rows(128, F32), cols(2 * SSM_HEADS, F32),
            rows(SSM_INNER, BF16), rows(2 * GM_WIDTH, BF16), rows(N_BRANCH * D, F32)]
    out_shape = [o[0] for o in outs]
    out_specs = [o[1] for o in outs]
    return pl.pallas_call(
        _proj_kernel,
        grid=(BATCH, nt),
        in_specs=[
            _row_spec(D, 0),
            pl.BlockSpec((1, HALO, D), lambda b, t: (b, jnp.maximum(t * (TM // HALO) - 1, 0), 0)),
            pl.BlockSpec((1, HALO, D), lambda b, t: (b, jnp.minimum((t + 1) * (TM // HALO), T // HALO - 1), 0)),
            _mod_spec(l, 0),
            _const_spec((1, D), 3 * l + 1),
            _const_spec((D, C_END), l),
            _const_spec((8, XBC), l),
            _const_spec((1, XBC), l),
            _const_spec((1, KV_RANK), l),
            _const_spec((KV_RANK, HEADS * HEAD_PAD), l),
            _const_spec((HEADS * V_ROWS, KV_RANK), l),
            _const_spec((HEADS * V_ROWS, 1)),
            _const_spec((1, Q_RANK), l),
            _const_spec((Q_RANK, HEADS * HEAD_PAD), l),
            _const_spec((Q_RANK, HEADS * HEAD_PAD), l),
            pl.BlockSpec((TM, HEAD_PAD), lambda b, t: (t, 0)),
            pl.BlockSpec((TM, HEAD_PAD), lambda b, t: (t, 0)),
        ],
        out_specs=out_specs,
        out_shape=out_shape,
        compiler_params=_params(("parallel", "parallel")),
        name="mixer_proj",
    )(xs, xs, xs, w["mods"], w["norm_g"], w["w_in"], w["conv_w"], w["conv_b"], w["kv_norm"], w["w_kn"], w["w_v_t"],
      w["v_ones"],
      w["q_norm"], w["w_q"], w["w_q_rot"], w["cos"], w["sin"])


def _attn_kernel(q_ref, k_ref, vt_ref, o_ref, *, t0):
    def run(nk):
        nblk = nk // KEY_BLOCK

        def score_block(hd, j):
            q = q_ref[0, :, hd * HEAD_PAD:(hd + 1) * HEAD_PAD]
            k = k_ref[0, j * KEY_BLOCK:(j + 1) * KEY_BLOCK, hd * HEAD_PAD:(hd + 1) * HEAD_PAD]
            return lax.dot_general(k, q, (((1,), (1,)), ((), ())), preferred_element_type=F32)

        def col_max(blocks):
            mx = functools.reduce(jnp.maximum, blocks)
            mx = jnp.max(mx.reshape(KEY_BLOCK // 8, 8, TM), axis=0)
            return jnp.max(mx, axis=0, keepdims=True)

        outs = []
        cur = [score_block(0, j) for j in range(nblk)]
        for hd in range(HEADS):
            mx = col_max(cur)
            nxt, ps = [], []
            for j in range(nblk):
                if hd + 1 < HEADS:
                    nxt.append(score_block(hd + 1, j))
                ps.append(jnp.exp2((cur[j] - mx) * EXP2_SCALE).astype(BF16))
            p = jnp.concatenate(ps, axis=0)
            ot = jnp.dot(vt_ref[0, hd * V_ROWS:(hd + 1) * V_ROWS, :nk], p, preferred_element_type=F32)
            outs.append(ot[:VDIM] / ot[VDIM:VDIM + 1])
            cur = nxt
        o_ref[0] = jnp.concatenate(outs, axis=0).T.astype(BF16)

    if t0 == 0:
        t = pl.program_id(1)
        pl.when(t == 0)(lambda: run(CTX))
        pl.when(t != 0)(lambda: run(T))
    else:
        run(T)


def _attn(q, k, v, *, t0):
    nt = T // TM - t0
    return pl.pallas_call(
        functools.partial(_attn_kernel, t0=t0),
        grid=(BATCH, nt),
        in_specs=[
            _row_spec(HEADS * HEAD_PAD, t0),
            pl.BlockSpec((1, T, HEADS * HEAD_PAD), lambda b, t: (b, 0, 0)),
            pl.BlockSpec((1, HEADS * V_ROWS, T), lambda b, t: (b, 0, 0)),
        ],
        out_specs=pl.BlockSpec((1, TM, HEADS * VDIM), lambda b, t: (b, t, 0)),
        out_shape=jax.ShapeDtypeStruct((BATCH, nt * TM, HEADS * VDIM), BF16),
        compiler_params=_params(("parallel", "arbitrary")),
        name="mla_attn",
    )(q, k, v)


def _ssd_kernel(xc_ref, dt_ref, dtt_ref, bias_row_ref, bias_col_ref, alog_row_ref, alog_col_ref, dfull_ref,
                y_ref, h_ref):
    row = lax.broadcasted_iota(jnp.int32, (CHUNK, CHUNK), 0)
    col = lax.broadcasted_iota(jnp.int32, (CHUNK, CHUNK), 1)
    lower = (col <= row)
    upper = (col >= row)
    lower01 = jnp.where(lower, 1.0, 0.0).astype(BF16)
    upper01 = jnp.where(upper, 1.0, 0.0).astype(BF16)
    lo_half = lax.broadcasted_iota(jnp.int32, (CHUNK, 128), 1) < SSM_P
    a_row = -jnp.exp(alog_row_ref[...])
    a_col = -jnp.exp(alog_col_ref[...])

    def chunk_step(c, d):
        r0 = pl.multiple_of(c * CHUNK, CHUNK)
        xc = xc_ref[0, pl.ds(r0, CHUNK), :]
        dcol = jax.nn.softplus(dt_ref[0, pl.ds(r0, CHUNK), :] + bias_row_ref[...])
        drow = jax.nn.softplus(dtt_ref[0, :, pl.ds(r0, CHUNK)] + bias_col_ref[...])
        tri_l, tri_r, mask = (lower01, upper01, lower) if d == 0 else (upper01, lower01, upper)
        acum_c = _dot_left01(tri_l, dcol * a_row)
        acum_r = _dot_right01(drow * a_col, tri_r)
        atot_c = acum_c[CHUNK - 1:CHUNK] if d == 0 else acum_c[0:1]
        w_c = jnp.exp(atot_c - acum_c) * dcol
        etot_c = jnp.exp(atot_c)

        def head_col(arr, hd):
            lane = d * SSM_HEADS + hd
            return arr[:, lane:lane + 1]

        def pair_lanes(arr, p):
            return jnp.where(lo_half[:arr.shape[0]], head_col(arr, 2 * p), head_col(arr, 2 * p + 1))

        ys = []
        for g in range(SSM_GROUPS):
            bm = xc[:, SSM_INNER + g * SSM_N:SSM_INNER + (g + 1) * SSM_N]
            cm = xc[:, SSM_INNER + (SSM_GROUPS + g) * SSM_N:SSM_INNER + (SSM_GROUPS + g + 1) * SSM_N]
            bmb = bm.astype(BF16)
            cmb = cm.astype(BF16)
            cb = lax.dot_general(cmb, bmb, (((1,), (1,)), ((), ())), preferred_element_type=F32)
            bt = bm.T.astype(BF16)
            for pp in range(2):
                p = g * 2 + pp
                sc, a_bc = [], []
                for hh in range(2):
                    hd = 2 * p + hh
                    a_bc.append(jnp.broadcast_to(head_col(acum_c, hd), (CHUNK, CHUNK)))
                    diff = a_bc[hh] - acum_r[d * SSM_HEADS + hd:d * SSM_HEADS + hd + 1]
                    dec = jnp.exp(jnp.where(mask, diff, -jnp.inf))
                    sc.append(cb * dec * drow[d * SSM_HEADS + hd:d * SSM_HEADS + hd + 1])
                x_p = xc[:, p * 128:(p + 1) * 128]
                xw = (x_p * pair_lanes(w_c, p)).astype(BF16)
                s_t = jnp.dot(bt, xw, preferred_element_type=F32)
                h_prev = h_ref[d, p]
                e_p = jnp.exp(jnp.where(lo_half, a_bc[0], a_bc[1]))
                y_inter = jnp.dot(cmb, h_prev.astype(BF16), preferred_element_type=F32) * e_p
                pmat = jnp.concatenate(sc, axis=1).astype(BF16)
                xbd = jnp.concatenate([jnp.where(lo_half, x_p, 0.0), jnp.where(lo_half, 0.0, x_p)],
                                      axis=0).astype(BF16)
                ys.append(jnp.dot(pmat, xbd, preferred_element_type=F32) + y_inter)
                h_ref[d, p] = h_prev * pair_lanes(etot_c, p) + s_t
        y = jnp.concatenate(ys, axis=1)
        if d == 0:
            y = y + dfull_ref[...] * xc[:, :SSM_INNER]
        y_ref[0, pl.ds(r0, CHUNK), :] += y

    h_ref[...] = jnp.zeros_like(h_ref)
    y_ref[...] = jnp.zeros_like(y_ref)

    def scan_body(i, carry):
        chunk_step(i, 0)
        chunk_step(jnp.where(i < CTX_CHUNKS, CTX_CHUNKS - 1 - i, N_CHUNKS + CTX_CHUNKS - 1 - i), 1)
        return carry

    lax.fori_loop(0, N_CHUNKS, scan_body, 0, unroll=2)


def _ssd(xc, dt, dtt, w, l):
    return pl.pallas_call(
        _ssd_kernel,
        grid=(BATCH,),
        in_specs=[
            pl.BlockSpec((1, T, XBC), lambda b: (b, 0, 0)),
            pl.BlockSpec((1, T, 128), lambda b: (b, 0, 0)),
            pl.BlockSpec((1, 2 * SSM_HEADS, T), lambda b: (b, 0, 0)),
            _const_spec((1, 128), l),
            _const_spec((2 * SSM_HEADS, 1), l),
            _const_spec((1, 128), l),
            _const_spec((2 * SSM_HEADS, 1), l),
            _const_spec((1, SSM_INNER), l),
        ],
        out_specs=pl.BlockSpec((1, T, SSM_INNER), lambda b: (b, 0, 0)),
        out_shape=jax.ShapeDtypeStruct((BATCH, T, SSM_INNER), F32),
        scratch_shapes=[
            pltpu.VMEM((2, SSM_HEADS // 2, SSM_N, 2 * SSM_P), F32),
        ],
        compiler_params=_params(("arbitrary",)),
        name="ssd_bidir",
    )(xc, dt, dtt, w["dt_bias_row"], w["dt_bias_col"], w["a_log_row"], w["a_log_col"], w["d_full"])


def _mixout_kernel(x_ref, mod_ref, a_ref, y_ref, z_ref, uv_ref, gate_ref,
                   wmo_ref, sn_ref, wso_ref, gn_ref, ws_ref, bs_ref, wgo_ref, bg_ref, wout_ref, o_ref):
    x = x_ref[0]
    m = mod_ref[0]
    o_mla = jnp.dot(a_ref[0], wmo_ref[...], preferred_element_type=F32)

    ys = y_ref[0] * _silu(z_ref[0].astype(F32))
    o_ssm = _bdot(_rms(ys, sn_ref[...]), wso_ref[...])

    uv = jax.nn.gelu(uv_ref[0].astype(F32))
    u = uv[:, :GM_WIDTH]
    v = uv[:, GM_WIDTH:]
    v = v - jnp.mean(v, axis=-1, keepdims=True)
    v = v * lax.rsqrt(jnp.mean(v * v, axis=-1, keepdims=True) + EPS) * gn_ref[...]
    lo_half = lax.broadcasted_iota(jnp.int32, (CHUNK, 128), 1) < GM_GDIM
    mixed = []
    for ch in range(TM // CHUNK):
        parts = []
        for p in range(GM_GROUPS // 2):
            vp = v[ch * CHUNK:(ch + 1) * CHUNK, p * 128:(p + 1) * 128]
            vbd = jnp.concatenate([jnp.where(lo_half, vp, 0.0), jnp.where(lo_half, 0.0, vp)], axis=0)
            parts.append(jnp.dot(ws_ref[p], vbd.astype(BF16), preferred_element_type=F32))
        mixed.append(jnp.concatenate(parts, axis=1) + bs_ref[...])
    mixed = jnp.concatenate(mixed, axis=0)
    o_gm = _bdot(u * mixed, wgo_ref[...])

    gts = jax.nn.sigmoid(gate_ref[0].astype(F32) + bg_ref[...])
    merged = gts[:, :D] * o_mla + gts[:, D:2 * D] * o_ssm + gts[:, 2 * D:] * o_gm
    y = _bdot(merged, wout_ref[...])
    o_ref[0] = x + m[5:6] * y


def _mixout(xs, a, y, z, uv, gate, w, l, *, t0):
    nt = T // TM - t0
    return pl.pallas_call(
        _mixout_kernel,
        grid=(BATCH, nt),
        in_specs=[
            _row_spec(D, t0),
            _mod_spec(l, t0),
            pl.BlockSpec((1, TM, HEADS * VDIM), lambda b, t: (b, t, 0)),
            _row_spec(SSM_INNER, t0),
            _row_spec(SSM_INNER, t0),
            _row_spec(2 * GM_WIDTH, t0),
            _row_spec(N_BRANCH * D, t0),
            _const_spec((HEADS * VDIM, D), l),
            _const_spec((1, SSM_INNER), l),
            _const_spec((SSM_INNER, D), l),
            _const_spec((1, GM_WIDTH), l),
            _const_spec((GM_GROUPS // 2, CHUNK, 2 * CHUNK), l),
            _const_spec((CHUNK, GM_WIDTH), l),
            _const_spec((GM_WIDTH, D), l),
            _const_spec((1, N_BRANCH * D), l),
            _const_spec((D, D), l),
        ],
        out_specs=pl.BlockSpec((1, TM, D), lambda b, t: (b, t, 0)),
        out_shape=jax.ShapeDtypeStruct((BATCH, nt * TM, D), F32),
        compiler_params=_params(("parallel", "parallel")),
        name="mixer_out",
    )(xs, w["mods"], a, y, z, uv, gate, w["w_mla_o"], w["ssm_norm"], w["w_ssm_o"], w["gm_norm"], w["w_s"],
      w["b_s"], w["w_gm_o"], w["b_gate"], w["w_out"])


def _rot_half(w):
    half = ROPE // 2
    return jnp.concatenate([-w[..., half:], w[..., :half]], axis=-1)


def _head_pad(nope, rope):
    pad = jnp.zeros(nope.shape[:-1] + (HEAD_PAD - NOPE - ROPE,), nope.dtype)
    out = jnp.concatenate([nope, rope, pad], axis=-1)
    return out.reshape(out.shape[:-2] + (HEADS * HEAD_PAD,))


def _prep_weights(w_in, mla_q_norm, mla_w_uq, mla_kv_norm, mla_w_ukv, mla_w_o, ssm_conv_w, ssm_conv_b,
                  ssm_a_log, ssm_dt_bias, ssm_d, ssm_norm, ssm_w_o, gm_norm, gm_w_s, gm_b_s, gm_w_o, b_gate, w_out):
    w = w_in
    o = 0
    kv = w[..., o:o + KV_RANK]; o += KV_RANK
    kr = w[..., o:o + ROPE]; o += ROPE
    xbc = w[..., o:o + XBC]; o += XBC
    dtw = w[..., o:o + 2 * SSM_HEADS]; o += 2 * SSM_HEADS
    rest = w[..., o:]

    def lanes(cols, start):
        return jnp.pad(cols, ((0, 0), (0, 0), (start, 128 - start - cols.shape[-1])))

    pieces = [kv, lanes(kr, NOPE), lanes(_rot_half(kr), NOPE), lanes(dtw, 0), xbc, rest]
    w_new = jnp.concatenate([p.astype(BF16) for p in pieces], axis=-1)

    uq = mla_w_uq.astype(BF16).reshape(DEPTH, Q_RANK, HEADS, NOPE + ROPE)
    uq_n, uq_r = uq[..., :NOPE], uq[..., NOPE:]
    ukv = mla_w_ukv.astype(BF16).reshape(DEPTH, KV_RANK, HEADS, NOPE + VDIM)
    kn = _head_pad(ukv[..., :NOPE], jnp.zeros((DEPTH, KV_RANK, HEADS, ROPE), BF16))
    v_t = jnp.pad(jnp.transpose(ukv[..., NOPE:], (0, 2, 3, 1)), ((0, 0), (0, 0), (0, V_ROWS - VDIM), (0, 0)))
    v_ones = jnp.tile((jnp.arange(V_ROWS) == VDIM).astype(F32), HEADS).reshape(HEADS * V_ROWS, 1)

    ws = gm_w_s.astype(BF16)
    ws_pair = jnp.concatenate([ws[:, 0::2], ws[:, 1::2]], axis=3)

    def row128(vals, fill=0.0):
        vals = vals.reshape(DEPTH, 1, -1)
        return jnp.pad(vals, ((0, 0), (0, 0), (0, 128 - vals.shape[-1])), constant_values=fill)

    return {
        "w_in": w_new,
        "kv_norm": mla_kv_norm.reshape(DEPTH, 1, KV_RANK),
        "w_kn": kn,
        "w_v_t": v_t.reshape(DEPTH, HEADS * V_ROWS, KV_RANK),
        "v_ones": v_ones,
        "q_norm": mla_q_norm.reshape(DEPTH, 1, Q_RANK),
        "w_q": _head_pad(uq_n, uq_r),
        "w_q_rot": _head_pad(jnp.zeros_like(uq_n), _rot_half(uq_r)),
        "conv_w": jnp.pad(jnp.swapaxes(ssm_conv_w, 1, 2), ((0, 0), (0, 8 - SSM_CONV), (0, 0))),
        "conv_b": ssm_conv_b.reshape(DEPTH, 1, XBC),
        "dt_bias_row": row128(ssm_dt_bias),
        "dt_bias_col": ssm_dt_bias.reshape(DEPTH, 2 * SSM_HEADS, 1),
        "a_log_row": row128(ssm_a_log, fill=-80.0),
        "a_log_col": ssm_a_log.reshape(DEPTH, 2 * SSM_HEADS, 1),
        "d_full": jnp.repeat(ssm_d, SSM_P, axis=1).reshape(DEPTH, 1, SSM_INNER),
        "w_mla_o": mla_w_o.astype(BF16),
        "ssm_norm": ssm_norm.reshape(DEPTH, 1, SSM_INNER),
        "w_ssm_o": ssm_w_o.astype(BF16),
        "gm_norm": gm_norm.reshape(DEPTH, 1, GM_WIDTH),
        "w_s": ws_pair,
        "b_s": jnp.repeat(jnp.swapaxes(gm_b_s, 1, 2), GM_GDIM, axis=2),
        "w_gm_o": gm_w_o.astype(BF16),
        "b_gate": b_gate.reshape(DEPTH, 1, N_BRANCH * D),
        "w_out": w_out.astype(BF16),
    }


def _rope_tables():
    rows = SEQ // GRID_W
    r = jnp.repeat(jnp.arange(rows, dtype=F32), GRID_W)
    c = jnp.tile(jnp.arange(GRID_W, dtype=F32), rows)
    n_freq = ROPE // 4
    inv = jnp.power(ROPE_BASE, -jnp.arange(n_freq, dtype=F32) / n_freq)
    ang = jnp.concatenate([r[:, None] * inv, c[:, None] * inv], axis=-1)
    cos, sin = jnp.cos(ang), jnp.sin(ang)
    cos = jnp.concatenate([jnp.ones((CTX, ROPE // 2), F32), cos], axis=0)
    sin = jnp.concatenate([jnp.zeros((CTX, ROPE // 2), F32), sin], axis=0)
    ones = jnp.ones((T, NOPE), F32)
    zeros_n = jnp.zeros((T, NOPE), F32)
    zeros_p = jnp.zeros((T, HEAD_PAD - NOPE - ROPE), F32)
    cos_t = jnp.concatenate([ones, cos, cos, zeros_p], axis=1)
    sin_t = jnp.concatenate([zeros_n, sin, sin, zeros_p], axis=1)
    return cos_t, sin_t


def kernel(x, c, ctx, c_ctx, w_ada, b_ada, norm_g, ffn1_w_in, ffn1_w_out, ffn2_w_in, ffn2_w_out, w_in, mla_q_norm,
           mla_w_uq, mla_kv_norm, mla_w_ukv, mla_w_o, ssm_conv_w, ssm_conv_b, ssm_a_log, ssm_dt_bias, ssm_d,
           ssm_norm, ssm_w_o, gm_norm, gm_w_s, gm_b_s, gm_w_o, b_gate, w_out, final_norm):
    c_all = jnp.concatenate([c, c_ctx[None, :], jnp.zeros((MOD_ROWS - BATCH - 1, D), F32)], axis=0)
    w = _prep_weights(w_in, mla_q_norm, mla_w_uq, mla_kv_norm, mla_w_ukv, mla_w_o, ssm_conv_w, ssm_conv_b, ssm_a_log,
                      ssm_dt_bias, ssm_d, ssm_norm, ssm_w_o, gm_norm, gm_w_s, gm_b_s, gm_w_o, b_gate, w_out)
    w["mods"] = _ada(c_all, w_ada, b_ada).reshape(DEPTH, MOD_ROWS, N_MOD, D)
    w["norm_g"] = norm_g.reshape(DEPTH * 3, 1, D)
    w["ffn_w_in"] = (ffn1_w_in.astype(BF16), ffn2_w_in.astype(BF16))
    w["ffn_w_out"] = (ffn1_w_out.astype(BF16), ffn2_w_out.astype(BF16))
    w["final_norm"] = final_norm.reshape(1, D)
    w["cos"], w["sin"] = _rope_tables()

    xs = (ctx, x)
    for l in range(DEPTH):
        last = l == DEPTH - 1
        t0 = 1 if last else 0
        xs = _ffn(xs, w, l, 0)
        q, k, vt, xbc, dt, dtt, z, uv, gate = _proj(xs, w, l)
        a = _attn(q, k, vt, t0=t0)
        y = _ssd(xbc, dt, dtt, w, l)
        xs = _mixout(xs, a, y, z, uv, gate, w, l, t0=t0)
        xs = _ffn(xs, w, l, 1, latent_only=last, final=last)
    return xs
```

```python
import functools
import math

import jax
import jax.numpy as jnp
from jax import lax
from jax.experimental import pallas as pl
from jax.experimental.pallas import tpu as pltpu

F32 = jnp.float32
BF16 = jnp.bfloat16

D = 1024
BATCH = 8
SEQ = 2048
DEPTH = 2
CTX = 256
T = CTX + SEQ
GRID_W = 64
EPS = 1e-6

HEADS = 8
NOPE = 64
ROPE = 32
VDIM = 64
Q_RANK = 256
KV_RANK = 128
ROPE_BASE = 10000.0
ATTN_SCALE = (NOPE + ROPE) ** -0.5
EXP2_SCALE = ATTN_SCALE * math.log2(math.e)
HEAD_PAD = 128
V_ROWS = 80

SSM_HEADS = 8
SSM_P = 64
SSM_INNER = SSM_HEADS * SSM_P
SSM_GROUPS = 2
SSM_N = 128
SSM_CONV = 5
CHUNK = 128
XBC = SSM_INNER + 2 * SSM_GROUPS * SSM_N
N_CHUNKS = T // CHUNK
CTX_CHUNKS = CTX // CHUNK
HALO = 8

GM_GROUPS = 8
GM_WIDTH = 512
GM_GDIM = GM_WIDTH // GM_GROUPS

D_FF = 2816
N_BRANCH = 3
N_MOD = 9
MOD_ROWS = 16
CTX_MOD_ROW = BATCH

KV_SIDE = KV_RANK + ROPE + XBC + 2 * SSM_HEADS

C_KV = 0
C_KR = 128
C_KRR = 256
C_DT = 384
C_XBC = 512
C_Q = C_XBC + XBC
C_Z = C_Q + Q_RANK
C_UV = C_Z + SSM_INNER
C_GATE = C_UV + 2 * GM_WIDTH
C_END = C_GATE + N_BRANCH * D

TM = 256
TM_LATENT = 512
KEY_BLOCK = 128
CONV_ROWS = 64
PROJ_PIECE = 512
VMEM_LIMIT = 56 * 1024 * 1024


def _rms(x, g):
    y = x * lax.rsqrt(jnp.mean(x * x, axis=-1, keepdims=True) + EPS)
    return y * g


def _silu(x):
    return x * jax.nn.sigmoid(x)


def _bdot(a, b):
    return jnp.dot(a.astype(BF16), b.astype(BF16), preferred_element_type=F32)


def _split3(a):
    a1 = a.astype(BF16)
    r1 = a - a1.astype(F32)
    a2 = r1.astype(BF16)
    a3 = (r1 - a2.astype(F32)).astype(BF16)
    return a1, a2, a3


def _dot_right01(a, m01):
    return sum(jnp.dot(p, m01, preferred_element_type=F32) for p in _split3(a))


def _dot_left01(m01, a):
    return sum(jnp.dot(m01, p, preferred_element_type=F32) for p in _split3(a))


def _const_spec(shape, l=None):
    nd = len(shape)
    if l is None:
        return pl.BlockSpec(shape, lambda *_: (0,) * nd, pipeline_mode=pl.Buffered(1))
    return pl.BlockSpec((None,) + shape, lambda *_: (l,) + (0,) * nd, pipeline_mode=pl.Buffered(1))


def _mod_spec(l, t0):
    return pl.BlockSpec((None, 1, N_MOD, D), lambda b, t: (l, jnp.where(t + t0 == 0, CTX_MOD_ROW, b), 0, 0))


def _row_spec(width, t0):
    return pl.BlockSpec((1, TM, width), lambda b, t: (b, t + t0, 0))


def _params(sem):
    return pltpu.CompilerParams(dimension_semantics=sem, vmem_limit_bytes=VMEM_LIMIT)


def _ada_kernel(c_ref, w_ref, b_ref, o_ref):
    s = _silu(c_ref[...])
    o_ref[0] = _bdot(s, w_ref[0]) + b_ref[0]


def _ada(c_all, w_ada, b_ada):
    tn = 1152
    nn = N_MOD * D // tn
    return pl.pallas_call(
        _ada_kernel,
        grid=(DEPTH, nn),
        in_specs=[
            pl.BlockSpec((MOD_ROWS, D), lambda l, j: (0, 0)),
            pl.BlockSpec((1, D, tn), lambda l, j: (l, 0, j)),
            pl.BlockSpec((1, 1, tn), lambda l, j: (l, 0, j)),
        ],
        out_specs=pl.BlockSpec((1, MOD_ROWS, tn), lambda l, j: (l, 0, j)),
        out_shape=jax.ShapeDtypeStruct((DEPTH, MOD_ROWS, N_MOD * D), F32),
        compiler_params=_params(("arbitrary", "arbitrary")),
        name="ada_mod",
    )(c_all, w_ada, b_ada.reshape(DEPTH, 1, N_MOD * D))


def _ffn_kernel(*refs, mod0, final, split_input):
    if split_input:
        ctx_ref, x_ref, mod_ref, g_ref, wi_ref, wo_ref, fin_ref, o_ref = refs
        x = jnp.where(pl.program_id(1) == 0, ctx_ref[0], x_ref[0])
    else:
        x_ref, mod_ref, g_ref, wi_ref, wo_ref, fin_ref, o_ref = refs
        x = x_ref[0]
    m = mod_ref[0]
    shift, scale, gate = m[mod0:mod0 + 1], m[mod0 + 1:mod0 + 2], m[mod0 + 2:mod0 + 3]
    h = _rms(x, g_ref[...]) * (1.0 + scale) + shift
    gu = jnp.dot(h.astype(BF16), wi_ref[...], preferred_element_type=F32)
    a = _silu(gu[:, :D_FF]) * gu[:, D_FF:]
    y = jnp.dot(a.astype(BF16), wo_ref[...], preferred_element_type=F32)
    out = x + (0.5 * gate) * y
    if final:
        out = _rms(out, fin_ref[...])
    o_ref[0] = out


def _ffn(xs, w, l, which, *, latent_only=False, final=False):
    split_input = isinstance(xs, tuple)
    t0 = 1 if latent_only else 0
    tm = TM_LATENT if latent_only else TM
    nt = SEQ // tm if latent_only else T // TM
    if split_input:
        x_specs = [pl.BlockSpec((1, TM, D), lambda b, t: (b, 0, 0)),
                   pl.BlockSpec((1, TM, D), lambda b, t: (b, jnp.maximum(t - 1, 0), 0))]
        xs_args = list(xs)
    else:
        x_specs = [pl.BlockSpec((1, tm, D), lambda b, t: (b, t, 0))]
        xs_args = [xs]
    return pl.pallas_call(
        functools.partial(_ffn_kernel, mod0=6 * which, final=final, split_input=split_input),
        grid=(BATCH, nt),
        in_specs=x_specs + [
            _mod_spec(l, t0),
            _const_spec((1, D), 3 * l + 2 * which),
            _const_spec((D, 2 * D_FF), l),
            _const_spec((D_FF, D), l),
            _const_spec((1, D)),
        ],
        out_specs=pl.BlockSpec((1, tm, D), lambda b, t: (b, t, 0)),
        out_shape=jax.ShapeDtypeStruct((BATCH, nt * tm, D), F32),
        compiler_params=_params(("parallel", "parallel")),
        name="ffn",
    )(*xs_args, w["mods"], w["norm_g"], w["ffn_w_in"][which], w["ffn_w_out"][which], w["final_norm"])


def _proj_kernel(x_ref, xprev_ref, xnext_ref, mod_ref, g_ref, wkv_ref, wq_side_ref, cw_ref, cb_ref, kvn_ref, wkn_ref,
                 wvt_ref, one_ref, qn_ref, wq_ref, wqr_ref, cos_ref, sin_ref,
                 q_out, k_out, vt_out, xc_out, dt_out, dtt_out, z_out, uv_out, gate_out, *, ctx_queries):
    t = pl.program_id(1)
    m = mod_ref[0]
    xe = jnp.concatenate([xprev_ref[0], x_ref[0], xnext_ref[0]], axis=0)
    he = _rms(xe, g_ref[...]) * (1.0 + m[4:5]) + m[3:4]
    hb = he[HALO:HALO + TM].astype(BF16)
    he = he.astype(BF16)
    cos = cos_ref[...]
    sin = sin_ref[...]

    small = jnp.dot(hb, wkv_ref[:, C_KV:C_XBC], preferred_element_type=F32)

    xbc = jnp.dot(he, wkv_ref[:, C_XBC:C_Q], preferred_element_type=F32)
    row = lax.broadcasted_iota(jnp.int32, (TM + 2 * HALO, 1), 0)
    has_prev = t >= 2
    has_next = (t >= 1) & (t < T // TM - 1)
    keep = ((row >= HALO) | has_prev) & ((row < HALO + TM) | has_next)
    def conv_lane_tile(c0):
        for r0 in range(0, TM, CONV_ROWS):
            win = xbc[r0:r0 + CONV_ROWS + 2 * HALO, c0:c0 + 128]
            if r0 == 0 or r0 + CONV_ROWS == TM:
                win = jnp.where(keep[r0:r0 + CONV_ROWS + 2 * HALO], win, 0.0)
            acc = cb_ref[:, c0:c0 + 128]
            for k in range(SSM_CONV):
                shift = (SSM_CONV // 2 - k) % (CONV_ROWS + 2 * HALO)
                tap = win if shift == 0 else pltpu.roll(win, shift, axis=0)
                acc = acc + cw_ref[k:k + 1, c0:c0 + 128] * tap[HALO:HALO + CONV_ROWS]
            xc_out[0, r0:r0 + CONV_ROWS, c0:c0 + 128] = _silu(acc)

    def query_side():
        def proj(c0, c1):
            return jnp.dot(hb, wq_side_ref[:, c0 - C_Q:c1 - C_Q], preferred_element_type=F32)

        q_lat = proj(C_Q, C_Z)
        conv_tiles = list(range(0, XBC, 128))
        for out, base, width in ((z_out, C_Z, SSM_INNER), (uv_out, C_UV, 2 * GM_WIDTH), (gate_out, C_GATE, N_BRANCH * D)):
            for c0 in range(0, width, PROJ_PIECE):
                out[0, :, c0:c0 + PROJ_PIECE] = proj(base + c0, base + c0 + PROJ_PIECE).astype(out.dtype)
                if conv_tiles:
                    conv_lane_tile(conv_tiles.pop(0))
        assert not conv_tiles
        qn = _rms(q_lat, qn_ref[...]).astype(BF16)
        qa = jnp.dot(qn, wq_ref[...], preferred_element_type=F32)
        qb = jnp.dot(qn, wqr_ref[...], preferred_element_type=F32)
        for hd in range(HEADS):
            sl = slice(hd * HEAD_PAD, (hd + 1) * HEAD_PAD)
            q_out[0, :, sl] = (qa[:, sl] * cos + qb[:, sl] * sin).astype(BF16)

    def query_side_unused():
        for out in (q_out, z_out, uv_out, gate_out):
            out[...] = jnp.zeros_like(out)
        for c0 in range(0, XBC, 128):
            conv_lane_tile(c0)

    if ctx_queries:
        query_side()
    else:
        pl.when(t != 0)(query_side)
        pl.when(t == 0)(query_side_unused)

    dt = small[:, C_DT:C_DT + 128]
    dt_out[0] = dt
    dtt_out[0] = dt.T[:2 * SSM_HEADS]

    kvn = _rms(small[:, C_KV:C_KV + 128], kvn_ref[...]).astype(BF16)
    kn = jnp.dot(kvn, wkn_ref[...], preferred_element_type=F32)
    kr = small[:, C_KR:C_KR + 128] * cos + small[:, C_KRR:C_KRR + 128] * sin
    for hd in range(HEADS):
        sl = slice(hd * HEAD_PAD, (hd + 1) * HEAD_PAD)
        k_out[0, :, sl] = (kn[:, sl] + kr).astype(BF16)
    vt = lax.dot_general(wvt_ref[...], kvn, (((1,), (1,)), ((), ())), preferred_element_type=F32)
    vt_out[0] = (vt + one_ref[...]).astype(BF16)


def _proj(xs, w, l, *, ctx_queries):
    nt = T // TM

    def rows(w, dt):
        return jax.ShapeDtypeStruct((BATCH, T, w), dt), _row_spec(w, 0)

    def cols(r, dt):
        return jax.ShapeDtypeStruct((BATCH, r, T), dt), pl.BlockSpec((1, r, TM), lambda b, t: (b, 0, t))

    outs = [rows(HEADS * HEAD_PAD, BF16), rows(HEADS * HEAD_PAD, BF16), cols(HEADS * V_ROWS, BF16),
            rows(XBC, F32), rows(128, F32), cols(2 * SSM_HEADS, F32),
            rows(SSM_INNER, BF16), rows(2 * GM_WIDTH, BF16), rows(N_BRANCH * D, F32)]
    out_shape = [o[0] for o in outs]
    out_specs = [o[1] for o in outs]
    return pl.pallas_call(
        functools.partial(_proj_kernel, ctx_queries=ctx_queries),
        grid=(BATCH, nt),
        in_specs=[
            _row_spec(D, 0),
            pl.BlockSpec((1, HALO, D), lambda b, t: (b, jnp.maximum(t * (TM // HALO) - 1, 0), 0)),
            pl.BlockSpec((1, HALO, D), lambda b, t: (b, jnp.minimum((t + 1) * (TM // HALO), T // HALO - 1), 0)),
            _mod_spec(l, 0),
            _const_spec((1, D), 3 * l + 1),
            _const_spec((D, C_Q), l),
            _const_spec((D, C_END - C_Q), l),
            _const_spec((8, XBC), l),
            _const_spec((1, XBC), l),
            _const_spec((1, KV_RANK), l),
            _const_spec((KV_RANK, HEADS * HEAD_PAD), l),
            _const_spec((HEADS * V_ROWS, KV_RANK), l),
            _const_spec((HEADS * V_ROWS, 1)),
            _const_spec((1, Q_RANK), l),
            _const_spec((Q_RANK, HEADS * HEAD_PAD), l),
            _const_spec((Q_RANK, HEADS * HEAD_PAD), l),
            pl.BlockSpec((TM, HEAD_PAD), lambda b, t: (t, 0)),
            pl.BlockSpec((TM, HEAD_PAD), lambda b, t: (t, 0)),
        ],
        out_specs=out_specs,
        out_shape=out_shape,
        compiler_params=_params(("parallel", "parallel")),
        name="mixer_proj",
    )(xs, xs, xs, w["mods"], w["norm_g"], w["w_in_kv"], w["w_in_q"], w["conv_w"], w["conv_b"], w["kv_norm"], w["w_kn"],
      w["w_v_t"],
      w["v_ones"],
      w["q_norm"], w["w_q"], w["w_q_rot"], w["cos"], w["sin"])


def _attn_kernel(q_ref, k_ref, vt_ref, o_ref, *, t0):
    def run(nk):
        nblk = nk // KEY_BLOCK

        def score_block(hd, j):
            q = q_ref[0, :, hd * HEAD_PAD:(hd + 1) * HEAD_PAD]
            k = k_ref[0, j * KEY_BLOCK:(j + 1) * KEY_BLOCK, hd * HEAD_PAD:(hd + 1) * HEAD_PAD]
            return lax.dot_general(k, q, (((1,), (1,)), ((), ())), preferred_element_type=F32)

        def col_max(blocks):
            mx = functools.reduce(jnp.maximum, blocks)
            mx = jnp.max(mx.reshape(KEY_BLOCK // 8, 8, TM), axis=0)
            return jnp.max(mx, axis=0, keepdims=True)

        outs = []
        cur = [score_block(0, j) for j in range(nblk)]
        for hd in range(HEADS):
            mx = col_max(cur)
            nxt, ps = [], []
            for j in range(nblk):
                if hd + 1 < HEADS:
                    nxt.append(score_block(hd + 1, j))
                ps.append(jnp.exp2((cur[j] - mx) * EXP2_SCALE).astype(BF16))
            p = jnp.concatenate(ps, axis=0)
            ot = jnp.dot(vt_ref[0, hd * V_ROWS:(hd + 1) * V_ROWS, :nk], p, preferred_element_type=F32)
            outs.append(ot[:VDIM] / ot[VDIM:VDIM + 1])
            cur = nxt
        o_ref[0] = jnp.concatenate(outs, axis=0).T.astype(BF16)

    if t0 == 0:
        t = pl.program_id(1)
        pl.when(t == 0)(lambda: run(CTX))
        pl.when(t != 0)(lambda: run(T))
    else:
        run(T)


def _attn(q, k, v, *, t0):
    nt = T // TM - t0
    return pl.pallas_call(
        functools.partial(_attn_kernel, t0=t0),
        grid=(BATCH, nt),
        in_specs=[
            _row_spec(HEADS * HEAD_PAD, t0),
            pl.BlockSpec((1, T, HEADS * HEAD_PAD), lambda b, t: (b, 0, 0)),
            pl.BlockSpec((1, HEADS * V_ROWS, T), lambda b, t: (b, 0, 0)),
        ],
        out_specs=pl.BlockSpec((1, TM, HEADS * VDIM), lambda b, t: (b, t, 0)),
        out_shape=jax.ShapeDtypeStruct((BATCH, nt * TM, HEADS * VDIM), BF16),
        compiler_params=_params(("parallel", "arbitrary")),
        name="mla_attn",
    )(q, k, v)


def _ssd_kernel(xc_ref, dt_ref, dtt_ref, bias_row_ref, bias_col_ref, alog_row_ref, alog_col_ref, dfull_ref,
                y_ref, h_ref):
    row = lax.broadcasted_iota(jnp.int32, (CHUNK, CHUNK), 0)
    col = lax.broadcasted_iota(jnp.int32, (CHUNK, CHUNK), 1)
    lower = (col <= row)
    upper = (col >= row)
    lower01 = jnp.where(lower, 1.0, 0.0).astype(BF16)
    upper01 = jnp.where(upper, 1.0, 0.0).astype(BF16)
    lo_half = lax.broadcasted_iota(jnp.int32, (CHUNK, 128), 1) < SSM_P
    a_row = -jnp.exp(alog_row_ref[...])
    a_col = -jnp.exp(alog_col_ref[...])

    def chunk_step(c, d):
        r0 = pl.multiple_of(c * CHUNK, CHUNK)
        xc = xc_ref[0, pl.ds(r0, CHUNK), :]
        dcol = jax.nn.softplus(dt_ref[0, pl.ds(r0, CHUNK), :] + bias_row_ref[...])
        drow = jax.nn.softplus(dtt_ref[0, :, pl.ds(r0, CHUNK)] + bias_col_ref[...])
        tri_l, tri_r, mask = (lower01, upper01, lower) if d == 0 else (upper01, lower01, upper)
        acum_c = _dot_left01(tri_l, dcol * a_row)
        acum_r = _dot_right01(drow * a_col, tri_r)
        atot_c = acum_c[CHUNK - 1:CHUNK] if d == 0 else acum_c[0:1]
        w_c = jnp.exp(atot_c - acum_c) * dcol
        etot_c = jnp.exp(atot_c)

        def head_col(arr, hd):
            lane = d * SSM_HEADS + hd
            return arr[:, lane:lane + 1]

        def pair_lanes(arr, p):
            return jnp.where(lo_half[:arr.shape[0]], head_col(arr, 2 * p), head_col(arr, 2 * p + 1))

        ys = []
        for g in range(SSM_GROUPS):
            bm = xc[:, SSM_INNER + g * SSM_N:SSM_INNER + (g + 1) * SSM_N]
            cm = xc[:, SSM_INNER + (SSM_GROUPS + g) * SSM_N:SSM_INNER + (SSM_GROUPS + g + 1) * SSM_N]
            bmb = bm.astype(BF16)
            cmb = cm.astype(BF16)
            cb = lax.dot_general(cmb, bmb, (((1,), (1,)), ((), ())), preferred_element_type=F32)
            bt = bm.T.astype(BF16)
            for pp in range(2):
                p = g * 2 + pp
                sc, a_bc = [], []
                for hh in range(2):
                    hd = 2 * p + hh
                    a_bc.append(jnp.broadcast_to(head_col(acum_c, hd), (CHUNK, CHUNK)))
                    diff = a_bc[hh] - acum_r[d * SSM_HEADS + hd:d * SSM_HEADS + hd + 1]
                    dec = jnp.exp(jnp.where(mask, diff, -jnp.inf))
                    sc.append(cb * dec * drow[d * SSM_HEADS + hd:d * SSM_HEADS + hd + 1])
                x_p = xc[:, p * 128:(p + 1) * 128]
                xw = (x_p * pair_lanes(w_c, p)).astype(BF16)
                s_t = jnp.dot(bt, xw, preferred_element_type=F32)
                h_prev = h_ref[d, p]
                e_p = jnp.exp(jnp.where(lo_half, a_bc[0], a_bc[1]))
                y_inter = jnp.dot(cmb, h_prev.astype(BF16), preferred_element_type=F32) * e_p
                pmat = jnp.concatenate(sc, axis=1).astype(BF16)
                xbd = jnp.concatenate([jnp.where(lo_half, x_p, 0.0), jnp.where(lo_half, 0.0, x_p)],
                                      axis=0).astype(BF16)
                ys.append(jnp.dot(pmat, xbd, preferred_element_type=F32) + y_inter)
                h_ref[d, p] = h_prev * pair_lanes(etot_c, p) + s_t
        y = jnp.concatenate(ys, axis=1)
        if d == 0:
            y = y + dfull_ref[...] * xc[:, :SSM_INNER]
        y_ref[0, pl.ds(r0, CHUNK), :] += y

    h_ref[...] = jnp.zeros_like(h_ref)
    y_ref[...] = jnp.zeros_like(y_ref)

    def scan_body(i, carry):
        chunk_step(i, 0)
        chunk_step(jnp.where(i < CTX_CHUNKS, CTX_CHUNKS - 1 - i, N_CHUNKS + CTX_CHUNKS - 1 - i), 1)
        return carry

    lax.fori_loop(0, N_CHUNKS, scan_body, 0, unroll=2)


def _ssd(xc, dt, dtt, w, l):
    return pl.pallas_call(
        _ssd_kernel,
        grid=(BATCH,),
        in_specs=[
            pl.BlockSpec((1, T, XBC), lambda b: (b, 0, 0)),
            pl.BlockSpec((1, T, 128), lambda b: (b, 0, 0)),
            pl.BlockSpec((1, 2 * SSM_HEADS, T), lambda b: (b, 0, 0)),
            _const_spec((1, 128), l),
            _const_spec((2 * SSM_HEADS, 1), l),
            _const_spec((1, 128), l),
            _const_spec((2 * SSM_HEADS, 1), l),
            _const_spec((1, SSM_INNER), l),
        ],
        out_specs=pl.BlockSpec((1, T, SSM_INNER), lambda b: (b, 0, 0)),
        out_shape=jax.ShapeDtypeStruct((BATCH, T, SSM_INNER), F32),
        scratch_shapes=[
            pltpu.VMEM((2, SSM_HEADS // 2, SSM_N, 2 * SSM_P), F32),
        ],
        compiler_params=_params(("arbitrary",)),
        name="ssd_bidir",
    )(xc, dt, dtt, w["dt_bias_row"], w["dt_bias_col"], w["a_log_row"], w["a_log_col"], w["d_full"])


def _mixout_kernel(x_ref, mod_ref, a_ref, y_ref, z_ref, uv_ref, gate_ref,
                   wmo_ref, sn_ref, wso_ref, gn_ref, ws_ref, bs_ref, wgo_ref, bg_ref, wout_ref, o_ref):
    x = x_ref[0]
    m = mod_ref[0]
    o_mla = jnp.dot(a_ref[0], wmo_ref[...], preferred_element_type=F32)

    ys = y_ref[0] * _silu(z_ref[0].astype(F32))
    o_ssm = _bdot(_rms(ys, sn_ref[...]), wso_ref[...])

    uv = jax.nn.gelu(uv_ref[0].astype(F32))
    u = uv[:, :GM_WIDTH]
    v = uv[:, GM_WIDTH:]
    v = v - jnp.mean(v, axis=-1, keepdims=True)
    v = v * lax.rsqrt(jnp.mean(v * v, axis=-1, keepdims=True) + EPS) * gn_ref[...]
    lo_half = lax.broadcasted_iota(jnp.int32, (CHUNK, 128), 1) < GM_GDIM
    mixed = []
    for ch in range(TM // CHUNK):
        parts = []
        for p in range(GM_GROUPS // 2):
            vp = v[ch * CHUNK:(ch + 1) * CHUNK, p * 128:(p + 1) * 128]
            vbd = jnp.concatenate([jnp.where(lo_half, vp, 0.0), jnp.where(lo_half, 0.0, vp)], axis=0)
            parts.append(jnp.dot(ws_ref[p], vbd.astype(BF16), preferred_element_type=F32))
        mixed.append(jnp.concatenate(parts, axis=1) + bs_ref[...])
    mixed = jnp.concatenate(mixed, axis=0)
    o_gm = _bdot(u * mixed, wgo_ref[...])

    gts = jax.nn.sigmoid(gate_ref[0].astype(F32) + bg_ref[...])
    merged = gts[:, :D] * o_mla + gts[:, D:2 * D] * o_ssm + gts[:, 2 * D:] * o_gm
    y = _bdot(merged, wout_ref[...])
    o_ref[0] = x + m[5:6] * y


def _mixout(xs, a, y, z, uv, gate, w, l, *, t0):
    nt = T // TM - t0
    return pl.pallas_call(
        _mixout_kernel,
        grid=(BATCH, nt),
        in_specs=[
            _row_spec(D, t0),
            _mod_spec(l, t0),
            pl.BlockSpec((1, TM, HEADS * VDIM), lambda b, t: (b, t, 0)),
            _row_spec(SSM_INNER, t0),
            _row_spec(SSM_INNER, t0),
            _row_spec(2 * GM_WIDTH, t0),
            _row_spec(N_BRANCH * D, t0),
            _const_spec((HEADS * VDIM, D), l),
            _const_spec((1, SSM_INNER), l),
            _const_spec((SSM_INNER, D), l),
            _const_spec((1, GM_WIDTH), l),
            _const_spec((GM_GROUPS // 2, CHUNK, 2 * CHUNK), l),
            _const_spec((CHUNK, GM_WIDTH), l),
            _const_spec((GM_WIDTH, D), l),
            _const_spec((1, N_BRANCH * D), l),
            _const_spec((D, D), l),
        ],
        out_specs=pl.BlockSpec((1, TM, D), lambda b, t: (b, t, 0)),
        out_shape=jax.ShapeDtypeStruct((BATCH, nt * TM, D), F32),
        compiler_params=_params(("parallel", "parallel")),
        name="mixer_out",
    )(xs, w["mods"], a, y, z, uv, gate, w["w_mla_o"], w["ssm_norm"], w["w_ssm_o"], w["gm_norm"], w["w_s"],
      w["b_s"], w["w_gm_o"], w["b_gate"], w["w_out"])


def _rot_half(w):
    half = ROPE // 2
    return jnp.concatenate([-w[..., half:], w[..., :half]], axis=-1)


def _head_pad(nope, rope):
    pad = jnp.zeros(nope.shape[:-1] + (HEAD_PAD - NOPE - ROPE,), nope.dtype)
    out = jnp.concatenate([nope, rope, pad], axis=-1)
    return out.reshape(out.shape[:-2] + (HEADS * HEAD_PAD,))


def _prep_weights(w_in, mla_q_norm, mla_w_uq, mla_kv_norm, mla_w_ukv, mla_w_o, ssm_conv_w, ssm_conv_b,
                  ssm_a_log, ssm_dt_bias, ssm_d, ssm_norm, ssm_w_o, gm_norm, gm_w_s, gm_b_s, gm_w_o, b_gate, w_out):
    w = w_in
    o = 0
    kv = w[..., o:o + KV_RANK]; o += KV_RANK
    kr = w[..., o:o + ROPE]; o += ROPE
    xbc = w[..., o:o + XBC]; o += XBC
    dtw = w[..., o:o + 2 * SSM_HEADS]; o += 2 * SSM_HEADS
    rest = w[..., o:]

    def lanes(cols, start):
        return jnp.pad(cols, ((0, 0), (0, 0), (start, 128 - start - cols.shape[-1])))

    pieces = [kv, lanes(kr, NOPE), lanes(_rot_half(kr), NOPE), lanes(dtw, 0), xbc]
    w_kv_side = jnp.concatenate([p.astype(BF16) for p in pieces], axis=-1)

    uq = mla_w_uq.astype(BF16).reshape(DEPTH, Q_RANK, HEADS, NOPE + ROPE)
    uq_n, uq_r = uq[..., :NOPE], uq[..., NOPE:]
    ukv = mla_w_ukv.astype(BF16).reshape(DEPTH, KV_RANK, HEADS, NOPE + VDIM)
    kn = _head_pad(ukv[..., :NOPE], jnp.zeros((DEPTH, KV_RANK, HEADS, ROPE), BF16))
    v_t = jnp.pad(jnp.transpose(ukv[..., NOPE:], (0, 2, 3, 1)), ((0, 0), (0, 0), (0, V_ROWS - VDIM), (0, 0)))
    v_ones = jnp.tile((jnp.arange(V_ROWS) == VDIM).astype(F32), HEADS).reshape(HEADS * V_ROWS, 1)

    ws = gm_w_s.astype(BF16)
    ws_pair = jnp.concatenate([ws[:, 0::2], ws[:, 1::2]], axis=3)

    def row128(vals, fill=0.0):
        vals = vals.reshape(DEPTH, 1, -1)
        return jnp.pad(vals, ((0, 0), (0, 0), (0, 128 - vals.shape[-1])), constant_values=fill)

    return {
        "w_in_kv": w_kv_side,
        "w_in_q": rest.astype(BF16),
        "kv_norm": mla_kv_norm.reshape(DEPTH, 1, KV_RANK),
        "w_kn": kn,
        "w_v_t": v_t.reshape(DEPTH, HEADS * V_ROWS, KV_RANK),
        "v_ones": v_ones,
        "q_norm": mla_q_norm.reshape(DEPTH, 1, Q_RANK),
        "w_q": _head_pad(uq_n, uq_r),
        "w_q_rot": _head_pad(jnp.zeros_like(uq_n), _rot_half(uq_r)),
        "conv_w": jnp.pad(jnp.swapaxes(ssm_conv_w, 1, 2), ((0, 0), (0, 8 - SSM_CONV), (0, 0))),
        "conv_b": ssm_conv_b.reshape(DEPTH, 1, XBC),
        "dt_bias_row": row128(ssm_dt_bias),
        "dt_bias_col": ssm_dt_bias.reshape(DEPTH, 2 * SSM_HEADS, 1),
        "a_log_row": row128(ssm_a_log, fill=-80.0),
        "a_log_col": ssm_a_log.reshape(DEPTH, 2 * SSM_HEADS, 1),
        "d_full": jnp.repeat(ssm_d, SSM_P, axis=1).reshape(DEPTH, 1, SSM_INNER),
        "w_mla_o": mla_w_o.astype(BF16),
        "ssm_norm": ssm_norm.reshape(DEPTH, 1, SSM_INNER),
        "w_ssm_o": ssm_w_o.astype(BF16),
        "gm_norm": gm_norm.reshape(DEPTH, 1, GM_WIDTH),
        "w_s": ws_pair,
        "b_s": jnp.repeat(jnp.swapaxes(gm_b_s, 1, 2), GM_GDIM, axis=2),
        "w_gm_o": gm_w_o.astype(BF16),
        "b_gate": b_gate.reshape(DEPTH, 1, N_BRANCH * D),
        "w_out": w_out.astype(BF16),
    }


def _rope_tables():
    rows = SEQ // GRID_W
    r = jnp.repeat(jnp.arange(rows, dtype=F32), GRID_W)
    c = jnp.tile(jnp.arange(GRID_W, dtype=F32), rows)
    n_freq = ROPE // 4
    inv = jnp.power(ROPE_BASE, -jnp.arange(n_freq, dtype=F32) / n_freq)
    ang = jnp.concatenate([r[:, None] * inv, c[:, None] * inv], axis=-1)
    cos, sin = jnp.cos(ang), jnp.sin(ang)
    cos = jnp.concatenate([jnp.ones((CTX, ROPE // 2), F32), cos], axis=0)
    sin = jnp.concatenate([jnp.zeros((CTX, ROPE // 2), F32), sin], axis=0)
    ones = jnp.ones((T, NOPE), F32)
    zeros_n = jnp.zeros((T, NOPE), F32)
    zeros_p = jnp.zeros((T, HEAD_PAD - NOPE - ROPE), F32)
    cos_t = jnp.concatenate([ones, cos, cos, zeros_p], axis=1)
    sin_t = jnp.concatenate([zeros_n, sin, sin, zeros_p], axis=1)
    return cos_t, sin_t


def kernel(x, c, ctx, c_ctx, w_ada, b_ada, norm_g, ffn1_w_in, ffn1_w_out, ffn2_w_in, ffn2_w_out, w_in, mla_q_norm,
           mla_w_uq, mla_kv_norm, mla_w_ukv, mla_w_o, ssm_conv_w, ssm_conv_b, ssm_a_log, ssm_dt_bias, ssm_d,
           ssm_norm, ssm_w_o, gm_norm, gm_w_s, gm_b_s, gm_w_o, b_gate, w_out, final_norm):
    c_all = jnp.concatenate([c, c_ctx[None, :], jnp.zeros((MOD_ROWS - BATCH - 1, D), F32)], axis=0)
    w = _prep_weights(w_in, mla_q_norm, mla_w_uq, mla_kv_norm, mla_w_ukv, mla_w_o, ssm_conv_w, ssm_conv_b, ssm_a_log,
                      ssm_dt_bias, ssm_d, ssm_norm, ssm_w_o, gm_norm, gm_w_s, gm_b_s, gm_w_o, b_gate, w_out)
    w["mods"] = _ada(c_all, w_ada, b_ada).reshape(DEPTH, MOD_ROWS, N_MOD, D)
    w["norm_g"] = norm_g.reshape(DEPTH * 3, 1, D)
    w["ffn_w_in"] = (ffn1_w_in.astype(BF16), ffn2_w_in.astype(BF16))
    w["ffn_w_out"] = (ffn1_w_out.astype(BF16), ffn2_w_out.astype(BF16))
    w["final_norm"] = final_norm.reshape(1, D)
    w["cos"], w["sin"] = _rope_tables()

    xs = (ctx, x)
    for l in range(DEPTH):
        last = l == DEPTH - 1
        t0 = 1 if last else 0
        xs = _ffn(xs, w, l, 0)
        q, k, vt, xbc, dt, dtt, z, uv, gate = _proj(xs, w, l, ctx_queries=not last)
        a = _attn(q, k, vt, t0=t0)
        y = _ssd(xbc, dt, dtt, w, l)
        xs = _mixout(xs, a, y, z, uv, gate, w, l, t0=t0)
        xs = _ffn(xs, w, l, 1, latent_only=last, final=last)
    return xs
```

```python
import functools
import math

import jax
import jax.numpy as jnp
from jax import lax
from jax.experimental import pallas as pl
from jax.experimental.pallas import tpu as pltpu

F32 = jnp.float32
BF16 = jnp.bfloat16

D = 1024
BATCH = 8
SEQ = 2048
DEPTH = 2
CTX = 256
T = CTX + SEQ
GRID_W = 64
EPS = 1e-6

HEADS = 8
NOPE = 64
ROPE = 32
VDIM = 64
Q_RANK = 256
KV_RANK = 128
ROPE_BASE = 10000.0
ATTN_SCALE = (NOPE + ROPE) ** -0.5
EXP2_SCALE = ATTN_SCALE * math.log2(math.e)
HEAD_PAD = 128
V_ROWS = 80

SSM_HEADS = 8
SSM_P = 64
SSM_INNER = SSM_HEADS * SSM_P
SSM_GROUPS = 2
SSM_N = 128
SSM_CONV = 5
CHUNK = 128
XBC = SSM_INNER + 2 * SSM_GROUPS * SSM_N
N_CHUNKS = T // CHUNK
CTX_CHUNKS = CTX // CHUNK
HALO = 8

GM_GROUPS = 8
GM_WIDTH = 512
GM_GDIM = GM_WIDTH // GM_GROUPS

D_FF = 2816
WI_STAGE_ROWS = 64
WO_STAGE_ROWS = 352
N_BRANCH = 3
N_MOD = 9
MOD_ROWS = 16
CTX_MOD_ROW = BATCH

C_KV = 0
C_KR = 128
C_KRR = 256
C_DT = 384
C_XBC = 512
C_Q = C_XBC + XBC
C_Z = C_Q + Q_RANK
C_UV = C_Z + SSM_INNER
C_GATE = C_UV + 2 * GM_WIDTH
C_END = C_GATE + N_BRANCH * D

TM = 256
TM_LATENT = 512
KEY_BLOCK = 128
CONV_ROWS = 64
PROJ_PIECE = 512
VMEM_LIMIT = 56 * 1024 * 1024


def _rms(x, g):
    y = x * lax.rsqrt(jnp.mean(x * x, axis=-1, keepdims=True) + EPS)
    return y * g


def _silu(x):
    return x * jax.nn.sigmoid(x)


def _bdot(a, b):
    return jnp.dot(a.astype(BF16), b.astype(BF16), preferred_element_type=F32)


def _split3(a):
    a1 = a.astype(BF16)
    r1 = a - a1.astype(F32)
    a2 = r1.astype(BF16)
    a3 = (r1 - a2.astype(F32)).astype(BF16)
    return a1, a2, a3


def _dot_right01(a, m01):
    return sum(jnp.dot(p, m01, preferred_element_type=F32) for p in _split3(a))


def _dot_left01(m01, a):
    return sum(jnp.dot(m01, p, preferred_element_type=F32) for p in _split3(a))


def _const_spec(shape, l=None):
    nd = len(shape)
    if l is None:
        return pl.BlockSpec(shape, lambda *_: (0,) * nd, pipeline_mode=pl.Buffered(1))
    return pl.BlockSpec((None,) + shape, lambda *_: (l,) + (0,) * nd, pipeline_mode=pl.Buffered(1))


def _mod_spec(l, t0):
    return pl.BlockSpec((None, 1, N_MOD, D), lambda b, t: (l, jnp.where(t + t0 == 0, CTX_MOD_ROW, b), 0, 0))


def _row_spec(width, t0):
    return pl.BlockSpec((1, TM, width), lambda b, t: (b, t + t0, 0))


def _params(sem):
    return pltpu.CompilerParams(dimension_semantics=sem, vmem_limit_bytes=VMEM_LIMIT)


def _ada_kernel(c_ref, w_ref, b_ref, o_ref):
    s = _silu(c_ref[...])
    o_ref[0] = _bdot(s, w_ref[0]) + b_ref[0]


def _ada(c_all, w_ada, b_ada):
    tn = 1152
    nn = N_MOD * D // tn
    return pl.pallas_call(
        _ada_kernel,
        grid=(DEPTH, nn),
        in_specs=[
            pl.BlockSpec((MOD_ROWS, D), lambda l, j: (0, 0)),
            pl.BlockSpec((1, D, tn), lambda l, j: (l, 0, j)),
            pl.BlockSpec((1, 1, tn), lambda l, j: (l, 0, j)),
        ],
        out_specs=pl.BlockSpec((1, MOD_ROWS, tn), lambda l, j: (l, 0, j)),
        out_shape=jax.ShapeDtypeStruct((DEPTH, MOD_ROWS, N_MOD * D), F32),
        compiler_params=_params(("arbitrary", "arbitrary")),
        name="ada_mod",
    )(c_all, w_ada, b_ada.reshape(DEPTH, 1, N_MOD * D))


def _stage_bf16(src_hbm, dst_ref, stage_ref, sem, rows):
    n = src_hbm.shape[0] // rows

    def copy(i):
        return pltpu.make_async_copy(src_hbm.at[pl.ds(i * rows, rows), :], stage_ref.at[i % 2], sem.at[i % 2])

    copy(0).start()
    for i in range(n):
        if i + 1 < n:
            copy(i + 1).start()
        copy(i).wait()
        dst_ref[i * rows:(i + 1) * rows, :] = stage_ref[i % 2].astype(BF16)


def _ffn_kernel(*refs, layer, mod0, final, split_input):
    if split_input:
        ctx_ref, x_ref, mod_ref, g_ref, wi_hbm, wo_hbm, fin_ref, o_ref, wi_ref, wo_ref, si_ref, so_ref, sem = refs
        x = jnp.where(pl.program_id(1) == 0, ctx_ref[0], x_ref[0])
    else:
        x_ref, mod_ref, g_ref, wi_hbm, wo_hbm, fin_ref, o_ref, wi_ref, wo_ref, si_ref, so_ref, sem = refs
        x = x_ref[0]

    @pl.when((pl.program_id(0) == 0) & (pl.program_id(1) == 0))
    def _():
        _stage_bf16(wi_hbm.at[layer], wi_ref, si_ref, sem.at[0], WI_STAGE_ROWS)
        _stage_bf16(wo_hbm.at[layer], wo_ref, so_ref, sem.at[1], WO_STAGE_ROWS)

    m = mod_ref[0]
    shift, scale, gate = m[mod0:mod0 + 1], m[mod0 + 1:mod0 + 2], m[mod0 + 2:mod0 + 3]
    h = _rms(x, g_ref[...]) * (1.0 + scale) + shift
    gu = jnp.dot(h.astype(BF16), wi_ref[...], preferred_element_type=F32)
    a = _silu(gu[:, :D_FF]) * gu[:, D_FF:]
    y = jnp.dot(a.astype(BF16), wo_ref[...], preferred_element_type=F32)
    out = x + (0.5 * gate) * y
    if final:
        out = _rms(out, fin_ref[...])
    o_ref[0] = out


def _ffn(xs, w, l, which, *, latent_only=False, final=False):
    split_input = isinstance(xs, tuple)
    t0 = 1 if latent_only else 0
    tm = TM_LATENT if latent_only else TM
    nt = SEQ // tm if latent_only else T // TM
    if split_input:
        x_specs = [pl.BlockSpec((1, TM, D), lambda b, t: (b, 0, 0)),
                   pl.BlockSpec((1, TM, D), lambda b, t: (b, jnp.maximum(t - 1, 0), 0))]
        xs_args = list(xs)
    else:
        x_specs = [pl.BlockSpec((1, tm, D), lambda b, t: (b, t, 0))]
        xs_args = [xs]
    return pl.pallas_call(
        functools.partial(_ffn_kernel, layer=l, mod0=6 * which, final=final, split_input=split_input),
        grid=(BATCH, nt),
        in_specs=x_specs + [
            _mod_spec(l, t0),
            _const_spec((1, D), 3 * l + 2 * which),
            pl.BlockSpec(memory_space=pl.ANY),
            pl.BlockSpec(memory_space=pl.ANY),
            _const_spec((1, D)),
        ],
        out_specs=pl.BlockSpec((1, tm, D), lambda b, t: (b, t, 0)),
        out_shape=jax.ShapeDtypeStruct((BATCH, nt * tm, D), F32),
        scratch_shapes=[
            pltpu.VMEM((D, 2 * D_FF), BF16),
            pltpu.VMEM((D_FF, D), BF16),
            pltpu.VMEM((2, WI_STAGE_ROWS, 2 * D_FF), F32),
            pltpu.VMEM((2, WO_STAGE_ROWS, D), F32),
            pltpu.SemaphoreType.DMA((2, 2)),
        ],
        compiler_params=_params(("arbitrary", "arbitrary")),
        name="ffn",
    )(*xs_args, w["mods"], w["norm_g"], w["ffn_w_in"][which], w["ffn_w_out"][which], w["final_norm"])


def _proj_kernel(x_ref, xprev_ref, xnext_ref, mod_ref, g_ref, wkv_ref, wq_side_ref, cw_ref, cb_ref, kvn_ref, wkn_ref,
                 wvt_ref, one_ref, qn_ref, wq_ref, wqr_ref, cos_ref, sin_ref,
                 q_out, k_out, vt_out, xc_out, dt_out, dtt_out, z_out, uv_out, gate_out, *, ctx_queries):
    t = pl.program_id(1)
    m = mod_ref[0]
    xe = jnp.concatenate([xprev_ref[0], x_ref[0], xnext_ref[0]], axis=0)
    he = _rms(xe, g_ref[...]) * (1.0 + m[4:5]) + m[3:4]
    hb = he[HALO:HALO + TM].astype(BF16)
    he = he.astype(BF16)
    cos = cos_ref[...]
    sin = sin_ref[...]

    small = jnp.dot(hb, wkv_ref[:, C_KV:C_XBC], preferred_element_type=F32)

    xbc = jnp.dot(he, wkv_ref[:, C_XBC:C_Q], preferred_element_type=F32)
    row = lax.broadcasted_iota(jnp.int32, (TM + 2 * HALO, 1), 0)
    has_prev = t >= 2
    has_next = (t >= 1) & (t < T // TM - 1)
    keep = ((row >= HALO) | has_prev) & ((row < HALO + TM) | has_next)
    def conv_lane_tile(c0):
        for r0 in range(0, TM, CONV_ROWS):
            win = xbc[r0:r0 + CONV_ROWS + 2 * HALO, c0:c0 + 128]
            if r0 == 0 or r0 + CONV_ROWS == TM:
                win = jnp.where(keep[r0:r0 + CONV_ROWS + 2 * HALO], win, 0.0)
            acc = cb_ref[:, c0:c0 + 128]
            for k in range(SSM_CONV):
                shift = (SSM_CONV // 2 - k) % (CONV_ROWS + 2 * HALO)
                tap = win if shift == 0 else pltpu.roll(win, shift, axis=0)
                acc = acc + cw_ref[k:k + 1, c0:c0 + 128] * tap[HALO:HALO + CONV_ROWS]
            xc_out[0, r0:r0 + CONV_ROWS, c0:c0 + 128] = _silu(acc)

    def query_side():
        def proj(c0, c1):
            return jnp.dot(hb, wq_side_ref[:, c0 - C_Q:c1 - C_Q], preferred_element_type=F32)

        q_lat = proj(C_Q, C_Z)
        conv_tiles = list(range(0, XBC, 128))
        for out, base, width in ((z_out, C_Z, SSM_INNER), (uv_out, C_UV, 2 * GM_WIDTH), (gate_out, C_GATE, N_BRANCH * D)):
            for c0 in range(0, width, PROJ_PIECE):
                out[0, :, c0:c0 + PROJ_PIECE] = proj(base + c0, base + c0 + PROJ_PIECE).astype(out.dtype)
                if conv_tiles:
                    conv_lane_tile(conv_tiles.pop(0))
        assert not conv_tiles
        qn = _rms(q_lat, qn_ref[...]).astype(BF16)
        qa = jnp.dot(qn, wq_ref[...], preferred_element_type=F32)
        qb = jnp.dot(qn, wqr_ref[...], preferred_element_type=F32)
        for hd in range(HEADS):
            sl = slice(hd * HEAD_PAD, (hd + 1) * HEAD_PAD)
            q_out[0, :, sl] = (qa[:, sl] * cos + qb[:, sl] * sin).astype(BF16)

    def query_side_unused():
        for out in (q_out, z_out, uv_out, gate_out):
            out[...] = jnp.zeros_like(out)
        for c0 in range(0, XBC, 128):
            conv_lane_tile(c0)

    if ctx_queries:
        query_side()
    else:
        pl.when(t != 0)(query_side)
        pl.when(t == 0)(query_side_unused)

    dt = small[:, C_DT:C_DT + 128]
    dt_out[0] = dt
    dtt_out[0] = dt.T[:2 * SSM_HEADS]

    kvn = _rms(small[:, C_KV:C_KV + 128], kvn_ref[...]).astype(BF16)
    kn = jnp.dot(kvn, wkn_ref[...], preferred_element_type=F32)
    kr = small[:, C_KR:C_KR + 128] * cos + small[:, C_KRR:C_KRR + 128] * sin
    for hd in range(HEADS):
        sl = slice(hd * HEAD_PAD, (hd + 1) * HEAD_PAD)
        k_out[0, :, sl] = (kn[:, sl] + kr).astype(BF16)
    vt = lax.dot_general(wvt_ref[...], kvn, (((1,), (1,)), ((), ())), preferred_element_type=F32)
    vt_out[0] = (vt + one_ref[...]).astype(BF16)


def _proj(xs, w, l, *, ctx_queries):
    nt = T // TM

    def rows(w, dt):
        return jax.ShapeDtypeStruct((BATCH, T, w), dt), _row_spec(w, 0)

    def cols(r, dt):
        return jax.ShapeDtypeStruct((BATCH, r, T), dt), pl.BlockSpec((1, r, TM), lambda b, t: (b, 0, t))

    outs = [rows(HEADS * HEAD_PAD, BF16), rows(HEADS * HEAD_PAD, BF16), cols(HEADS * V_ROWS, BF16),
            rows(XBC, F32), rows(128, F32), cols(2 * SSM_HEADS, F32),
            rows(SSM_INNER, BF16), rows(2 * GM_WIDTH, BF16), rows(N_BRANCH * D, F32)]
    out_shape = [o[0] for o in outs]
    out_specs = [o[1] for o in outs]
    return pl.pallas_call(
        functools.partial(_proj_kernel, ctx_queries=ctx_queries),
        grid=(BATCH, nt),
        in_specs=[
            _row_spec(D, 0),
            pl.BlockSpec((1, HALO, D), lambda b, t: (b, jnp.maximum(t * (TM // HALO) - 1, 0), 0)),
            pl.BlockSpec((1, HALO, D), lambda b, t: (b, jnp.minimum((t + 1) * (TM // HALO), T // HALO - 1), 0)),
            _mod_spec(l, 0),
            _const_spec((1, D), 3 * l + 1),
            _const_spec((D, C_Q), l),
            _const_spec((D, C_END - C_Q), l),
            _const_spec((8, XBC), l),
            _const_spec((1, XBC), l),
            _const_spec((1, KV_RANK), l),
            _const_spec((KV_RANK, HEADS * HEAD_PAD), l),
            _const_spec((HEADS * V_ROWS, KV_RANK), l),
            _const_spec((HEADS * V_ROWS, 1)),
            _const_spec((1, Q_RANK), l),
            _const_spec((Q_RANK, HEADS * HEAD_PAD), l),
            _const_spec((Q_RANK, HEADS * HEAD_PAD), l),
            pl.BlockSpec((TM, HEAD_PAD), lambda b, t: (t, 0)),
            pl.BlockSpec((TM, HEAD_PAD), lambda b, t: (t, 0)),
        ],
        out_specs=out_specs,
        out_shape=out_shape,
        compiler_params=_params(("parallel", "parallel")),
        name="mixer_proj",
    )(xs, xs, xs, w["mods"], w["norm_g"], w["w_in_kv"], w["w_in_q"], w["conv_w"], w["conv_b"], w["kv_norm"], w["w_kn"],
      w["w_v_t"],
      w["v_ones"],
      w["q_norm"], w["w_q"], w["w_q_rot"], w["cos"], w["sin"])


def _attn_kernel(q_ref, k_ref, vt_ref, o_ref, *, t0):
    def run(nk):
        nblk = nk // KEY_BLOCK

        def score_block(hd, j):
            q = q_ref[0, :, hd * HEAD_PAD:(hd + 1) * HEAD_PAD]
            k = k_ref[0, j * KEY_BLOCK:(j + 1) * KEY_BLOCK, hd * HEAD_PAD:(hd + 1) * HEAD_PAD]
            return lax.dot_general(k, q, (((1,), (1,)), ((), ())), preferred_element_type=F32)

        def col_max(blocks):
            mx = functools.reduce(jnp.maximum, blocks)
            mx = jnp.max(mx.reshape(KEY_BLOCK // 8, 8, TM), axis=0)
            return jnp.max(mx, axis=0, keepdims=True)

        outs = []
        cur = [score_block(0, j) for j in range(nblk)]
        for hd in range(HEADS):
            mx = col_max(cur)
            nxt, ps = [], []
            for j in range(nblk):
                if hd + 1 < HEADS:
                    nxt.append(score_block(hd + 1, j))
                ps.append(jnp.exp2((cur[j] - mx) * EXP2_SCALE).astype(BF16))
            p = jnp.concatenate(ps, axis=0)
            ot = jnp.dot(vt_ref[0, hd * V_ROWS:(hd + 1) * V_ROWS, :nk], p, preferred_element_type=F32)
            outs.append(ot[:VDIM] / ot[VDIM:VDIM + 1])
            cur = nxt
        o_ref[0] = jnp.concatenate(outs, axis=0).T.astype(BF16)

    if t0 == 0:
        t = pl.program_id(1)
        pl.when(t == 0)(lambda: run(CTX))
        pl.when(t != 0)(lambda: run(T))
    else:
        run(T)


def _attn(q, k, v, *, t0):
    nt = T // TM - t0
    return pl.pallas_call(
        functools.partial(_attn_kernel, t0=t0),
        grid=(BATCH, nt),
        in_specs=[
            _row_spec(HEADS * HEAD_PAD, t0),
            pl.BlockSpec((1, T, HEADS * HEAD_PAD), lambda b, t: (b, 0, 0)),
            pl.BlockSpec((1, HEADS * V_ROWS, T), lambda b, t: (b, 0, 0)),
        ],
        out_specs=pl.BlockSpec((1, TM, HEADS * VDIM), lambda b, t: (b, t, 0)),
        out_shape=jax.ShapeDtypeStruct((BATCH, nt * TM, HEADS * VDIM), BF16),
        compiler_params=_params(("parallel", "arbitrary")),
        name="mla_attn",
    )(q, k, v)


def _ssd_kernel(xc_ref, dt_ref, dtt_ref, bias_row_ref, bias_col_ref, alog_row_ref, alog_col_ref, dfull_ref,
                y_ref, h_ref):
    row = lax.broadcasted_iota(jnp.int32, (CHUNK, CHUNK), 0)
    col = lax.broadcasted_iota(jnp.int32, (CHUNK, CHUNK), 1)
    lower = (col <= row)
    upper = (col >= row)
    lower01 = jnp.where(lower, 1.0, 0.0).astype(BF16)
    upper01 = jnp.where(upper, 1.0, 0.0).astype(BF16)
    lo_half = lax.broadcasted_iota(jnp.int32, (CHUNK, 128), 1) < SSM_P
    a_row = -jnp.exp(alog_row_ref[...])
    a_col = -jnp.exp(alog_col_ref[...])

    def chunk_step(c, d):
        r0 = pl.multiple_of(c * CHUNK, CHUNK)
        xc = xc_ref[0, pl.ds(r0, CHUNK), :]
        dcol = jax.nn.softplus(dt_ref[0, pl.ds(r0, CHUNK), :] + bias_row_ref[...])
        drow = jax.nn.softplus(dtt_ref[0, :, pl.ds(r0, CHUNK)] + bias_col_ref[...])
        tri_l, tri_r, mask = (lower01, upper01, lower) if d == 0 else (upper01, lower01, upper)
        acum_c = _dot_left01(tri_l, dcol * a_row)
        acum_r = _dot_right01(drow * a_col, tri_r)
        atot_c = acum_c[CHUNK - 1:CHUNK] if d == 0 else acum_c[0:1]
        w_c = jnp.exp(atot_c - acum_c) * dcol
        etot_c = jnp.exp(atot_c)

        def head_col(arr, hd):
            lane = d * SSM_HEADS + hd
            return arr[:, lane:lane + 1]

        def pair_lanes(arr, p):
            return jnp.where(lo_half[:arr.shape[0]], head_col(arr, 2 * p), head_col(arr, 2 * p + 1))

        ys = []
        for g in range(SSM_GROUPS):
            bm = xc[:, SSM_INNER + g * SSM_N:SSM_INNER + (g + 1) * SSM_N]
            cm = xc[:, SSM_INNER + (SSM_GROUPS + g) * SSM_N:SSM_INNER + (SSM_GROUPS + g + 1) * SSM_N]
            bmb = bm.astype(BF16)
            cmb = cm.astype(BF16)
            cb = lax.dot_general(cmb, bmb, (((1,), (1,)), ((), ())), preferred_element_type=F32)
            bt = bm.T.astype(BF16)
            for pp in range(2):
                p = g * 2 + pp
                sc, a_bc = [], []
                for hh in range(2):
                    hd = 2 * p + hh
                    a_bc.append(jnp.broadcast_to(head_col(acum_c, hd), (CHUNK, CHUNK)))
                    diff = a_bc[hh] - acum_r[d * SSM_HEADS + hd:d * SSM_HEADS + hd + 1]
                    dec = jnp.exp(jnp.where(mask, diff, -jnp.inf))
                    sc.append(cb * dec * drow[d * SSM_HEADS + hd:d * SSM_HEADS + hd + 1])
                x_p = xc[:, p * 128:(p + 1) * 128]
                xw = (x_p * pair_lanes(w_c, p)).astype(BF16)
                s_t = jnp.dot(bt, xw, preferred_element_type=F32)
                h_prev = h_ref[d, p]
                e_p = jnp.exp(jnp.where(lo_half, a_bc[0], a_bc[1]))
                y_inter = jnp.dot(cmb, h_prev.astype(BF16), preferred_element_type=F32) * e_p
                pmat = jnp.concatenate(sc, axis=1).astype(BF16)
                xbd = jnp.concatenate([jnp.where(lo_half, x_p, 0.0), jnp.where(lo_half, 0.0, x_p)],
                                      axis=0).astype(BF16)
                ys.append(jnp.dot(pmat, xbd, preferred_element_type=F32) + y_inter)
                h_ref[d, p] = h_prev * pair_lanes(etot_c, p) + s_t
        y = jnp.concatenate(ys, axis=1)
        if d == 0:
            y = y + dfull_ref[...] * xc[:, :SSM_INNER]
        y_ref[0, pl.ds(r0, CHUNK), :] += y

    h_ref[...] = jnp.zeros_like(h_ref)
    y_ref[...] = jnp.zeros_like(y_ref)

    def scan_body(i, carry):
        chunk_step(i, 0)
        chunk_step(jnp.where(i < CTX_CHUNKS, CTX_CHUNKS - 1 - i, N_CHUNKS + CTX_CHUNKS - 1 - i), 1)
        return carry

    lax.fori_loop(0, N_CHUNKS, scan_body, 0, unroll=2)


def _ssd(xc, dt, dtt, w, l):
    return pl.pallas_call(
        _ssd_kernel,
        grid=(BATCH,),
        in_specs=[
            pl.BlockSpec((1, T, XBC), lambda b: (b, 0, 0)),
            pl.BlockSpec((1, T, 128), lambda b: (b, 0, 0)),
            pl.BlockSpec((1, 2 * SSM_HEADS, T), lambda b: (b, 0, 0)),
            _const_spec((1, 128), l),
            _const_spec((2 * SSM_HEADS, 1), l),
            _const_spec((1, 128), l),
            _const_spec((2 * SSM_HEADS, 1), l),
            _const_spec((1, SSM_INNER), l),
        ],
        out_specs=pl.BlockSpec((1, T, SSM_INNER), lambda b: (b, 0, 0)),
        out_shape=jax.ShapeDtypeStruct((BATCH, T, SSM_INNER), F32),
        scratch_shapes=[
            pltpu.VMEM((2, SSM_HEADS // 2, SSM_N, 2 * SSM_P), F32),
        ],
        compiler_params=_params(("arbitrary",)),
        name="ssd_bidir",
    )(xc, dt, dtt, w["dt_bias_row"], w["dt_bias_col"], w["a_log_row"], w["a_log_col"], w["d_full"])


def _mixout_kernel(x_ref, mod_ref, a_ref, y_ref, z_ref, uv_ref, gate_ref,
                   wmo_ref, sn_ref, wso_ref, gn_ref, ws_ref, bs_ref, wgo_ref, bg_ref, wout_ref, o_ref):
    x = x_ref[0]
    m = mod_ref[0]
    o_mla = jnp.dot(a_ref[0], wmo_ref[...], preferred_element_type=F32)

    ys = y_ref[0] * _silu(z_ref[0].astype(F32))
    o_ssm = _bdot(_rms(ys, sn_ref[...]), wso_ref[...])

    uv = jax.nn.gelu(uv_ref[0].astype(F32))
    u = uv[:, :GM_WIDTH]
    v = uv[:, GM_WIDTH:]
    v = v - jnp.mean(v, axis=-1, keepdims=True)
    v = v * lax.rsqrt(jnp.mean(v * v, axis=-1, keepdims=True) + EPS) * gn_ref[...]
    lo_half = lax.broadcasted_iota(jnp.int32, (CHUNK, 128), 1) < GM_GDIM
    mixed = []
    for ch in range(TM // CHUNK):
        parts = []
        for p in range(GM_GROUPS // 2):
            vp = v[ch * CHUNK:(ch + 1) * CHUNK, p * 128:(p + 1) * 128]
            vbd = jnp.concatenate([jnp.where(lo_half, vp, 0.0), jnp.where(lo_half, 0.0, vp)], axis=0)
            parts.append(jnp.dot(ws_ref[p], vbd.astype(BF16), preferred_element_type=F32))
        mixed.append(jnp.concatenate(parts, axis=1) + bs_ref[...])
    mixed = jnp.concatenate(mixed, axis=0)
    o_gm = _bdot(u * mixed, wgo_ref[...])

    gts = jax.nn.sigmoid(gate_ref[0].astype(F32) + bg_ref[...])
    merged = gts[:, :D] * o_mla + gts[:, D:2 * D] * o_ssm + gts[:, 2 * D:] * o_gm
    y = _bdot(merged, wout_ref[...])
    o_ref[0] = x + m[5:6] * y


def _mixout(xs, a, y, z, uv, gate, w, l, *, t0):
    nt = T // TM - t0
    return pl.pallas_call(
        _mixout_kernel,
        grid=(BATCH, nt),
        in_specs=[
            _row_spec(D, t0),
            _mod_spec(l, t0),
            pl.BlockSpec((1, TM, HEADS * VDIM), lambda b, t: (b, t, 0)),
            _row_spec(SSM_INNER, t0),
            _row_spec(SSM_INNER, t0),
            _row_spec(2 * GM_WIDTH, t0),
            _row_spec(N_BRANCH * D, t0),
            _const_spec((HEADS * VDIM, D), l),
            _const_spec((1, SSM_INNER), l),
            _const_spec((SSM_INNER, D), l),
            _const_spec((1, GM_WIDTH), l),
            _const_spec((GM_GROUPS // 2, CHUNK, 2 * CHUNK), l),
            _const_spec((CHUNK, GM_WIDTH), l),
            _const_spec((GM_WIDTH, D), l),
            _const_spec((1, N_BRANCH * D), l),
            _const_spec((D, D), l),
        ],
        out_specs=pl.BlockSpec((1, TM, D), lambda b, t: (b, t, 0)),
        out_shape=jax.ShapeDtypeStruct((BATCH, nt * TM, D), F32),
        compiler_params=_params(("parallel", "parallel")),
        name="mixer_out",
    )(xs, w["mods"], a, y, z, uv, gate, w["w_mla_o"], w["ssm_norm"], w["w_ssm_o"], w["gm_norm"], w["w_s"],
      w["b_s"], w["w_gm_o"], w["b_gate"], w["w_out"])


def _rot_half(w):
    half = ROPE // 2
    return jnp.concatenate([-w[..., half:], w[..., :half]], axis=-1)


def _head_pad(nope, rope):
    pad = jnp.zeros(nope.shape[:-1] + (HEAD_PAD - NOPE - ROPE,), nope.dtype)
    out = jnp.concatenate([nope, rope, pad], axis=-1)
    return out.reshape(out.shape[:-2] + (HEADS * HEAD_PAD,))


def _prep_weights(w_in, mla_q_norm, mla_w_uq, mla_kv_norm, mla_w_ukv, mla_w_o, ssm_conv_w, ssm_conv_b,
                  ssm_a_log, ssm_dt_bias, ssm_d, ssm_norm, ssm_w_o, gm_norm, gm_w_s, gm_b_s, gm_w_o, b_gate, w_out):
    w = w_in
    o = 0
    kv = w[..., o:o + KV_RANK]; o += KV_RANK
    kr = w[..., o:o + ROPE]; o += ROPE
    xbc = w[..., o:o + XBC]; o += XBC
    dtw = w[..., o:o + 2 * SSM_HEADS]; o += 2 * SSM_HEADS
    rest = w[..., o:]

    def lanes(cols, start):
        return jnp.pad(cols, ((0, 0), (0, 0), (start, 128 - start - cols.shape[-1])))

    pieces = [kv, lanes(kr, NOPE), lanes(_rot_half(kr), NOPE), lanes(dtw, 0), xbc]
    w_kv_side = jnp.concatenate([p.astype(BF16) for p in pieces], axis=-1)

    uq = mla_w_uq.astype(BF16).reshape(DEPTH, Q_RANK, HEADS, NOPE + ROPE)
    uq_n, uq_r = uq[..., :NOPE], uq[..., NOPE:]
    ukv = mla_w_ukv.astype(BF16).reshape(DEPTH, KV_RANK, HEADS, NOPE + VDIM)
    kn = _head_pad(ukv[..., :NOPE], jnp.zeros((DEPTH, KV_RANK, HEADS, ROPE), BF16))
    v_t = jnp.pad(jnp.transpose(ukv[..., NOPE:], (0, 2, 3, 1)), ((0, 0), (0, 0), (0, V_ROWS - VDIM), (0, 0)))
    v_ones = jnp.tile((jnp.arange(V_ROWS) == VDIM).astype(F32), HEADS).reshape(HEADS * V_ROWS, 1)

    ws = gm_w_s.astype(BF16)
    ws_pair = jnp.concatenate([ws[:, 0::2], ws[:, 1::2]], axis=3)

    def row128(vals, fill=0.0):
        vals = vals.reshape(DEPTH, 1, -1)
        return jnp.pad(vals, ((0, 0), (0, 0), (0, 128 - vals.shape[-1])), constant_values=fill)

    return {
        "w_in_kv": w_kv_side,
        "w_in_q": rest.astype(BF16),
        "kv_norm": mla_kv_norm.reshape(DEPTH, 1, KV_RANK),
        "w_kn": kn,
        "w_v_t": v_t.reshape(DEPTH, HEADS * V_ROWS, KV_RANK),
        "v_ones": v_ones,
        "q_norm": mla_q_norm.reshape(DEPTH, 1, Q_RANK),
        "w_q": _head_pad(uq_n, uq_r),
        "w_q_rot": _head_pad(jnp.zeros_like(uq_n), _rot_half(uq_r)),
        "conv_w": jnp.pad(jnp.swapaxes(ssm_conv_w, 1, 2), ((0, 0), (0, 8 - SSM_CONV), (0, 0))),
        "conv_b": ssm_conv_b.reshape(DEPTH, 1, XBC),
        "dt_bias_row": row128(ssm_dt_bias),
        "dt_bias_col": ssm_dt_bias.reshape(DEPTH, 2 * SSM_HEADS, 1),
        "a_log_row": row128(ssm_a_log, fill=-80.0),
        "a_log_col": ssm_a_log.reshape(DEPTH, 2 * SSM_HEADS, 1),
        "d_full": jnp.repeat(ssm_d, SSM_P, axis=1).reshape(DEPTH, 1, SSM_INNER),
        "w_mla_o": mla_w_o.astype(BF16),
        "ssm_norm": ssm_norm.reshape(DEPTH, 1, SSM_INNER),
        "w_ssm_o": ssm_w_o.astype(BF16),
        "gm_norm": gm_norm.reshape(DEPTH, 1, GM_WIDTH),
        "w_s": ws_pair,
        "b_s": jnp.repeat(jnp.swapaxes(gm_b_s, 1, 2), GM_GDIM, axis=2),
        "w_gm_o": gm_w_o.astype(BF16),
        "b_gate": b_gate.reshape(DEPTH, 1, N_BRANCH * D),
        "w_out": w_out.astype(BF16),
    }


def _rope_tables():
    rows = SEQ // GRID_W
    r = jnp.repeat(jnp.arange(rows, dtype=F32), GRID_W)
    c = jnp.tile(jnp.arange(GRID_W, dtype=F32), rows)
    n_freq = ROPE // 4
    inv = jnp.power(ROPE_BASE, -jnp.arange(n_freq, dtype=F32) / n_freq)
    ang = jnp.concatenate([r[:, None] * inv, c[:, None] * inv], axis=-1)
    cos, sin = jnp.cos(ang), jnp.sin(ang)
    cos = jnp.concatenate([jnp.ones((CTX, ROPE // 2), F32), cos], axis=0)
    sin = jnp.concatenate([jnp.zeros((CTX, ROPE // 2), F32), sin], axis=0)
    ones = jnp.ones((T, NOPE), F32)
    zeros_n = jnp.zeros((T, NOPE), F32)
    zeros_p = jnp.zeros((T, HEAD_PAD - NOPE - ROPE), F32)
    cos_t = jnp.concatenate([ones, cos, cos, zeros_p], axis=1)
    sin_t = jnp.concatenate([zeros_n, sin, sin, zeros_p], axis=1)
    return cos_t, sin_t


def kernel(x, c, ctx, c_ctx, w_ada, b_ada, norm_g, ffn1_w_in, ffn1_w_out, ffn2_w_in, ffn2_w_out, w_in, mla_q_norm,
           mla_w_uq, mla_kv_norm, mla_w_ukv, mla_w_o, ssm_conv_w, ssm_conv_b, ssm_a_log, ssm_dt_bias, ssm_d,
           ssm_norm, ssm_w_o, gm_norm, gm_w_s, gm_b_s, gm_w_o, b_gate, w_out, final_norm):
    c_all = jnp.concatenate([c, c_ctx[None, :], jnp.zeros((MOD_ROWS - BATCH - 1, D), F32)], axis=0)
    w = _prep_weights(w_in, mla_q_norm, mla_w_uq, mla_kv_norm, mla_w_ukv, mla_w_o, ssm_conv_w, ssm_conv_b, ssm_a_log,
                      ssm_dt_bias, ssm_d, ssm_norm, ssm_w_o, gm_norm, gm_w_s, gm_b_s, gm_w_o, b_gate, w_out)
    w["mods"] = _ada(c_all, w_ada, b_ada).reshape(DEPTH, MOD_ROWS, N_MOD, D)
    w["norm_g"] = norm_g.reshape(DEPTH * 3, 1, D)
    w["ffn_w_in"] = (ffn1_w_in, ffn2_w_in)
    w["ffn_w_out"] = (ffn1_w_out, ffn2_w_out)
    w["final_norm"] = final_norm.reshape(1, D)
    w["cos"], w["sin"] = _rope_tables()

    xs = (ctx, x)
    for l in range(DEPTH):
        last = l == DEPTH - 1
        t0 = 1 if last else 0
        xs = _ffn(xs, w, l, 0)
        q, k, vt, xbc, dt, dtt, z, uv, gate = _proj(xs, w, l, ctx_queries=not last)
        a = _attn(q, k, vt, t0=t0)
        y = _ssd(xbc, dt, dtt, w, l)
        xs = _mixout(xs, a, y, z, uv, gate, w, l, t0=t0)
        xs = _ffn(xs, w, l, 1, latent_only=last, final=last)
    return xs
```

```python
import functools
import math

import jax
import jax.numpy as jnp
from jax import lax
from jax.experimental import pallas as pl
from jax.experimental.pallas import tpu as pltpu

F32 = jnp.float32
BF16 = jnp.bfloat16

D = 1024
BATCH = 8
SEQ = 2048
DEPTH = 2
CTX = 256
T = CTX + SEQ
GRID_W = 64
EPS = 1e-6

HEADS = 8
NOPE = 64
ROPE = 32
VDIM = 64
Q_RANK = 256
KV_RANK = 128
ROPE_BASE = 10000.0
ATTN_SCALE = (NOPE + ROPE) ** -0.5
EXP2_SCALE = ATTN_SCALE * math.log2(math.e)
HEAD_PAD = 128
V_ROWS = 80

SSM_HEADS = 8
SSM_P = 64
SSM_INNER = SSM_HEADS * SSM_P
SSM_GROUPS = 2
SSM_N = 128
SSM_CONV = 5
CHUNK = 128
XBC = SSM_INNER + 2 * SSM_GROUPS * SSM_N
N_CHUNKS = T // CHUNK
CTX_CHUNKS = CTX // CHUNK
HALO = 8

GM_GROUPS = 8
GM_WIDTH = 512
GM_GDIM = GM_WIDTH // GM_GROUPS

D_FF = 2816
STAGE_SLOTS = 4
WI_STAGE_ROWS = 32
WO_STAGE_ROWS = 176
N_BRANCH = 3
N_MOD = 9
MOD_ROWS = 16
CTX_MOD_ROW = BATCH

C_KV = 0
C_KR = 128
C_KRR = 256
C_DT = 384
C_XBC = 512
C_Q = C_XBC + XBC
C_Z = C_Q + Q_RANK
C_UV = C_Z + SSM_INNER
C_GATE = C_UV + 2 * GM_WIDTH
C_END = C_GATE + N_BRANCH * D

TM = 256
TM_LATENT = 512
KEY_BLOCK = 128
CONV_ROWS = 64
PROJ_PIECE = 512
VMEM_LIMIT = 56 * 1024 * 1024


def _rms(x, g):
    y = x * lax.rsqrt(jnp.mean(x * x, axis=-1, keepdims=True) + EPS)
    return y * g


def _silu(x):
    return x * jax.nn.sigmoid(x)


def _bdot(a, b):
    return jnp.dot(a.astype(BF16), b.astype(BF16), preferred_element_type=F32)


def _split3(a):
    a1 = a.astype(BF16)
    r1 = a - a1.astype(F32)
    a2 = r1.astype(BF16)
    a3 = (r1 - a2.astype(F32)).astype(BF16)
    return a1, a2, a3


def _dot_right01(a, m01):
    return sum(jnp.dot(p, m01, preferred_element_type=F32) for p in _split3(a))


def _dot_left01(m01, a):
    return sum(jnp.dot(m01, p, preferred_element_type=F32) for p in _split3(a))


def _const_spec(shape, l=None):
    nd = len(shape)
    if l is None:
        return pl.BlockSpec(shape, lambda *_: (0,) * nd, pipeline_mode=pl.Buffered(1))
    return pl.BlockSpec((None,) + shape, lambda *_: (l,) + (0,) * nd, pipeline_mode=pl.Buffered(1))


def _mod_spec(l, t0):
    return pl.BlockSpec((None, 1, N_MOD, D), lambda b, t: (l, jnp.where(t + t0 == 0, CTX_MOD_ROW, b), 0, 0))


def _row_spec(width, t0):
    return pl.BlockSpec((1, TM, width), lambda b, t: (b, t + t0, 0))


def _params(sem):
    return pltpu.CompilerParams(dimension_semantics=sem, vmem_limit_bytes=VMEM_LIMIT)


def _ada_kernel(c_ref, w_ref, b_ref, o_ref):
    s = _silu(c_ref[...])
    o_ref[0] = _bdot(s, w_ref[0]) + b_ref[0]


def _ada(c_all, w_ada, b_ada):
    tn = 1152
    nn = N_MOD * D // tn
    return pl.pallas_call(
        _ada_kernel,
        grid=(DEPTH, nn),
        in_specs=[
            pl.BlockSpec((MOD_ROWS, D), lambda l, j: (0, 0)),
            pl.BlockSpec((1, D, tn), lambda l, j: (l, 0, j)),
            pl.BlockSpec((1, 1, tn), lambda l, j: (l, 0, j)),
        ],
        out_specs=pl.BlockSpec((1, MOD_ROWS, tn), lambda l, j: (l, 0, j)),
        out_shape=jax.ShapeDtypeStruct((DEPTH, MOD_ROWS, N_MOD * D), F32),
        compiler_params=_params(("arbitrary", "arbitrary")),
        name="ada_mod",
    )(c_all, w_ada, b_ada.reshape(DEPTH, 1, N_MOD * D))


def _stage_bf16(src_hbm, dst_ref, stage_ref, sem, rows):
    n = src_hbm.shape[0] // rows

    def copy(i):
        slot = i % STAGE_SLOTS
        return pltpu.make_async_copy(src_hbm.at[pl.ds(i * rows, rows), :], stage_ref.at[slot], sem.at[slot])

    for i in range(min(STAGE_SLOTS - 1, n)):
        copy(i).start()
    for i in range(n):
        if i + STAGE_SLOTS - 1 < n:
            copy(i + STAGE_SLOTS - 1).start()
        copy(i).wait()
        dst_ref[i * rows:(i + 1) * rows, :] = stage_ref[i % STAGE_SLOTS].astype(BF16)


def _ffn_kernel(*refs, layer, mod0, final, split_input):
    if split_input:
        ctx_ref, x_ref, mod_ref, g_ref, wi_hbm, wo_hbm, fin_ref, o_ref, wi_ref, wo_ref, si_ref, so_ref, sem = refs
        x = jnp.where(pl.program_id(1) == 0, ctx_ref[0], x_ref[0])
    else:
        x_ref, mod_ref, g_ref, wi_hbm, wo_hbm, fin_ref, o_ref, wi_ref, wo_ref, si_ref, so_ref, sem = refs
        x = x_ref[0]

    @pl.when((pl.program_id(0) == 0) & (pl.program_id(1) == 0))
    def _():
        _stage_bf16(wi_hbm.at[layer], wi_ref, si_ref, sem.at[0], WI_STAGE_ROWS)
        _stage_bf16(wo_hbm.at[layer], wo_ref, so_ref, sem.at[1], WO_STAGE_ROWS)

    m = mod_ref[0]
    shift, scale, gate = m[mod0:mod0 + 1], m[mod0 + 1:mod0 + 2], m[mod0 + 2:mod0 + 3]
    h = _rms(x, g_ref[...]) * (1.0 + scale) + shift
    gu = jnp.dot(h.astype(BF16), wi_ref[...], preferred_element_type=F32)
    a = _silu(gu[:, :D_FF]) * gu[:, D_FF:]
    y = jnp.dot(a.astype(BF16), wo_ref[...], preferred_element_type=F32)
    out = x + (0.5 * gate) * y
    if final:
        out = _rms(out, fin_ref[...])
    o_ref[0] = out


def _ffn(xs, w, l, which, *, latent_only=False, final=False):
    split_input = isinstance(xs, tuple)
    t0 = 1 if latent_only else 0
    tm = TM_LATENT if latent_only else TM
    nt = SEQ // tm if latent_only else T // TM
    if split_input:
        x_specs = [pl.BlockSpec((1, TM, D), lambda b, t: (b, 0, 0)),
                   pl.BlockSpec((1, TM, D), lambda b, t: (b, jnp.maximum(t - 1, 0), 0))]
        xs_args = list(xs)
    else:
        x_specs = [pl.BlockSpec((1, tm, D), lambda b, t: (b, t, 0))]
        xs_args = [xs]
    return pl.pallas_call(
        functools.partial(_ffn_kernel, layer=l, mod0=6 * which, final=final, split_input=split_input),
        grid=(BATCH, nt),
        in_specs=x_specs + [
            _mod_spec(l, t0),
            _const_spec((1, D), 3 * l + 2 * which),
            pl.BlockSpec(memory_space=pl.ANY),
            pl.BlockSpec(memory_space=pl.ANY),
            _const_spec((1, D)),
        ],
        out_specs=pl.BlockSpec((1, tm, D), lambda b, t: (b, t, 0)),
        out_shape=jax.ShapeDtypeStruct((BATCH, nt * tm, D), F32),
        scratch_shapes=[
            pltpu.VMEM((D, 2 * D_FF), BF16),
            pltpu.VMEM((D_FF, D), BF16),
            pltpu.VMEM((STAGE_SLOTS, WI_STAGE_ROWS, 2 * D_FF), F32),
            pltpu.VMEM((STAGE_SLOTS, WO_STAGE_ROWS, D), F32),
            pltpu.SemaphoreType.DMA((2, STAGE_SLOTS)),
        ],
        compiler_params=_params(("arbitrary", "arbitrary")),
        name="ffn",
    )(*xs_args, w["mods"], w["norm_g"], w["ffn_w_in"][which], w["ffn_w_out"][which], w["final_norm"])


def _proj_kernel(x_ref, xprev_ref, xnext_ref, mod_ref, g_ref, wkv_ref, wq_side_ref, cw_ref, cb_ref, kvn_ref, wkn_ref,
                 wvt_ref, one_ref, qn_ref, wq_ref, wqr_ref, cos_ref, sin_ref,
                 q_out, k_out, vt_out, xc_out, dt_out, dtt_out, z_out, uv_out, gate_out, *, ctx_queries):
    t = pl.program_id(1)
    m = mod_ref[0]
    xe = jnp.concatenate([xprev_ref[0], x_ref[0], xnext_ref[0]], axis=0)
    he = _rms(xe, g_ref[...]) * (1.0 + m[4:5]) + m[3:4]
    hb = he[HALO:HALO + TM].astype(BF16)
    he = he.astype(BF16)
    cos = cos_ref[...]
    sin = sin_ref[...]

    small = jnp.dot(hb, wkv_ref[:, C_KV:C_XBC], preferred_element_type=F32)

    xbc = jnp.dot(he, wkv_ref[:, C_XBC:C_Q], preferred_element_type=F32)
    row = lax.broadcasted_iota(jnp.int32, (TM + 2 * HALO, 1), 0)
    has_prev = t >= 2
    has_next = (t >= 1) & (t < T // TM - 1)
    keep = ((row >= HALO) | has_prev) & ((row < HALO + TM) | has_next)
    def conv_lane_tile(c0):
        for r0 in range(0, TM, CONV_ROWS):
            win = xbc[r0:r0 + CONV_ROWS + 2 * HALO, c0:c0 + 128]
            if r0 == 0 or r0 + CONV_ROWS == TM:
                win = jnp.where(keep[r0:r0 + CONV_ROWS + 2 * HALO], win, 0.0)
            acc = cb_ref[:, c0:c0 + 128]
            for k in range(SSM_CONV):
                shift = (SSM_CONV // 2 - k) % (CONV_ROWS + 2 * HALO)
                tap = win if shift == 0 else pltpu.roll(win, shift, axis=0)
                acc = acc + cw_ref[k:k + 1, c0:c0 + 128] * tap[HALO:HALO + CONV_ROWS]
            xc_out[0, r0:r0 + CONV_ROWS, c0:c0 + 128] = _silu(acc)

    def query_side():
        def proj(c0, c1):
            return jnp.dot(hb, wq_side_ref[:, c0 - C_Q:c1 - C_Q], preferred_element_type=F32)

        q_lat = proj(C_Q, C_Z)
        conv_tiles = list(range(0, XBC, 128))
        for out, base, width in ((z_out, C_Z, SSM_INNER), (uv_out, C_UV, 2 * GM_WIDTH), (gate_out, C_GATE, N_BRANCH * D)):
            for c0 in range(0, width, PROJ_PIECE):
                out[0, :, c0:c0 + PROJ_PIECE] = proj(base + c0, base + c0 + PROJ_PIECE).astype(out.dtype)
                if conv_tiles:
                    conv_lane_tile(conv_tiles.pop(0))
        assert not conv_tiles
        qn = _rms(q_lat, qn_ref[...]).astype(BF16)
        qa = jnp.dot(qn, wq_ref[...], preferred_element_type=F32)
        qb = jnp.dot(qn, wqr_ref[...], preferred_element_type=F32)
        for hd in range(HEADS):
            sl = slice(hd * HEAD_PAD, (hd + 1) * HEAD_PAD)
            q_out[0, :, sl] = (qa[:, sl] * cos + qb[:, sl] * sin).astype(BF16)

    def query_side_unused():
        for out in (q_out, z_out, uv_out, gate_out):
            out[...] = jnp.zeros_like(out)
        for c0 in range(0, XBC, 128):
            conv_lane_tile(c0)

    if ctx_queries:
        query_side()
    else:
        pl.when(t != 0)(query_side)
        pl.when(t == 0)(query_side_unused)

    dt = small[:, C_DT:C_DT + 128]
    dt_out[0] = dt
    dtt_out[0] = dt.T[:2 * SSM_HEADS]

    kvn = _rms(small[:, C_KV:C_KV + 128], kvn_ref[...]).astype(BF16)
    kn = jnp.dot(kvn, wkn_ref[...], preferred_element_type=F32)
    kr = small[:, C_KR:C_KR + 128] * cos + small[:, C_KRR:C_KRR + 128] * sin
    for hd in range(HEADS):
        sl = slice(hd * HEAD_PAD, (hd + 1) * HEAD_PAD)
        k_out[0, :, sl] = (kn[:, sl] + kr).astype(BF16)
    vt = lax.dot_general(wvt_ref[...], kvn, (((1,), (1,)), ((), ())), preferred_element_type=F32)
    vt_out[0] = (vt + one_ref[...]).astype(BF16)


def _proj(xs, w, l, *, ctx_queries):
    nt = T // TM

    def rows(w, dt):
        return jax.ShapeDtypeStruct((BATCH, T, w), dt), _row_spec(w, 0)

    def cols(r, dt):
        return jax.ShapeDtypeStruct((BATCH, r, T), dt), pl.BlockSpec((1, r, TM), lambda b, t: (b, 0, t))

    outs = [rows(HEADS * HEAD_PAD, BF16), rows(HEADS * HEAD_PAD, BF16), cols(HEADS * V_ROWS, BF16),
            rows(XBC, F32), rows(128, F32), cols(2 * SSM_HEADS, F32),
            rows(SSM_INNER, BF16), rows(2 * GM_WIDTH, BF16), rows(N_BRANCH * D, F32)]
    out_shape = [o[0] for o in outs]
    out_specs = [o[1] for o in outs]
    return pl.pallas_call(
        functools.partial(_proj_kernel, ctx_queries=ctx_queries),
        grid=(BATCH, nt),
        in_specs=[
            _row_spec(D, 0),
            pl.BlockSpec((1, HALO, D), lambda b, t: (b, jnp.maximum(t * (TM // HALO) - 1, 0), 0)),
            pl.BlockSpec((1, HALO, D), lambda b, t: (b, jnp.minimum((t + 1) * (TM // HALO), T // HALO - 1), 0)),
            _mod_spec(l, 0),
            _const_spec((1, D), 3 * l + 1),
            _const_spec((D, C_Q), l),
            _const_spec((D, C_END - C_Q), l),
            _const_spec((8, XBC), l),
            _const_spec((1, XBC), l),
            _const_spec((1, KV_RANK), l),
            _const_spec((KV_RANK, HEADS * HEAD_PAD), l),
            _const_spec((HEADS * V_ROWS, KV_RANK), l),
            _const_spec((HEADS * V_ROWS, 1)),
            _const_spec((1, Q_RANK), l),
            _const_spec((Q_RANK, HEADS * HEAD_PAD), l),
            _const_spec((Q_RANK, HEADS * HEAD_PAD), l),
            pl.BlockSpec((TM, HEAD_PAD), lambda b, t: (t, 0)),
            pl.BlockSpec((TM, HEAD_PAD), lambda b, t: (t, 0)),
        ],
        out_specs=out_specs,
        out_shape=out_shape,
        compiler_params=_params(("parallel", "parallel")),
        name="mixer_proj",
    )(xs, xs, xs, w["mods"], w["norm_g"], w["w_in_kv"], w["w_in_q"], w["conv_w"], w["conv_b"], w["kv_norm"], w["w_kn"],
      w["w_v_t"],
      w["v_ones"],
      w["q_norm"], w["w_q"], w["w_q_rot"], w["cos"], w["sin"])


def _attn_kernel(q_ref, k_ref, vt_ref, o_ref, *, t0):
    def run(nk):
        nblk = nk // KEY_BLOCK

        def score_block(hd, j):
            q = q_ref[0, :, hd * HEAD_PAD:(hd + 1) * HEAD_PAD]
            k = k_ref[0, j * KEY_BLOCK:(j + 1) * KEY_BLOCK, hd * HEAD_PAD:(hd + 1) * HEAD_PAD]
            return lax.dot_general(k, q, (((1,), (1,)), ((), ())), preferred_element_type=F32)

        def col_max(blocks):
            mx = functools.reduce(jnp.maximum, blocks)
            mx = jnp.max(mx.reshape(KEY_BLOCK // 8, 8, TM), axis=0)
            return jnp.max(mx, axis=0, keepdims=True)

        outs = []
        cur = [score_block(0, j) for j in range(nblk)]
        for hd in range(HEADS):
            mx = col_max(cur)
            nxt, ps = [], []
            for j in range(nblk):
                if hd + 1 < HEADS:
                    nxt.append(score_block(hd + 1, j))
                ps.append(jnp.exp2((cur[j] - mx) * EXP2_SCALE).astype(BF16))
            p = jnp.concatenate(ps, axis=0)
            ot = jnp.dot(vt_ref[0, hd * V_ROWS:(hd + 1) * V_ROWS, :nk], p, preferred_element_type=F32)
            outs.append(ot[:VDIM] / ot[VDIM:VDIM + 1])
            cur = nxt
        o_ref[0] = jnp.concatenate(outs, axis=0).T.astype(BF16)

    if t0 == 0:
        t = pl.program_id(1)
        pl.when(t == 0)(lambda: run(CTX))
        pl.when(t != 0)(lambda: run(T))
    else:
        run(T)


def _attn(q, k, v, *, t0):
    nt = T // TM - t0
    return pl.pallas_call(
        functools.partial(_attn_kernel, t0=t0),
        grid=(BATCH, nt),
        in_specs=[
            _row_spec(HEADS * HEAD_PAD, t0),
            pl.BlockSpec((1, T, HEADS * HEAD_PAD), lambda b, t: (b, 0, 0)),
            pl.BlockSpec((1, HEADS * V_ROWS, T), lambda b, t: (b, 0, 0)),
        ],
        out_specs=pl.BlockSpec((1, TM, HEADS * VDIM), lambda b, t: (b, t, 0)),
        out_shape=jax.ShapeDtypeStruct((BATCH, nt * TM, HEADS * VDIM), BF16),
        compiler_params=_params(("parallel", "arbitrary")),
        name="mla_attn",
    )(q, k, v)


def _ssd_kernel(xc_ref, dt_ref, dtt_ref, bias_row_ref, bias_col_ref, alog_row_ref, alog_col_ref, dfull_ref,
                y_ref, h_ref):
    row = lax.broadcasted_iota(jnp.int32, (CHUNK, CHUNK), 0)
    col = lax.broadcasted_iota(jnp.int32, (CHUNK, CHUNK), 1)
    lower = (col <= row)
    upper = (col >= row)
    lower01 = jnp.where(lower, 1.0, 0.0).astype(BF16)
    upper01 = jnp.where(upper, 1.0, 0.0).astype(BF16)
    lo_half = lax.broadcasted_iota(jnp.int32, (CHUNK, 128), 1) < SSM_P
    a_row = -jnp.exp(alog_row_ref[...])
    a_col = -jnp.exp(alog_col_ref[...])

    def chunk_step(c, d):
        r0 = pl.multiple_of(c * CHUNK, CHUNK)
        xc = xc_ref[0, pl.ds(r0, CHUNK), :]
        dcol = jax.nn.softplus(dt_ref[0, pl.ds(r0, CHUNK), :] + bias_row_ref[...])
        drow = jax.nn.softplus(dtt_ref[0, :, pl.ds(r0, CHUNK)] + bias_col_ref[...])
        tri_l, tri_r, mask = (lower01, upper01, lower) if d == 0 else (upper01, lower01, upper)
        acum_c = _dot_left01(tri_l, dcol * a_row)
        acum_r = _dot_right01(drow * a_col, tri_r)
        atot_c = acum_c[CHUNK - 1:CHUNK] if d == 0 else acum_c[0:1]
        w_c = jnp.exp(atot_c - acum_c) * dcol
        etot_c = jnp.exp(atot_c)

        def head_col(arr, hd):
            lane = d * SSM_HEADS + hd
            return arr[:, lane:lane + 1]

        def pair_lanes(arr, p):
            return jnp.where(lo_half[:arr.shape[0]], head_col(arr, 2 * p), head_col(arr, 2 * p + 1))

        ys = []
        for g in range(SSM_GROUPS):
            bm = xc[:, SSM_INNER + g * SSM_N:SSM_INNER + (g + 1) * SSM_N]
            cm = xc[:, SSM_INNER + (SSM_GROUPS + g) * SSM_N:SSM_INNER + (SSM_GROUPS + g + 1) * SSM_N]
            bmb = bm.astype(BF16)
            cmb = cm.astype(BF16)
            cb = lax.dot_general(cmb, bmb, (((1,), (1,)), ((), ())), preferred_element_type=F32)
            bt = bm.T.astype(BF16)
            for pp in range(2):
                p = g * 2 + pp
                sc, a_bc = [], []
                for hh in range(2):
                    hd = 2 * p + hh
                    a_bc.append(jnp.broadcast_to(head_col(acum_c, hd), (CHUNK, CHUNK)))
                    diff = a_bc[hh] - acum_r[d * SSM_HEADS + hd:d * SSM_HEADS + hd + 1]
                    dec = jnp.exp(jnp.where(mask, diff, -jnp.inf))
                    sc.append(cb * dec * drow[d * SSM_HEADS + hd:d * SSM_HEADS + hd + 1])
                x_p = xc[:, p * 128:(p + 1) * 128]
                xw = (x_p * pair_lanes(w_c, p)).astype(BF16)
                s_t = jnp.dot(bt, xw, preferred_element_type=F32)
                h_prev = h_ref[d, p]
                e_p = jnp.exp(jnp.where(lo_half, a_bc[0], a_bc[1]))
                y_inter = jnp.dot(cmb, h_prev.astype(BF16), preferred_element_type=F32) * e_p
                pmat = jnp.concatenate(sc, axis=1).astype(BF16)
                xbd = jnp.concatenate([jnp.where(lo_half, x_p, 0.0), jnp.where(lo_half, 0.0, x_p)],
                                      axis=0).astype(BF16)
                ys.append(jnp.dot(pmat, xbd, preferred_element_type=F32) + y_inter)
                h_ref[d, p] = h_prev * pair_lanes(etot_c, p) + s_t
        y = jnp.concatenate(ys, axis=1)
        if d == 0:
            y = y + dfull_ref[...] * xc[:, :SSM_INNER]
        y_ref[0, pl.ds(r0, CHUNK), :] += y

    h_ref[...] = jnp.zeros_like(h_ref)
    y_ref[...] = jnp.zeros_like(y_ref)

    def scan_body(i, carry):
        chunk_step(i, 0)
        chunk_step(jnp.where(i < CTX_CHUNKS, CTX_CHUNKS - 1 - i, N_CHUNKS + CTX_CHUNKS - 1 - i), 1)
        return carry

    lax.fori_loop(0, N_CHUNKS, scan_body, 0, unroll=2)


def _ssd(xc, dt, dtt, w, l):
    return pl.pallas_call(
        _ssd_kernel,
        grid=(BATCH,),
        in_specs=[
            pl.BlockSpec((1, T, XBC), lambda b: (b, 0, 0)),
            pl.BlockSpec((1, T, 128), lambda b: (b, 0, 0)),
            pl.BlockSpec((1, 2 * SSM_HEADS, T), lambda b: (b, 0, 0)),
            _const_spec((1, 128), l),
            _const_spec((2 * SSM_HEADS, 1), l),
            _const_spec((1, 128), l),
            _const_spec((2 * SSM_HEADS, 1), l),
            _const_spec((1, SSM_INNER), l),
        ],
        out_specs=pl.BlockSpec((1, T, SSM_INNER), lambda b: (b, 0, 0)),
        out_shape=jax.ShapeDtypeStruct((BATCH, T, SSM_INNER), F32),
        scratch_shapes=[
            pltpu.VMEM((2, SSM_HEADS // 2, SSM_N, 2 * SSM_P), F32),
        ],
        compiler_params=_params(("arbitrary",)),
        name="ssd_bidir",
    )(xc, dt, dtt, w["dt_bias_row"], w["dt_bias_col"], w["a_log_row"], w["a_log_col"], w["d_full"])


def _mixout_kernel(x_ref, mod_ref, a_ref, y_ref, z_ref, uv_ref, gate_ref,
                   wmo_ref, sn_ref, wso_ref, gn_ref, ws_ref, bs_ref, wgo_ref, bg_ref, wout_ref, o_ref):
    x = x_ref[0]
    m = mod_ref[0]
    o_mla = jnp.dot(a_ref[0], wmo_ref[...], preferred_element_type=F32)

    ys = y_ref[0] * _silu(z_ref[0].astype(F32))
    o_ssm = _bdot(_rms(ys, sn_ref[...]), wso_ref[...])

    uv = jax.nn.gelu(uv_ref[0].astype(F32))
    u = uv[:, :GM_WIDTH]
    v = uv[:, GM_WIDTH:]
    v = v - jnp.mean(v, axis=-1, keepdims=True)
    v = v * lax.rsqrt(jnp.mean(v * v, axis=-1, keepdims=True) + EPS) * gn_ref[...]
    lo_half = lax.broadcasted_iota(jnp.int32, (CHUNK, 128), 1) < GM_GDIM
    mixed = []
    for ch in range(TM // CHUNK):
        parts = []
        for p in range(GM_GROUPS // 2):
            vp = v[ch * CHUNK:(ch + 1) * CHUNK, p * 128:(p + 1) * 128]
            vbd = jnp.concatenate([jnp.where(lo_half, vp, 0.0), jnp.where(lo_half, 0.0, vp)], axis=0)
            parts.append(jnp.dot(ws_ref[p], vbd.astype(BF16), preferred_element_type=F32))
        mixed.append(jnp.concatenate(parts, axis=1) + bs_ref[...])
    mixed = jnp.concatenate(mixed, axis=0)
    o_gm = _bdot(u * mixed, wgo_ref[...])

    gts = jax.nn.sigmoid(gate_ref[0].astype(F32) + bg_ref[...])
    merged = gts[:, :D] * o_mla + gts[:, D:2 * D] * o_ssm + gts[:, 2 * D:] * o_gm
    y = _bdot(merged, wout_ref[...])
    o_ref[0] = x + m[5:6] * y


def _mixout(xs, a, y, z, uv, gate, w, l, *, t0):
    nt = T // TM - t0
    return pl.pallas_call(
        _mixout_kernel,
        grid=(BATCH, nt),
        in_specs=[
            _row_spec(D, t0),
            _mod_spec(l, t0),
            pl.BlockSpec((1, TM, HEADS * VDIM), lambda b, t: (b, t, 0)),
            _row_spec(SSM_INNER, t0),
            _row_spec(SSM_INNER, t0),
            _row_spec(2 * GM_WIDTH, t0),
            _row_spec(N_BRANCH * D, t0),
            _const_spec((HEADS * VDIM, D), l),
            _const_spec((1, SSM_INNER), l),
            _const_spec((SSM_INNER, D), l),
            _const_spec((1, GM_WIDTH), l),
            _const_spec((GM_GROUPS // 2, CHUNK, 2 * CHUNK), l),
            _const_spec((CHUNK, GM_WIDTH), l),
            _const_spec((GM_WIDTH, D), l),
            _const_spec((1, N_BRANCH * D), l),
            _const_spec((D, D), l),
        ],
        out_specs=pl.BlockSpec((1, TM, D), lambda b, t: (b, t, 0)),
        out_shape=jax.ShapeDtypeStruct((BATCH, nt * TM, D), F32),
        compiler_params=_params(("parallel", "parallel")),
        name="mixer_out",
    )(xs, w["mods"], a, y, z, uv, gate, w["w_mla_o"], w["ssm_norm"], w["w_ssm_o"], w["gm_norm"], w["w_s"],
      w["b_s"], w["w_gm_o"], w["b_gate"], w["w_out"])


def _rot_half(w):
    half = ROPE // 2
    return jnp.concatenate([-w[..., half:], w[..., :half]], axis=-1)


def _head_pad(nope, rope):
    pad = jnp.zeros(nope.shape[:-1] + (HEAD_PAD - NOPE - ROPE,), nope.dtype)
    out = jnp.concatenate([nope, rope, pad], axis=-1)
    return out.reshape(out.shape[:-2] + (HEADS * HEAD_PAD,))


def _prep_weights(w_in, mla_q_norm, mla_w_uq, mla_kv_norm, mla_w_ukv, mla_w_o, ssm_conv_w, ssm_conv_b,
                  ssm_a_log, ssm_dt_bias, ssm_d, ssm_norm, ssm_w_o, gm_norm, gm_w_s, gm_b_s, gm_w_o, b_gate, w_out):
    w = w_in
    o = 0
    kv = w[..., o:o + KV_RANK]; o += KV_RANK
    kr = w[..., o:o + ROPE]; o += ROPE
    xbc = w[..., o:o + XBC]; o += XBC
    dtw = w[..., o:o + 2 * SSM_HEADS]; o += 2 * SSM_HEADS
    rest = w[..., o:]

    def lanes(cols, start):
        return jnp.pad(cols, ((0, 0), (0, 0), (start, 128 - start - cols.shape[-1])))

    pieces = [kv, lanes(kr, NOPE), lanes(_rot_half(kr), NOPE), lanes(dtw, 0), xbc]
    w_kv_side = jnp.concatenate([p.astype(BF16) for p in pieces], axis=-1)

    uq = mla_w_uq.astype(BF16).reshape(DEPTH, Q_RANK, HEADS, NOPE + ROPE)
    uq_n, uq_r = uq[..., :NOPE], uq[..., NOPE:]
    ukv = mla_w_ukv.astype(BF16).reshape(DEPTH, KV_RANK, HEADS, NOPE + VDIM)
    kn = _head_pad(ukv[..., :NOPE], jnp.zeros((DEPTH, KV_RANK, HEADS, ROPE), BF16))
    v_t = jnp.pad(jnp.transpose(ukv[..., NOPE:], (0, 2, 3, 1)), ((0, 0), (0, 0), (0, V_ROWS - VDIM), (0, 0)))
    v_ones = jnp.tile((jnp.arange(V_ROWS) == VDIM).astype(F32), HEADS).reshape(HEADS * V_ROWS, 1)

    ws = gm_w_s.astype(BF16)
    ws_pair = jnp.concatenate([ws[:, 0::2], ws[:, 1::2]], axis=3)

    def row128(vals, fill=0.0):
        vals = vals.reshape(DEPTH, 1, -1)
        return jnp.pad(vals, ((0, 0), (0, 0), (0, 128 - vals.shape[-1])), constant_values=fill)

    return {
        "w_in_kv": w_kv_side,
        "w_in_q": rest.astype(BF16),
        "kv_norm": mla_kv_norm.reshape(DEPTH, 1, KV_RANK),
        "w_kn": kn,
        "w_v_t": v_t.reshape(DEPTH, HEADS * V_ROWS, KV_RANK),
        "v_ones": v_ones,
        "q_norm": mla_q_norm.reshape(DEPTH, 1, Q_RANK),
        "w_q": _head_pad(uq_n, uq_r),
        "w_q_rot": _head_pad(jnp.zeros_like(uq_n), _rot_half(uq_r)),
        "conv_w": jnp.pad(jnp.swapaxes(ssm_conv_w, 1, 2), ((0, 0), (0, 8 - SSM_CONV), (0, 0))),
        "conv_b": ssm_conv_b.reshape(DEPTH, 1, XBC),
        "dt_bias_row": row128(ssm_dt_bias),
        "dt_bias_col": ssm_dt_bias.reshape(DEPTH, 2 * SSM_HEADS, 1),
        "a_log_row": row128(ssm_a_log, fill=-80.0),
        "a_log_col": ssm_a_log.reshape(DEPTH, 2 * SSM_HEADS, 1),
        "d_full": jnp.repeat(ssm_d, SSM_P, axis=1).reshape(DEPTH, 1, SSM_INNER),
        "w_mla_o": mla_w_o.astype(BF16),
        "ssm_norm": ssm_norm.reshape(DEPTH, 1, SSM_INNER),
        "w_ssm_o": ssm_w_o.astype(BF16),
        "gm_norm": gm_norm.reshape(DEPTH, 1, GM_WIDTH),
        "w_s": ws_pair,
        "b_s": jnp.repeat(jnp.swapaxes(gm_b_s, 1, 2), GM_GDIM, axis=2),
        "w_gm_o": gm_w_o.astype(BF16),
        "b_gate": b_gate.reshape(DEPTH, 1, N_BRANCH * D),
        "w_out": w_out.astype(BF16),
    }


def _rope_tables():
    rows = SEQ // GRID_W
    r = jnp.repeat(jnp.arange(rows, dtype=F32), GRID_W)
    c = jnp.tile(jnp.arange(GRID_W, dtype=F32), rows)
    n_freq = ROPE // 4
    inv = jnp.power(ROPE_BASE, -jnp.arange(n_freq, dtype=F32) / n_freq)
    ang = jnp.concatenate([r[:, None] * inv, c[:, None] * inv], axis=-1)
    cos, sin = jnp.cos(ang), jnp.sin(ang)
    cos = jnp.concatenate([jnp.ones((CTX, ROPE // 2), F32), cos], axis=0)
    sin = jnp.concatenate([jnp.zeros((CTX, ROPE // 2), F32), sin], axis=0)
    ones = jnp.ones((T, NOPE), F32)
    zeros_n = jnp.zeros((T, NOPE), F32)
    zeros_p = jnp.zeros((T, HEAD_PAD - NOPE - ROPE), F32)
    cos_t = jnp.concatenate([ones, cos, cos, zeros_p], axis=1)
    sin_t = jnp.concatenate([zeros_n, sin, sin, zeros_p], axis=1)
    return cos_t, sin_t


def kernel(x, c, ctx, c_ctx, w_ada, b_ada, norm_g, ffn1_w_in, ffn1_w_out, ffn2_w_in, ffn2_w_out, w_in, mla_q_norm,
           mla_w_uq, mla_kv_norm, mla_w_ukv, mla_w_o, ssm_conv_w, ssm_conv_b, ssm_a_log, ssm_dt_bias, ssm_d,
           ssm_norm, ssm_w_o, gm_norm, gm_w_s, gm_b_s, gm_w_o, b_gate, w_out, final_norm):
    c_all = jnp.concatenate([c, c_ctx[None, :], jnp.zeros((MOD_ROWS - BATCH - 1, D), F32)], axis=0)
    w = _prep_weights(w_in, mla_q_norm, mla_w_uq, mla_kv_norm, mla_w_ukv, mla_w_o, ssm_conv_w, ssm_conv_b, ssm_a_log,
                      ssm_dt_bias, ssm_d, ssm_norm, ssm_w_o, gm_norm, gm_w_s, gm_b_s, gm_w_o, b_gate, w_out)
    w["mods"] = _ada(c_all, w_ada, b_ada).reshape(DEPTH, MOD_ROWS, N_MOD, D)
    w["norm_g"] = norm_g.reshape(DEPTH * 3, 1, D)
    w["ffn_w_in"] = (ffn1_w_in, ffn2_w_in)
    w["ffn_w_out"] = (ffn1_w_out, ffn2_w_out)
    w["final_norm"] = final_norm.reshape(1, D)
    w["cos"], w["sin"] = _rope_tables()

    xs = (ctx, x)
    for l in range(DEPTH):
        last = l == DEPTH - 1
        t0 = 1 if last else 0
        xs = _ffn(xs, w, l, 0)
        q, k, vt, xbc, dt, dtt, z, uv, gate = _proj(xs, w, l, ctx_queries=not last)
        a = _attn(q, k, vt, t0=t0)
        y = _ssd(xbc, dt, dtt, w, l)
        xs = _mixout(xs, a, y, z, uv, gate, w, l, t0=t0)
        xs = _ffn(xs, w, l, 1, latent_only=last, final=last)
    return xs
```

```python
import functools
import math

import jax
import jax.numpy as jnp
from jax import lax
from jax.experimental import pallas as pl
from jax.experimental.pallas import tpu as pltpu

F32 = jnp.float32
BF16 = jnp.bfloat16

D = 1024
BATCH = 8
SEQ = 2048
DEPTH = 2
CTX = 256
T = CTX + SEQ
GRID_W = 64
EPS = 1e-6

HEADS = 8
NOPE = 64
ROPE = 32
VDIM = 64
Q_RANK = 256
KV_RANK = 128
ROPE_BASE = 10000.0
ATTN_SCALE = (NOPE + ROPE) ** -0.5
EXP2_SCALE = ATTN_SCALE * math.log2(math.e)
HEAD_PAD = 128
V_ROWS = 80

SSM_HEADS = 8
SSM_P = 64
SSM_INNER = SSM_HEADS * SSM_P
SSM_GROUPS = 2
SSM_N = 128
SSM_CONV = 5
CHUNK = 128
XBC = SSM_INNER + 2 * SSM_GROUPS * SSM_N
N_CHUNKS = T // CHUNK
CTX_CHUNKS = CTX // CHUNK
HALO = 8

GM_GROUPS = 8
GM_WIDTH = 512
GM_GDIM = GM_WIDTH // GM_GROUPS

D_FF = 2816
N_BRANCH = 3
N_MOD = 9
MOD_ROWS = 16
CTX_MOD_ROW = BATCH

C_KV = 0
C_KR = 128
C_KRR = 256
C_DT = 384
C_XBC = 512
C_Q = C_XBC + XBC
C_Z = C_Q + Q_RANK
C_UV = C_Z + SSM_INNER
C_GATE = C_UV + 2 * GM_WIDTH
C_END = C_GATE + N_BRANCH * D

TM = 256
TM_LATENT = 512
KEY_BLOCK = 128
CONV_ROWS = 64
PROJ_PIECE = 512
VMEM_LIMIT = 56 * 1024 * 1024


def _rms(x, g):
    y = x * lax.rsqrt(jnp.mean(x * x, axis=-1, keepdims=True) + EPS)
    return y * g


def _silu(x):
    return x * jax.nn.sigmoid(x)


def _bdot(a, b):
    return jnp.dot(a.astype(BF16), b.astype(BF16), preferred_element_type=F32)


def _split3(a):
    a1 = a.astype(BF16)
    r1 = a - a1.astype(F32)
    a2 = r1.astype(BF16)
    a3 = (r1 - a2.astype(F32)).astype(BF16)
    return a1, a2, a3


def _dot_right01(a, m01):
    return sum(jnp.dot(p, m01, preferred_element_type=F32) for p in _split3(a))


def _dot_left01(m01, a):
    return sum(jnp.dot(m01, p, preferred_element_type=F32) for p in _split3(a))


def _const_spec(shape, l=None):
    nd = len(shape)
    if l is None:
        return pl.BlockSpec(shape, lambda *_: (0,) * nd, pipeline_mode=pl.Buffered(1))
    return pl.BlockSpec((None,) + shape, lambda *_: (l,) + (0,) * nd, pipeline_mode=pl.Buffered(1))


def _mod_spec(l, t0):
    return pl.BlockSpec((None, 1, N_MOD, D), lambda b, t: (l, jnp.where(t + t0 == 0, CTX_MOD_ROW, b), 0, 0))


def _row_spec(width, t0):
    return pl.BlockSpec((1, TM, width), lambda b, t: (b, t + t0, 0))


def _params(sem):
    return pltpu.CompilerParams(dimension_semantics=sem, vmem_limit_bytes=VMEM_LIMIT)


def _ada_kernel(c_ref, w_ref, b_ref, o_ref):
    s = _silu(c_ref[...])
    o_ref[0] = _bdot(s, w_ref[0]) + b_ref[0]


def _ada(c_all, w_ada, b_ada):
    tn = 1152
    nn = N_MOD * D // tn
    return pl.pallas_call(
        _ada_kernel,
        grid=(DEPTH, nn),
        in_specs=[
            pl.BlockSpec((MOD_ROWS, D), lambda l, j: (0, 0)),
            pl.BlockSpec((1, D, tn), lambda l, j: (l, 0, j)),
            pl.BlockSpec((1, 1, tn), lambda l, j: (l, 0, j)),
        ],
        out_specs=pl.BlockSpec((1, MOD_ROWS, tn), lambda l, j: (l, 0, j)),
        out_shape=jax.ShapeDtypeStruct((DEPTH, MOD_ROWS, N_MOD * D), F32),
        compiler_params=_params(("arbitrary", "arbitrary")),
        name="ada_mod",
    )(c_all, w_ada, b_ada.reshape(DEPTH, 1, N_MOD * D))


def _ffn_rows(x, m, mod0, g_ref, wi_ref, wo_ref, fin_ref, final):
    shift, scale, gate = m[mod0:mod0 + 1], m[mod0 + 1:mod0 + 2], m[mod0 + 2:mod0 + 3]
    h = _rms(x, g_ref[...]) * (1.0 + scale) + shift
    gu = jnp.dot(h.astype(BF16), wi_ref[...], preferred_element_type=F32)
    a = _silu(gu[:, :D_FF]) * gu[:, D_FF:]
    y = jnp.dot(a.astype(BF16), wo_ref[...], preferred_element_type=F32)
    out = x + (0.5 * gate) * y
    if final:
        out = _rms(out, fin_ref[...])
    return out


def _ffn_kernel(*refs, mod0, final, split_input):
    if split_input:
        ctx_ref, x_ref, mod_ref, g_ref, wi_ref, wo_ref, fin_ref, o_ref = refs
        x = jnp.where(pl.program_id(1) == 0, ctx_ref[0], x_ref[0])
    else:
        x_ref, mod_ref, g_ref, wi_ref, wo_ref, fin_ref, o_ref = refs
        x = x_ref[0]
    o_ref[0] = _ffn_rows(x, mod_ref[0], mod0, g_ref, wi_ref, wo_ref, fin_ref, final)


def _ffn(xs, w, l, which, *, latent_only=False, final=False):
    split_input = isinstance(xs, tuple)
    t0 = 1 if latent_only else 0
    tm = TM_LATENT if latent_only else TM
    nt = SEQ // tm if latent_only else T // TM
    if split_input:
        x_specs = [pl.BlockSpec((1, TM, D), lambda b, t: (b, 0, 0)),
                   pl.BlockSpec((1, TM, D), lambda b, t: (b, jnp.maximum(t - 1, 0), 0))]
        xs_args = list(xs)
    else:
        x_specs = [pl.BlockSpec((1, tm, D), lambda b, t: (b, t, 0))]
        xs_args = [xs]
    return pl.pallas_call(
        functools.partial(_ffn_kernel, mod0=6 * which, final=final, split_input=split_input),
        grid=(BATCH, nt),
        in_specs=x_specs + [
            _mod_spec(l, t0),
            _const_spec((1, D), 3 * l + 2 * which),
            _const_spec((D, 2 * D_FF), l),
            _const_spec((D_FF, D), l),
            _const_spec((1, D)),
        ],
        out_specs=pl.BlockSpec((1, tm, D), lambda b, t: (b, t, 0)),
        out_shape=jax.ShapeDtypeStruct((BATCH, nt * tm, D), F32),
        compiler_params=_params(("parallel", "parallel")),
        name="ffn",
    )(*xs_args, w["mods"], w["norm_g"], w["ffn_w_in"][which], w["ffn_w_out"][which], w["final_norm"])


def _proj_kernel(x_ref, xprev_ref, xnext_ref, mod_ref, g_ref, wkv_ref, wq_side_ref, cw_ref, cb_ref, kvn_ref, wkn_ref,
                 wvt_ref, one_ref, qn_ref, wq_ref, wqr_ref, cos_ref, sin_ref,
                 q_out, k_out, vt_out, xc_out, dt_out, dtt_out, z_out, uv_out, gate_out, *, ctx_queries):
    t = pl.program_id(1)
    m = mod_ref[0]
    xe = jnp.concatenate([xprev_ref[0], x_ref[0], xnext_ref[0]], axis=0)
    he = _rms(xe, g_ref[...]) * (1.0 + m[4:5]) + m[3:4]
    hb = he[HALO:HALO + TM].astype(BF16)
    he = he.astype(BF16)
    cos = cos_ref[...]
    sin = sin_ref[...]

    small = jnp.dot(hb, wkv_ref[:, C_KV:C_XBC], preferred_element_type=F32)

    xbc = jnp.dot(he, wkv_ref[:, C_XBC:C_Q], preferred_element_type=F32)
    row = lax.broadcasted_iota(jnp.int32, (TM + 2 * HALO, 1), 0)
    has_prev = t >= 2
    has_next = (t >= 1) & (t < T // TM - 1)
    keep = ((row >= HALO) | has_prev) & ((row < HALO + TM) | has_next)
    def conv_lane_tile(c0):
        for r0 in range(0, TM, CONV_ROWS):
            win = xbc[r0:r0 + CONV_ROWS + 2 * HALO, c0:c0 + 128]
            if r0 == 0 or r0 + CONV_ROWS == TM:
                win = jnp.where(keep[r0:r0 + CONV_ROWS + 2 * HALO], win, 0.0)
            acc = cb_ref[:, c0:c0 + 128]
            for k in range(SSM_CONV):
                shift = (SSM_CONV // 2 - k) % (CONV_ROWS + 2 * HALO)
                tap = win if shift == 0 else pltpu.roll(win, shift, axis=0)
                acc = acc + cw_ref[k:k + 1, c0:c0 + 128] * tap[HALO:HALO + CONV_ROWS]
            xc_out[0, r0:r0 + CONV_ROWS, c0:c0 + 128] = _silu(acc)

    def query_side():
        def proj(c0, c1):
            return jnp.dot(hb, wq_side_ref[:, c0 - C_Q:c1 - C_Q], preferred_element_type=F32)

        q_lat = proj(C_Q, C_Z)
        conv_tiles = list(range(0, XBC, 128))
        for out, base, width in ((z_out, C_Z, SSM_INNER), (uv_out, C_UV, 2 * GM_WIDTH), (gate_out, C_GATE, N_BRANCH * D)):
            for c0 in range(0, width, PROJ_PIECE):
                out[0, :, c0:c0 + PROJ_PIECE] = proj(base + c0, base + c0 + PROJ_PIECE).astype(out.dtype)
                if conv_tiles:
                    conv_lane_tile(conv_tiles.pop(0))
        assert not conv_tiles
        qn = _rms(q_lat, qn_ref[...]).astype(BF16)
        qa = jnp.dot(qn, wq_ref[...], preferred_element_type=F32)
        qb = jnp.dot(qn, wqr_ref[...], preferred_element_type=F32)
        for hd in range(HEADS):
            sl = slice(hd * HEAD_PAD, (hd + 1) * HEAD_PAD)
            q_out[0, :, sl] = (qa[:, sl] * cos + qb[:, sl] * sin).astype(BF16)

    def query_side_unused():
        for out in (q_out, z_out, uv_out, gate_out):
            out[...] = jnp.zeros_like(out)
        for c0 in range(0, XBC, 128):
            conv_lane_tile(c0)

    if ctx_queries:
        query_side()
    else:
        pl.when(t != 0)(query_side)
        pl.when(t == 0)(query_side_unused)

    dt = small[:, C_DT:C_DT + 128]
    dt_out[0] = dt
    dtt_out[0] = dt.T[:2 * SSM_HEADS]

    kvn = _rms(small[:, C_KV:C_KV + 128], kvn_ref[...]).astype(BF16)
    kn = jnp.dot(kvn, wkn_ref[...], preferred_element_type=F32)
    kr = small[:, C_KR:C_KR + 128] * cos + small[:, C_KRR:C_KRR + 128] * sin
    for hd in range(HEADS):
        sl = slice(hd * HEAD_PAD, (hd + 1) * HEAD_PAD)
        k_out[0, :, sl] = (kn[:, sl] + kr).astype(BF16)
    vt = lax.dot_general(wvt_ref[...], kvn, (((1,), (1,)), ((), ())), preferred_element_type=F32)
    vt_out[0] = (vt + one_ref[...]).astype(BF16)


def _proj(xs, w, l, *, ctx_queries):
    nt = T // TM

    def rows(w, dt):
        return jax.ShapeDtypeStruct((BATCH, T, w), dt), _row_spec(w, 0)

    def cols(r, dt):
        return jax.ShapeDtypeStruct((BATCH, r, T), dt), pl.BlockSpec((1, r, TM), lambda b, t: (b, 0, t))

    outs = [rows(HEADS * HEAD_PAD, BF16), rows(HEADS * HEAD_PAD, BF16), cols(HEADS * V_ROWS, BF16),
            rows(XBC, F32), rows(128, F32), cols(2 * SSM_HEADS, F32),
            rows(SSM_INNER, BF16), rows(2 * GM_WIDTH, BF16), rows(N_BRANCH * D, F32)]
    out_shape = [o[0] for o in outs]
    out_specs = [o[1] for o in outs]
    return pl.pallas_call(
        functools.partial(_proj_kernel, ctx_queries=ctx_queries),
        grid=(BATCH, nt),
        in_specs=[
            _row_spec(D, 0),
            pl.BlockSpec((1, HALO, D), lambda b, t: (b, jnp.maximum(t * (TM // HALO) - 1, 0), 0)),
            pl.BlockSpec((1, HALO, D), lambda b, t: (b, jnp.minimum((t + 1) * (TM // HALO), T // HALO - 1), 0)),
            _mod_spec(l, 0),
            _const_spec((1, D), 3 * l + 1),
            _const_spec((D, C_Q), l),
            _const_spec((D, C_END - C_Q), l),
            _const_spec((8, XBC), l),
            _const_spec((1, XBC), l),
            _const_spec((1, KV_RANK), l),
            _const_spec((KV_RANK, HEADS * HEAD_PAD), l),
            _const_spec((HEADS * V_ROWS, KV_RANK), l),
            _const_spec((HEADS * V_ROWS, 1)),
            _const_spec((1, Q_RANK), l),
            _const_spec((Q_RANK, HEADS * HEAD_PAD), l),
            _const_spec((Q_RANK, HEADS * HEAD_PAD), l),
            pl.BlockSpec((TM, HEAD_PAD), lambda b, t: (t, 0)),
            pl.BlockSpec((TM, HEAD_PAD), lambda b, t: (t, 0)),
        ],
        out_specs=out_specs,
        out_shape=out_shape,
        compiler_params=_params(("parallel", "parallel")),
        name="mixer_proj",
    )(xs, xs, xs, w["mods"], w["norm_g"], w["w_in_kv"], w["w_in_q"], w["conv_w"], w["conv_b"], w["kv_norm"], w["w_kn"],
      w["w_v_t"],
      w["v_ones"],
      w["q_norm"], w["w_q"], w["w_q_rot"], w["cos"], w["sin"])


def _attn_kernel(q_ref, k_ref, vt_ref, o_ref, *, t0):
    def run(nk):
        nblk = nk // KEY_BLOCK

        def score_block(hd, j):
            q = q_ref[0, :, hd * HEAD_PAD:(hd + 1) * HEAD_PAD]
            k = k_ref[0, j * KEY_BLOCK:(j + 1) * KEY_BLOCK, hd * HEAD_PAD:(hd + 1) * HEAD_PAD]
            return lax.dot_general(k, q, (((1,), (1,)), ((), ())), preferred_element_type=F32)

        def col_max(blocks):
            mx = functools.reduce(jnp.maximum, blocks)
            mx = jnp.max(mx.reshape(KEY_BLOCK // 8, 8, TM), axis=0)
            return jnp.max(mx, axis=0, keepdims=True)

        outs = []
        cur = [score_block(0, j) for j in range(nblk)]
        for hd in range(HEADS):
            mx = col_max(cur)
            nxt, ps = [], []
            for j in range(nblk):
                if hd + 1 < HEADS:
                    nxt.append(score_block(hd + 1, j))
                ps.append(jnp.exp2((cur[j] - mx) * EXP2_SCALE).astype(BF16))
            p = jnp.concatenate(ps, axis=0)
            ot = jnp.dot(vt_ref[0, hd * V_ROWS:(hd + 1) * V_ROWS, :nk], p, preferred_element_type=F32)
            outs.append(ot[:VDIM] / ot[VDIM:VDIM + 1])
            cur = nxt
        o_ref[0] = jnp.concatenate(outs, axis=0).T.astype(BF16)

    if t0 == 0:
        t = pl.program_id(1)
        pl.when(t == 0)(lambda: run(CTX))
        pl.when(t != 0)(lambda: run(T))
    else:
        run(T)


def _attn(q, k, v, *, t0):
    nt = T // TM - t0
    return pl.pallas_call(
        functools.partial(_attn_kernel, t0=t0),
        grid=(BATCH, nt),
        in_specs=[
            _row_spec(HEADS * HEAD_PAD, t0),
            pl.BlockSpec((1, T, HEADS * HEAD_PAD), lambda b, t: (b, 0, 0)),
            pl.BlockSpec((1, HEADS * V_ROWS, T), lambda b, t: (b, 0, 0)),
        ],
        out_specs=pl.BlockSpec((1, TM, HEADS * VDIM), lambda b, t: (b, t, 0)),
        out_shape=jax.ShapeDtypeStruct((BATCH, nt * TM, HEADS * VDIM), BF16),
        compiler_params=_params(("parallel", "arbitrary")),
        name="mla_attn",
    )(q, k, v)


def _ssd_kernel(xc_ref, dt_ref, dtt_ref, bias_row_ref, bias_col_ref, alog_row_ref, alog_col_ref, dfull_ref,
                y_ref, h_ref):
    row = lax.broadcasted_iota(jnp.int32, (CHUNK, CHUNK), 0)
    col = lax.broadcasted_iota(jnp.int32, (CHUNK, CHUNK), 1)
    lower = (col <= row)
    upper = (col >= row)
    lower01 = jnp.where(lower, 1.0, 0.0).astype(BF16)
    upper01 = jnp.where(upper, 1.0, 0.0).astype(BF16)
    lo_half = lax.broadcasted_iota(jnp.int32, (CHUNK, 128), 1) < SSM_P
    a_row = -jnp.exp(alog_row_ref[...])
    a_col = -jnp.exp(alog_col_ref[...])

    def chunk_step(c, d):
        r0 = pl.multiple_of(c * CHUNK, CHUNK)
        xc = xc_ref[0, pl.ds(r0, CHUNK), :]
        dcol = jax.nn.softplus(dt_ref[0, pl.ds(r0, CHUNK), :] + bias_row_ref[...])
        drow = jax.nn.softplus(dtt_ref[0, :, pl.ds(r0, CHUNK)] + bias_col_ref[...])
        tri_l, tri_r, mask = (lower01, upper01, lower) if d == 0 else (upper01, lower01, upper)
        acum_c = _dot_left01(tri_l, dcol * a_row)
        acum_r = _dot_right01(drow * a_col, tri_r)
        atot_c = acum_c[CHUNK - 1:CHUNK] if d == 0 else acum_c[0:1]
        w_c = jnp.exp(atot_c - acum_c) * dcol
        etot_c = jnp.exp(atot_c)

        def head_col(arr, hd):
            lane = d * SSM_HEADS + hd
            return arr[:, lane:lane + 1]

        def pair_lanes(arr, p):
            return jnp.where(lo_half[:arr.shape[0]], head_col(arr, 2 * p), head_col(arr, 2 * p + 1))

        ys = []
        for g in range(SSM_GROUPS):
            bm = xc[:, SSM_INNER + g * SSM_N:SSM_INNER + (g + 1) * SSM_N]
            cm = xc[:, SSM_INNER + (SSM_GROUPS + g) * SSM_N:SSM_INNER + (SSM_GROUPS + g + 1) * SSM_N]
            bmb = bm.astype(BF16)
            cmb = cm.astype(BF16)
            cb = lax.dot_general(cmb, bmb, (((1,), (1,)), ((), ())), preferred_element_type=F32)
            bt = bm.T.astype(BF16)
            for pp in range(2):
                p = g * 2 + pp
                sc, a_bc = [], []
                for hh in range(2):
                    hd = 2 * p + hh
                    a_bc.append(jnp.broadcast_to(head_col(acum_c, hd), (CHUNK, CHUNK)))
                    diff = a_bc[hh] - acum_r[d * SSM_HEADS + hd:d * SSM_HEADS + hd + 1]
                    dec = jnp.exp(jnp.where(mask, diff, -jnp.inf))
                    sc.append(cb * dec * drow[d * SSM_HEADS + hd:d * SSM_HEADS + hd + 1])
                x_p = xc[:, p * 128:(p + 1) * 128]
                xw = (x_p * pair_lanes(w_c, p)).astype(BF16)
                s_t = jnp.dot(bt, xw, preferred_element_type=F32)
                h_prev = h_ref[d, p]
                e_p = jnp.exp(jnp.where(lo_half, a_bc[0], a_bc[1]))
                y_inter = jnp.dot(cmb, h_prev.astype(BF16), preferred_element_type=F32) * e_p
                pmat = jnp.concatenate(sc, axis=1).astype(BF16)
                xbd = jnp.concatenate([jnp.where(lo_half, x_p, 0.0), jnp.where(lo_half, 0.0, x_p)],
                                      axis=0).astype(BF16)
                ys.append(jnp.dot(pmat, xbd, preferred_element_type=F32) + y_inter)
                h_ref[d, p] = h_prev * pair_lanes(etot_c, p) + s_t
        y = jnp.concatenate(ys, axis=1)
        if d == 0:
            y = y + dfull_ref[...] * xc[:, :SSM_INNER]
        y_ref[0, pl.ds(r0, CHUNK), :] += y

    h_ref[...] = jnp.zeros_like(h_ref)
    y_ref[...] = jnp.zeros_like(y_ref)

    def scan_body(i, carry):
        chunk_step(i, 0)
        chunk_step(jnp.where(i < CTX_CHUNKS, CTX_CHUNKS - 1 - i, N_CHUNKS + CTX_CHUNKS - 1 - i), 1)
        return carry

    lax.fori_loop(0, N_CHUNKS, scan_body, 0, unroll=2)


def _ssd(xc, dt, dtt, w, l):
    return pl.pallas_call(
        _ssd_kernel,
        grid=(BATCH,),
        in_specs=[
            pl.BlockSpec((1, T, XBC), lambda b: (b, 0, 0)),
            pl.BlockSpec((1, T, 128), lambda b: (b, 0, 0)),
            pl.BlockSpec((1, 2 * SSM_HEADS, T), lambda b: (b, 0, 0)),
            _const_spec((1, 128), l),
            _const_spec((2 * SSM_HEADS, 1), l),
            _const_spec((1, 128), l),
            _const_spec((2 * SSM_HEADS, 1), l),
            _const_spec((1, SSM_INNER), l),
        ],
        out_specs=pl.BlockSpec((1, T, SSM_INNER), lambda b: (b, 0, 0)),
        out_shape=jax.ShapeDtypeStruct((BATCH, T, SSM_INNER), F32),
        scratch_shapes=[
            pltpu.VMEM((2, SSM_HEADS // 2, SSM_N, 2 * SSM_P), F32),
        ],
        compiler_params=_params(("arbitrary",)),
        name="ssd_bidir",
    )(xc, dt, dtt, w["dt_bias_row"], w["dt_bias_col"], w["a_log_row"], w["a_log_col"], w["d_full"])


def _mixout_kernel(x_ref, mod_ref, a_ref, y_ref, z_ref, uv_ref, gate_ref,
                   wmo_ref, sn_ref, wso_ref, gn_ref, ws_ref, bs_ref, wgo_ref, bg_ref, wout_ref,
                   g2_ref, wi_ref, wo_ref, fin_ref, o_ref, *, final):
    x = x_ref[0]
    m = mod_ref[0]
    o_mla = jnp.dot(a_ref[0], wmo_ref[...], preferred_element_type=F32)

    ys = y_ref[0] * _silu(z_ref[0].astype(F32))
    o_ssm = _bdot(_rms(ys, sn_ref[...]), wso_ref[...])

    uv = jax.nn.gelu(uv_ref[0].astype(F32))
    u = uv[:, :GM_WIDTH]
    v = uv[:, GM_WIDTH:]
    v = v - jnp.mean(v, axis=-1, keepdims=True)
    v = v * lax.rsqrt(jnp.mean(v * v, axis=-1, keepdims=True) + EPS) * gn_ref[...]
    lo_half = lax.broadcasted_iota(jnp.int32, (CHUNK, 128), 1) < GM_GDIM
    mixed = []
    for ch in range(TM // CHUNK):
        parts = []
        for p in range(GM_GROUPS // 2):
            vp = v[ch * CHUNK:(ch + 1) * CHUNK, p * 128:(p + 1) * 128]
            vbd = jnp.concatenate([jnp.where(lo_half, vp, 0.0), jnp.where(lo_half, 0.0, vp)], axis=0)
            parts.append(jnp.dot(ws_ref[p], vbd.astype(BF16), preferred_element_type=F32))
        mixed.append(jnp.concatenate(parts, axis=1) + bs_ref[...])
    mixed = jnp.concatenate(mixed, axis=0)
    o_gm = _bdot(u * mixed, wgo_ref[...])

    gts = jax.nn.sigmoid(gate_ref[0].astype(F32) + bg_ref[...])
    merged = gts[:, :D] * o_mla + gts[:, D:2 * D] * o_ssm + gts[:, 2 * D:] * o_gm
    y = _bdot(merged, wout_ref[...])
    x = x + m[5:6] * y
    o_ref[0] = _ffn_rows(x, m, 6, g2_ref, wi_ref, wo_ref, fin_ref, final)


def _mixout(xs, a, y, z, uv, gate, w, l, *, t0, final):
    nt = T // TM - t0
    return pl.pallas_call(
        functools.partial(_mixout_kernel, final=final),
        grid=(BATCH, nt),
        in_specs=[
            _row_spec(D, t0),
            _mod_spec(l, t0),
            pl.BlockSpec((1, TM, HEADS * VDIM), lambda b, t: (b, t, 0)),
            _row_spec(SSM_INNER, t0),
            _row_spec(SSM_INNER, t0),
            _row_spec(2 * GM_WIDTH, t0),
            _row_spec(N_BRANCH * D, t0),
            _const_spec((HEADS * VDIM, D), l),
            _const_spec((1, SSM_INNER), l),
            _const_spec((SSM_INNER, D), l),
            _const_spec((1, GM_WIDTH), l),
            _const_spec((GM_GROUPS // 2, CHUNK, 2 * CHUNK), l),
            _const_spec((CHUNK, GM_WIDTH), l),
            _const_spec((GM_WIDTH, D), l),
            _const_spec((1, N_BRANCH * D), l),
            _const_spec((D, D), l),
            _const_spec((1, D), 3 * l + 2),
            _const_spec((D, 2 * D_FF), l),
            _const_spec((D_FF, D), l),
            _const_spec((1, D)),
        ],
        out_specs=pl.BlockSpec((1, TM, D), lambda b, t: (b, t, 0)),
        out_shape=jax.ShapeDtypeStruct((BATCH, nt * TM, D), F32),
        compiler_params=_params(("parallel", "parallel")),
        name="mixer_out_ffn",
    )(xs, w["mods"], a, y, z, uv, gate, w["w_mla_o"], w["ssm_norm"], w["w_ssm_o"], w["gm_norm"], w["w_s"],
      w["b_s"], w["w_gm_o"], w["b_gate"], w["w_out"], w["norm_g"], w["ffn_w_in"][1], w["ffn_w_out"][1],
      w["final_norm"])


def _rot_half(w):
    half = ROPE // 2
    return jnp.concatenate([-w[..., half:], w[..., :half]], axis=-1)


def _head_pad(nope, rope):
    pad = jnp.zeros(nope.shape[:-1] + (HEAD_PAD - NOPE - ROPE,), nope.dtype)
    out = jnp.concatenate([nope, rope, pad], axis=-1)
    return out.reshape(out.shape[:-2] + (HEADS * HEAD_PAD,))


def _prep_weights(w_in, mla_q_norm, mla_w_uq, mla_kv_norm, mla_w_ukv, mla_w_o, ssm_conv_w, ssm_conv_b,
                  ssm_a_log, ssm_dt_bias, ssm_d, ssm_norm, ssm_w_o, gm_norm, gm_w_s, gm_b_s, gm_w_o, b_gate, w_out):
    w = w_in
    o = 0
    kv = w[..., o:o + KV_RANK]; o += KV_RANK
    kr = w[..., o:o + ROPE]; o += ROPE
    xbc = w[..., o:o + XBC]; o += XBC
    dtw = w[..., o:o + 2 * SSM_HEADS]; o += 2 * SSM_HEADS
    rest = w[..., o:]

    def lanes(cols, start):
        return jnp.pad(cols, ((0, 0), (0, 0), (start, 128 - start - cols.shape[-1])))

    pieces = [kv, lanes(kr, NOPE), lanes(_rot_half(kr), NOPE), lanes(dtw, 0), xbc]
    w_kv_side = jnp.concatenate([p.astype(BF16) for p in pieces], axis=-1)

    uq = mla_w_uq.astype(BF16).reshape(DEPTH, Q_RANK, HEADS, NOPE + ROPE)
    uq_n, uq_r = uq[..., :NOPE], uq[..., NOPE:]
    ukv = mla_w_ukv.astype(BF16).reshape(DEPTH, KV_RANK, HEADS, NOPE + VDIM)
    kn = _head_pad(ukv[..., :NOPE], jnp.zeros((DEPTH, KV_RANK, HEADS, ROPE), BF16))
    v_t = jnp.pad(jnp.transpose(ukv[..., NOPE:], (0, 2, 3, 1)), ((0, 0), (0, 0), (0, V_ROWS - VDIM), (0, 0)))
    v_ones = jnp.tile((jnp.arange(V_ROWS) == VDIM).astype(F32), HEADS).reshape(HEADS * V_ROWS, 1)

    ws = gm_w_s.astype(BF16)
    ws_pair = jnp.concatenate([ws[:, 0::2], ws[:, 1::2]], axis=3)

    def row128(vals, fill=0.0):
        vals = vals.reshape(DEPTH, 1, -1)
        return jnp.pad(vals, ((0, 0), (0, 0), (0, 128 - vals.shape[-1])), constant_values=fill)

    return {
        "w_in_kv": w_kv_side,
        "w_in_q": rest.astype(BF16),
        "kv_norm": mla_kv_norm.reshape(DEPTH, 1, KV_RANK),
        "w_kn": kn,
        "w_v_t": v_t.reshape(DEPTH, HEADS * V_ROWS, KV_RANK),
        "v_ones": v_ones,
        "q_norm": mla_q_norm.reshape(DEPTH, 1, Q_RANK),
        "w_q": _head_pad(uq_n, uq_r),
        "w_q_rot": _head_pad(jnp.zeros_like(uq_n), _rot_half(uq_r)),
        "conv_w": jnp.pad(jnp.swapaxes(ssm_conv_w, 1, 2), ((0, 0), (0, 8 - SSM_CONV), (0, 0))),
        "conv_b": ssm_conv_b.reshape(DEPTH, 1, XBC),
        "dt_bias_row": row128(ssm_dt_bias),
        "dt_bias_col": ssm_dt_bias.reshape(DEPTH, 2 * SSM_HEADS, 1),
        "a_log_row": row128(ssm_a_log, fill=-80.0),
        "a_log_col": ssm_a_log.reshape(DEPTH, 2 * SSM_HEADS, 1),
        "d_full": jnp.repeat(ssm_d, SSM_P, axis=1).reshape(DEPTH, 1, SSM_INNER),
        "w_mla_o": mla_w_o.astype(BF16),
        "ssm_norm": ssm_norm.reshape(DEPTH, 1, SSM_INNER),
        "w_ssm_o": ssm_w_o.astype(BF16),
        "gm_norm": gm_norm.reshape(DEPTH, 1, GM_WIDTH),
        "w_s": ws_pair,
        "b_s": jnp.repeat(jnp.swapaxes(gm_b_s, 1, 2), GM_GDIM, axis=2),
        "w_gm_o": gm_w_o.astype(BF16),
        "b_gate": b_gate.reshape(DEPTH, 1, N_BRANCH * D),
        "w_out": w_out.astype(BF16),
    }


def _rope_tables():
    rows = SEQ // GRID_W
    r = jnp.repeat(jnp.arange(rows, dtype=F32), GRID_W)
    c = jnp.tile(jnp.arange(GRID_W, dtype=F32), rows)
    n_freq = ROPE // 4
    inv = jnp.power(ROPE_BASE, -jnp.arange(n_freq, dtype=F32) / n_freq)
    ang = jnp.concatenate([r[:, None] * inv, c[:, None] * inv], axis=-1)
    cos, sin = jnp.cos(ang), jnp.sin(ang)
    cos = jnp.concatenate([jnp.ones((CTX, ROPE // 2), F32), cos], axis=0)
    sin = jnp.concatenate([jnp.zeros((CTX, ROPE // 2), F32), sin], axis=0)
    ones = jnp.ones((T, NOPE), F32)
    zeros_n = jnp.zeros((T, NOPE), F32)
    zeros_p = jnp.zeros((T, HEAD_PAD - NOPE - ROPE), F32)
    cos_t = jnp.concatenate([ones, cos, cos, zeros_p], axis=1)
    sin_t = jnp.concatenate([zeros_n, sin, sin, zeros_p], axis=1)
    return cos_t, sin_t


def kernel(x, c, ctx, c_ctx, w_ada, b_ada, norm_g, ffn1_w_in, ffn1_w_out, ffn2_w_in, ffn2_w_out, w_in, mla_q_norm,
           mla_w_uq, mla_kv_norm, mla_w_ukv, mla_w_o, ssm_conv_w, ssm_conv_b, ssm_a_log, ssm_dt_bias, ssm_d,
           ssm_norm, ssm_w_o, gm_norm, gm_w_s, gm_b_s, gm_w_o, b_gate, w_out, final_norm):
    c_all = jnp.concatenate([c, c_ctx[None, :], jnp.zeros((MOD_ROWS - BATCH - 1, D), F32)], axis=0)
    w = _prep_weights(w_in, mla_q_norm, mla_w_uq, mla_kv_norm, mla_w_ukv, mla_w_o, ssm_conv_w, ssm_conv_b, ssm_a_log,
                      ssm_dt_bias, ssm_d, ssm_norm, ssm_w_o, gm_norm, gm_w_s, gm_b_s, gm_w_o, b_gate, w_out)
    w["mods"] = _ada(c_all, w_ada, b_ada).reshape(DEPTH, MOD_ROWS, N_MOD, D)
    w["norm_g"] = norm_g.reshape(DEPTH * 3, 1, D)
    w["ffn_w_in"] = (ffn1_w_in.astype(BF16), ffn2_w_in.astype(BF16))
    w["ffn_w_out"] = (ffn1_w_out.astype(BF16), ffn2_w_out.astype(BF16))
    w["final_norm"] = final_norm.reshape(1, D)
    w["cos"], w["sin"] = _rope_tables()

    xs = (ctx, x)
    for l in range(DEPTH):
        last = l == DEPTH - 1
        t0 = 1 if last else 0
        xs = _ffn(xs, w, l, 0)
        q, k, vt, xbc, dt, dtt, z, uv, gate = _proj(xs, w, l, ctx_queries=not last)
        a = _attn(q, k, vt, t0=t0)
        y = _ssd(xbc, dt, dtt, w, l)
        xs = _mixout(xs, a, y, z, uv, gate, w, l, t0=t0, final=last)
    return xs
```

```python
import functools
import math

import jax
import jax.numpy as jnp
from jax import lax
from jax.experimental import pallas as pl
from jax.experimental.pallas import tpu as pltpu

F32 = jnp.float32
BF16 = jnp.bfloat16

D = 1024
BATCH = 8
SEQ = 2048
DEPTH = 2
CTX = 256
T = CTX + SEQ
GRID_W = 64
EPS = 1e-6

HEADS = 8
NOPE = 64
ROPE = 32
VDIM = 64
Q_RANK = 256
KV_RANK = 128
ROPE_BASE = 10000.0
ATTN_SCALE = (NOPE + ROPE) ** -0.5
EXP2_SCALE = ATTN_SCALE * math.log2(math.e)
HEAD_PAD = 128
V_ROWS = 80

SSM_HEADS = 8
SSM_P = 64
SSM_INNER = SSM_HEADS * SSM_P
SSM_GROUPS = 2
SSM_N = 128
SSM_CONV = 5
CHUNK = 128
XBC = SSM_INNER + 2 * SSM_GROUPS * SSM_N
N_CHUNKS = T // CHUNK
CTX_CHUNKS = CTX // CHUNK
HALO = 8

GM_GROUPS = 8
GM_WIDTH = 512
GM_GDIM = GM_WIDTH // GM_GROUPS

D_FF = 2816
N_BRANCH = 3
N_MOD = 9
MOD_ROWS = 16
CTX_MOD_ROW = BATCH

C_KV = 0
C_KR = 128
C_KRR = 256
C_DT = 384
C_XBC = 512
C_Q = C_XBC + XBC
C_Z = C_Q + Q_RANK
C_UV = C_Z + SSM_INNER
C_GATE = C_UV + 2 * GM_WIDTH
C_END = C_GATE + N_BRANCH * D

TM = 256
TM_LATENT = 512
KEY_BLOCK = 128
CONV_ROWS = 64
PROJ_PIECE = 512
VMEM_LIMIT = 56 * 1024 * 1024


def _rms(x, g):
    y = x * lax.rsqrt(jnp.mean(x * x, axis=-1, keepdims=True) + EPS)
    return y * g


def _silu(x):
    return x * jax.nn.sigmoid(x)


def _bdot(a, b):
    return jnp.dot(a.astype(BF16), b.astype(BF16), preferred_element_type=F32)


def _split3(a):
    a1 = a.astype(BF16)
    r1 = a - a1.astype(F32)
    a2 = r1.astype(BF16)
    a3 = (r1 - a2.astype(F32)).astype(BF16)
    return a1, a2, a3


def _dot_right01(a, m01):
    return sum(jnp.dot(p, m01, preferred_element_type=F32) for p in _split3(a))


def _dot_left01(m01, a):
    return sum(jnp.dot(m01, p, preferred_element_type=F32) for p in _split3(a))


def _const_spec(shape, l=None):
    nd = len(shape)
    if l is None:
        return pl.BlockSpec(shape, lambda *_: (0,) * nd, pipeline_mode=pl.Buffered(1))
    return pl.BlockSpec((None,) + shape, lambda *_: (l,) + (0,) * nd, pipeline_mode=pl.Buffered(1))


def _mod_spec(l, t0):
    return pl.BlockSpec((None, 1, N_MOD, D), lambda b, t: (l, jnp.where(t + t0 == 0, CTX_MOD_ROW, b), 0, 0))


def _row_spec(width, t0):
    return pl.BlockSpec((1, TM, width), lambda b, t: (b, t + t0, 0))


def _params(sem):
    return pltpu.CompilerParams(dimension_semantics=sem, vmem_limit_bytes=VMEM_LIMIT)


def _ada_kernel(c_ref, w_ref, b_ref, o_ref):
    s = _silu(c_ref[...])
    o_ref[0] = _bdot(s, w_ref[0]) + b_ref[0]


def _ada(c_all, w_ada, b_ada):
    tn = 1152
    nn = N_MOD * D // tn
    return pl.pallas_call(
        _ada_kernel,
        grid=(DEPTH, nn),
        in_specs=[
            pl.BlockSpec((MOD_ROWS, D), lambda l, j: (0, 0)),
            pl.BlockSpec((1, D, tn), lambda l, j: (l, 0, j)),
            pl.BlockSpec((1, 1, tn), lambda l, j: (l, 0, j)),
        ],
        out_specs=pl.BlockSpec((1, MOD_ROWS, tn), lambda l, j: (l, 0, j)),
        out_shape=jax.ShapeDtypeStruct((DEPTH, MOD_ROWS, N_MOD * D), F32),
        compiler_params=_params(("arbitrary", "arbitrary")),
        name="ada_mod",
    )(c_all, w_ada, b_ada.reshape(DEPTH, 1, N_MOD * D))


def _ffn_rows(x, m, mod0, g_ref, wi_ref, wo_ref, fin_ref, final):
    shift, scale, gate = m[mod0:mod0 + 1], m[mod0 + 1:mod0 + 2], m[mod0 + 2:mod0 + 3]
    h = _rms(x, g_ref[...]) * (1.0 + scale) + shift
    gu = jnp.dot(h.astype(BF16), wi_ref[...], preferred_element_type=F32)
    a = _silu(gu[:, :D_FF]) * gu[:, D_FF:]
    y = jnp.dot(a.astype(BF16), wo_ref[...], preferred_element_type=F32)
    out = x + (0.5 * gate) * y
    if final:
        out = _rms(out, fin_ref[...])
    return out


def _ffn_kernel(*refs, mod0, final, split_input):
    if split_input:
        ctx_ref, x_ref, mod_ref, g_ref, wi_ref, wo_ref, fin_ref, o_ref = refs
        x = jnp.where(pl.program_id(1) == 0, ctx_ref[0], x_ref[0])
    else:
        x_ref, mod_ref, g_ref, wi_ref, wo_ref, fin_ref, o_ref = refs
        x = x_ref[0]
    o_ref[0] = _ffn_rows(x, mod_ref[0], mod0, g_ref, wi_ref, wo_ref, fin_ref, final)


def _ffn(xs, w, l, which, *, latent_only=False, final=False):
    split_input = isinstance(xs, tuple)
    t0 = 1 if latent_only else 0
    tm = TM_LATENT if latent_only else TM
    nt = SEQ // tm if latent_only else T // TM
    if split_input:
        x_specs = [pl.BlockSpec((1, TM, D), lambda b, t: (b, 0, 0)),
                   pl.BlockSpec((1, TM, D), lambda b, t: (b, jnp.maximum(t - 1, 0), 0))]
        xs_args = list(xs)
    else:
        x_specs = [pl.BlockSpec((1, tm, D), lambda b, t: (b, t, 0))]
        xs_args = [xs]
    return pl.pallas_call(
        functools.partial(_ffn_kernel, mod0=6 * which, final=final, split_input=split_input),
        grid=(BATCH, nt),
        in_specs=x_specs + [
            _mod_spec(l, t0),
            _const_spec((1, D), 3 * l + 2 * which),
            _const_spec((D, 2 * D_FF), l),
            _const_spec((D_FF, D), l),
            _const_spec((1, D)),
        ],
        out_specs=pl.BlockSpec((1, tm, D), lambda b, t: (b, t, 0)),
        out_shape=jax.ShapeDtypeStruct((BATCH, nt * tm, D), F32),
        compiler_params=_params(("parallel", "parallel")),
        name="ffn",
    )(*xs_args, w["mods"], w["norm_g"], w["ffn_w_in"][which], w["ffn_w_out"][which], w["final_norm"])


def _proj_kernel(x_ref, xprev_ref, xnext_ref, mod_ref, g_ref, wkv_ref, wq_side_ref, cw_ref, cb_ref, kvn_ref, wkn_ref,
                 wvt_ref, one_ref, qn_ref, wq_ref, wqr_ref, cos_ref, sin_ref,
                 q_out, k_out, vt_out, xc_out, dt_out, dtt_out, z_out, uv_out, gate_out, *, ctx_queries):
    t = pl.program_id(1)
    m = mod_ref[0]
    xe = jnp.concatenate([xprev_ref[0], x_ref[0], xnext_ref[0]], axis=0)
    he = _rms(xe, g_ref[...]) * (1.0 + m[4:5]) + m[3:4]
    hb = he[HALO:HALO + TM].astype(BF16)
    he = he.astype(BF16)
    cos = cos_ref[...]
    sin = sin_ref[...]

    small = jnp.dot(hb, wkv_ref[:, C_KV:C_XBC], preferred_element_type=F32)

    xbc = jnp.dot(he, wkv_ref[:, C_XBC:C_Q], preferred_element_type=F32)
    row = lax.broadcasted_iota(jnp.int32, (TM + 2 * HALO, 1), 0)
    has_prev = t >= 2
    has_next = (t >= 1) & (t < T // TM - 1)
    keep = ((row >= HALO) | has_prev) & ((row < HALO + TM) | has_next)
    def conv_lane_tile(c0):
        for r0 in range(0, TM, CONV_ROWS):
            win = xbc[r0:r0 + CONV_ROWS + 2 * HALO, c0:c0 + 128]
            if r0 == 0 or r0 + CONV_ROWS == TM:
                win = jnp.where(keep[r0:r0 + CONV_ROWS + 2 * HALO], win, 0.0)
            acc = cb_ref[:, c0:c0 + 128]
            for k in range(SSM_CONV):
                shift = (SSM_CONV // 2 - k) % (CONV_ROWS + 2 * HALO)
                tap = win if shift == 0 else pltpu.roll(win, shift, axis=0)
                acc = acc + cw_ref[k:k + 1, c0:c0 + 128] * tap[HALO:HALO + CONV_ROWS]
            xc_out[0, r0:r0 + CONV_ROWS, c0:c0 + 128] = _silu(acc)

    def query_side():
        def proj(c0, c1):
            return jnp.dot(hb, wq_side_ref[:, c0 - C_Q:c1 - C_Q], preferred_element_type=F32)

        q_lat = proj(C_Q, C_Z)
        conv_tiles = list(range(0, XBC, 128))
        for out, base, width in ((z_out, C_Z, SSM_INNER), (uv_out, C_UV, 2 * GM_WIDTH), (gate_out, C_GATE, N_BRANCH * D)):
            for c0 in range(0, width, PROJ_PIECE):
                out[0, :, c0:c0 + PROJ_PIECE] = proj(base + c0, base + c0 + PROJ_PIECE).astype(out.dtype)
                if conv_tiles:
                    conv_lane_tile(conv_tiles.pop(0))
        assert not conv_tiles
        qn = _rms(q_lat, qn_ref[...]).astype(BF16)
        qa = jnp.dot(qn, wq_ref[...], preferred_element_type=F32)
        qb = jnp.dot(qn, wqr_ref[...], preferred_element_type=F32)
        for hd in range(HEADS):
            sl = slice(hd * HEAD_PAD, (hd + 1) * HEAD_PAD)
            q_out[0, :, sl] = (qa[:, sl] * cos + qb[:, sl] * sin).astype(BF16)

    def query_side_unused():
        for out in (q_out, z_out, uv_out, gate_out):
            out[...] = jnp.zeros_like(out)
        for c0 in range(0, XBC, 128):
            conv_lane_tile(c0)

    if ctx_queries:
        query_side()
    else:
        pl.when(t != 0)(query_side)
        pl.when(t == 0)(query_side_unused)

    dt = small[:, C_DT:C_DT + 128]
    dt_out[0] = dt
    dtt_out[0] = dt.T[:2 * SSM_HEADS]

    kvn = _rms(small[:, C_KV:C_KV + 128], kvn_ref[...]).astype(BF16)
    kn = jnp.dot(kvn, wkn_ref[...], preferred_element_type=F32)
    kr = small[:, C_KR:C_KR + 128] * cos + small[:, C_KRR:C_KRR + 128] * sin
    for hd in range(HEADS):
        sl = slice(hd * HEAD_PAD, (hd + 1) * HEAD_PAD)
        k_out[0, :, sl] = (kn[:, sl] + kr).astype(BF16)
    vt = lax.dot_general(wvt_ref[...], kvn, (((1,), (1,)), ((), ())), preferred_element_type=F32)
    vt_out[0] = (vt + one_ref[...]).astype(BF16)


def _proj(xs, w, l, *, ctx_queries):
    nt = T // TM

    def rows(w, dt):
        return jax.ShapeDtypeStruct((BATCH, T, w), dt), _row_spec(w, 0)

    def cols(r, dt):
        return jax.ShapeDtypeStruct((BATCH, r, T), dt), pl.BlockSpec((1, r, TM), lambda b, t: (b, 0, t))

    outs = [rows(HEADS * HEAD_PAD, BF16), rows(HEADS * HEAD_PAD, BF16), cols(HEADS * V_ROWS, BF16),
            rows(XBC, F32), rows(128, F32), cols(2 * SSM_HEADS, F32),
            rows(SSM_INNER, BF16), rows(2 * GM_WIDTH, BF16), rows(N_BRANCH * D, F32)]
    out_shape = [o[0] for o in outs]
    out_specs = [o[1] for o in outs]
    return pl.pallas_call(
        functools.partial(_proj_kernel, ctx_queries=ctx_queries),
        grid=(BATCH, nt),
        in_specs=[
            _row_spec(D, 0),
            pl.BlockSpec((1, HALO, D), lambda b, t: (b, jnp.maximum(t * (TM // HALO) - 1, 0), 0)),
            pl.BlockSpec((1, HALO, D), lambda b, t: (b, jnp.minimum((t + 1) * (TM // HALO), T // HALO - 1), 0)),
            _mod_spec(l, 0),
            _const_spec((1, D), 3 * l + 1),
            _const_spec((D, C_Q), l),
            _const_spec((D, C_END - C_Q), l),
            _const_spec((8, XBC), l),
            _const_spec((1, XBC), l),
            _const_spec((1, KV_RANK), l),
            _const_spec((KV_RANK, HEADS * HEAD_PAD), l),
            _const_spec((HEADS * V_ROWS, KV_RANK), l),
            _const_spec((HEADS * V_ROWS, 1)),
            _const_spec((1, Q_RANK), l),
            _const_spec((Q_RANK, HEADS * HEAD_PAD), l),
            _const_spec((Q_RANK, HEADS * HEAD_PAD), l),
            pl.BlockSpec((TM, HEAD_PAD), lambda b, t: (t, 0)),
            pl.BlockSpec((TM, HEAD_PAD), lambda b, t: (t, 0)),
        ],
        out_specs=out_specs,
        out_shape=out_shape,
        compiler_params=_params(("parallel", "parallel")),
        name="mixer_proj",
    )(xs, xs, xs, w["mods"], w["norm_g"], w["w_in_kv"], w["w_in_q"], w["conv_w"], w["conv_b"], w["kv_norm"], w["w_kn"],
      w["w_v_t"],
      w["v_ones"],
      w["q_norm"], w["w_q"], w["w_q_rot"], w["cos"], w["sin"])


def _attn_kernel(q_ref, k_ref, vt_ref, o_ref, *, t0):
    def run(nk):
        nblk = nk // KEY_BLOCK

        def score_block(hd, j):
            q = q_ref[0, :, hd * HEAD_PAD:(hd + 1) * HEAD_PAD]
            k = k_ref[0, j * KEY_BLOCK:(j + 1) * KEY_BLOCK, hd * HEAD_PAD:(hd + 1) * HEAD_PAD]
            return lax.dot_general(k, q, (((1,), (1,)), ((), ())), preferred_element_type=F32)

        def col_max(blocks):
            mx = functools.reduce(jnp.maximum, blocks)
            mx = jnp.max(mx.reshape(KEY_BLOCK // 8, 8, TM), axis=0)
            return jnp.max(mx, axis=0, keepdims=True)

        outs = []
        cur = [score_block(0, j) for j in range(nblk)]
        for hd in range(HEADS):
            mx = col_max(cur)
            nxt, ps = [], []
            for j in range(nblk):
                if hd + 1 < HEADS:
                    nxt.append(score_block(hd + 1, j))
                ps.append(jnp.exp2((cur[j] - mx) * EXP2_SCALE).astype(BF16))
            p = jnp.concatenate(ps, axis=0)
            ot = jnp.dot(vt_ref[0, hd * V_ROWS:(hd + 1) * V_ROWS, :nk], p, preferred_element_type=F32)
            outs.append(ot[:VDIM] / ot[VDIM:VDIM + 1])
            cur = nxt
        o_ref[0] = jnp.concatenate(outs, axis=0).T.astype(BF16)

    if t0 == 0:
        t = pl.program_id(1)
        pl.when(t == 0)(lambda: run(CTX))
        pl.when(t != 0)(lambda: run(T))
    else:
        run(T)


def _attn(q, k, v, *, t0):
    nt = T // TM - t0
    return pl.pallas_call(
        functools.partial(_attn_kernel, t0=t0),
        grid=(BATCH, nt),
        in_specs=[
            _row_spec(HEADS * HEAD_PAD, t0),
            pl.BlockSpec((1, T, HEADS * HEAD_PAD), lambda b, t: (b, 0, 0)),
            pl.BlockSpec((1, HEADS * V_ROWS, T), lambda b, t: (b, 0, 0)),
        ],
        out_specs=pl.BlockSpec((1, TM, HEADS * VDIM), lambda b, t: (b, t, 0)),
        out_shape=jax.ShapeDtypeStruct((BATCH, nt * TM, HEADS * VDIM), BF16),
        compiler_params=_params(("parallel", "arbitrary")),
        name="mla_attn",
    )(q, k, v)


def _ssd_kernel(xc_ref, dt_ref, dtt_ref, bias_row_ref, bias_col_ref, alog_row_ref, alog_col_ref, dfull_ref,
                y_ref, h_ref, dcol_ref, drow_ref, acol_ref, arow_ref):
    row = lax.broadcasted_iota(jnp.int32, (CHUNK, CHUNK), 0)
    col = lax.broadcasted_iota(jnp.int32, (CHUNK, CHUNK), 1)
    lower = (col <= row)
    upper = (col >= row)
    lower01 = jnp.where(lower, 1.0, 0.0).astype(BF16)
    upper01 = jnp.where(upper, 1.0, 0.0).astype(BF16)
    lo_half = lax.broadcasted_iota(jnp.int32, (CHUNK, 128), 1) < SSM_P

    chunks = range(N_CHUNKS)
    dcol_all = jax.nn.softplus(dt_ref[0] + bias_row_ref[...])
    drow_all = jax.nn.softplus(dtt_ref[0] + bias_col_ref[...])
    dcol_ref[...] = dcol_all
    drow_ref[...] = drow_all
    a_col_form = dcol_all * -jnp.exp(alog_row_ref[...])
    a_wide = jnp.concatenate([a_col_form[c * CHUNK:(c + 1) * CHUNK] for c in chunks], axis=1)
    causal_lane = lax.broadcasted_iota(jnp.int32, (CHUNK, N_CHUNKS * 128), 1) % 128 < SSM_HEADS
    acol_wide = jnp.where(causal_lane, _dot_left01(lower01, a_wide), _dot_left01(upper01, a_wide))
    a_row_form = drow_all * -jnp.exp(alog_col_ref[...])
    a_tall = jnp.concatenate([a_row_form[:, c * CHUNK:(c + 1) * CHUNK] for c in chunks], axis=0)
    causal_row = lax.broadcasted_iota(jnp.int32, (N_CHUNKS * 2 * SSM_HEADS, CHUNK), 0) % (2 * SSM_HEADS) < SSM_HEADS
    arow_tall = jnp.where(causal_row, _dot_right01(a_tall, upper01), _dot_right01(a_tall, lower01))
    for c in chunks:
        acol_ref[c] = acol_wide[:, c * 128:(c + 1) * 128]
        arow_ref[c] = arow_tall[c * 2 * SSM_HEADS:(c + 1) * 2 * SSM_HEADS]

    def chunk_step(c, d):
        r0 = pl.multiple_of(c * CHUNK, CHUNK)
        xc = xc_ref[0, pl.ds(r0, CHUNK), :]
        dcol = dcol_ref[pl.ds(r0, CHUNK), :]
        drow = drow_ref[:, pl.ds(r0, CHUNK)]
        mask = lower if d == 0 else upper
        acum_c = acol_ref[c]
        acum_r = arow_ref[c]
        atot_c = acum_c[CHUNK - 1:CHUNK] if d == 0 else acum_c[0:1]
        w_c = jnp.exp(atot_c - acum_c) * dcol
        etot_c = jnp.exp(atot_c)

        def head_col(arr, hd):
            lane = d * SSM_HEADS + hd
            return arr[:, lane:lane + 1]

        def pair_lanes(arr, p):
            return jnp.where(lo_half[:arr.shape[0]], head_col(arr, 2 * p), head_col(arr, 2 * p + 1))

        ys = []
        for g in range(SSM_GROUPS):
            bm = xc[:, SSM_INNER + g * SSM_N:SSM_INNER + (g + 1) * SSM_N]
            cm = xc[:, SSM_INNER + (SSM_GROUPS + g) * SSM_N:SSM_INNER + (SSM_GROUPS + g + 1) * SSM_N]
            bmb = bm.astype(BF16)
            cmb = cm.astype(BF16)
            cb = lax.dot_general(cmb, bmb, (((1,), (1,)), ((), ())), preferred_element_type=F32)
            bt = bm.T.astype(BF16)
            for pp in range(2):
                p = g * 2 + pp
                sc, a_bc = [], []
                for hh in range(2):
                    hd = 2 * p + hh
                    a_bc.append(jnp.broadcast_to(head_col(acum_c, hd), (CHUNK, CHUNK)))
                    diff = a_bc[hh] - acum_r[d * SSM_HEADS + hd:d * SSM_HEADS + hd + 1]
                    dec = jnp.exp(jnp.where(mask, diff, -jnp.inf))
                    sc.append(cb * dec * drow[d * SSM_HEADS + hd:d * SSM_HEADS + hd + 1])
                x_p = xc[:, p * 128:(p + 1) * 128]
                xw = (x_p * pair_lanes(w_c, p)).astype(BF16)
                s_t = jnp.dot(bt, xw, preferred_element_type=F32)
                h_prev = h_ref[d, p]
                e_p = jnp.exp(jnp.where(lo_half, a_bc[0], a_bc[1]))
                y_inter = jnp.dot(cmb, h_prev.astype(BF16), preferred_element_type=F32) * e_p
                pmat = jnp.concatenate(sc, axis=1).astype(BF16)
                xbd = jnp.concatenate([jnp.where(lo_half, x_p, 0.0), jnp.where(lo_half, 0.0, x_p)],
                                      axis=0).astype(BF16)
                ys.append(jnp.dot(pmat, xbd, preferred_element_type=F32) + y_inter)
                h_ref[d, p] = h_prev * pair_lanes(etot_c, p) + s_t
        y = jnp.concatenate(ys, axis=1)
        if d == 0:
            y = y + dfull_ref[...] * xc[:, :SSM_INNER]
        y_ref[0, pl.ds(r0, CHUNK), :] += y

    h_ref[...] = jnp.zeros_like(h_ref)
    y_ref[...] = jnp.zeros_like(y_ref)

    def scan_body(i, carry):
        chunk_step(i, 0)
        chunk_step(jnp.where(i < CTX_CHUNKS, CTX_CHUNKS - 1 - i, N_CHUNKS + CTX_CHUNKS - 1 - i), 1)
        return carry

    lax.fori_loop(0, N_CHUNKS, scan_body, 0, unroll=2)


def _ssd(xc, dt, dtt, w, l):
    return pl.pallas_call(
        _ssd_kernel,
        grid=(BATCH,),
        in_specs=[
            pl.BlockSpec((1, T, XBC), lambda b: (b, 0, 0)),
            pl.BlockSpec((1, T, 128), lambda b: (b, 0, 0)),
            pl.BlockSpec((1, 2 * SSM_HEADS, T), lambda b: (b, 0, 0)),
            _const_spec((1, 128), l),
            _const_spec((2 * SSM_HEADS, 1), l),
            _const_spec((1, 128), l),
            _const_spec((2 * SSM_HEADS, 1), l),
            _const_spec((1, SSM_INNER), l),
        ],
        out_specs=pl.BlockSpec((1, T, SSM_INNER), lambda b: (b, 0, 0)),
        out_shape=jax.ShapeDtypeStruct((BATCH, T, SSM_INNER), F32),
        scratch_shapes=[
            pltpu.VMEM((2, SSM_HEADS // 2, SSM_N, 2 * SSM_P), F32),
            pltpu.VMEM((T, 128), F32),
            pltpu.VMEM((2 * SSM_HEADS, T), F32),
            pltpu.VMEM((N_CHUNKS, CHUNK, 128), F32),
            pltpu.VMEM((N_CHUNKS, 2 * SSM_HEADS, CHUNK), F32),
        ],
        compiler_params=_params(("arbitrary",)),
        name="ssd_bidir",
    )(xc, dt, dtt, w["dt_bias_row"], w["dt_bias_col"], w["a_log_row"], w["a_log_col"], w["d_full"])


def _mixout_kernel(x_ref, mod_ref, a_ref, y_ref, z_ref, uv_ref, gate_ref,
                   wmo_ref, sn_ref, wso_ref, gn_ref, ws_ref, bs_ref, wgo_ref, bg_ref, wout_ref,
                   g2_ref, wi_ref, wo_ref, fin_ref, o_ref, *, final):
    x = x_ref[0]
    m = mod_ref[0]
    o_mla = jnp.dot(a_ref[0], wmo_ref[...], preferred_element_type=F32)

    ys = y_ref[0] * _silu(z_ref[0].astype(F32))
    o_ssm = _bdot(_rms(ys, sn_ref[...]), wso_ref[...])

    uv = jax.nn.gelu(uv_ref[0].astype(F32))
    u = uv[:, :GM_WIDTH]
    v = uv[:, GM_WIDTH:]
    v = v - jnp.mean(v, axis=-1, keepdims=True)
    v = v * lax.rsqrt(jnp.mean(v * v, axis=-1, keepdims=True) + EPS) * gn_ref[...]
    lo_half = lax.broadcasted_iota(jnp.int32, (CHUNK, 128), 1) < GM_GDIM
    mixed = []
    for ch in range(TM // CHUNK):
        parts = []
        for p in range(GM_GROUPS // 2):
            vp = v[ch * CHUNK:(ch + 1) * CHUNK, p * 128:(p + 1) * 128]
            vbd = jnp.concatenate([jnp.where(lo_half, vp, 0.0), jnp.where(lo_half, 0.0, vp)], axis=0)
            parts.append(jnp.dot(ws_ref[p], vbd.astype(BF16), preferred_element_type=F32))
        mixed.append(jnp.concatenate(parts, axis=1) + bs_ref[...])
    mixed = jnp.concatenate(mixed, axis=0)
    o_gm = _bdot(u * mixed, wgo_ref[...])

    gts = jax.nn.sigmoid(gate_ref[0].astype(F32) + bg_ref[...])
    merged = gts[:, :D] * o_mla + gts[:, D:2 * D] * o_ssm + gts[:, 2 * D:] * o_gm
    y = _bdot(merged, wout_ref[...])
    x = x + m[5:6] * y
    o_ref[0] = _ffn_rows(x, m, 6, g2_ref, wi_ref, wo_ref, fin_ref, final)


def _mixout(xs, a, y, z, uv, gate, w, l, *, t0, final):
    nt = T // TM - t0
    return pl.pallas_call(
        functools.partial(_mixout_kernel, final=final),
        grid=(BATCH, nt),
        in_specs=[
            _row_spec(D, t0),
            _mod_spec(l, t0),
            pl.BlockSpec((1, TM, HEADS * VDIM), lambda b, t: (b, t, 0)),
            _row_spec(SSM_INNER, t0),
            _row_spec(SSM_INNER, t0),
            _row_spec(2 * GM_WIDTH, t0),
            _row_spec(N_BRANCH * D, t0),
            _const_spec((HEADS * VDIM, D), l),
            _const_spec((1, SSM_INNER), l),
            _const_spec((SSM_INNER, D), l),
            _const_spec((1, GM_WIDTH), l),
            _const_spec((GM_GROUPS // 2, CHUNK, 2 * CHUNK), l),
            _const_spec((CHUNK, GM_WIDTH), l),
            _const_spec((GM_WIDTH, D), l),
            _const_spec((1, N_BRANCH * D), l),
            _const_spec((D, D), l),
            _const_spec((1, D), 3 * l + 2),
            _const_spec((D, 2 * D_FF), l),
            _const_spec((D_FF, D), l),
            _const_spec((1, D)),
        ],
        out_specs=pl.BlockSpec((1, TM, D), lambda b, t: (b, t, 0)),
        out_shape=jax.ShapeDtypeStruct((BATCH, nt * TM, D), F32),
        compiler_params=_params(("parallel", "parallel")),
        name="mixer_out_ffn",
    )(xs, w["mods"], a, y, z, uv, gate, w["w_mla_o"], w["ssm_norm"], w["w_ssm_o"], w["gm_norm"], w["w_s"],
      w["b_s"], w["w_gm_o"], w["b_gate"], w["w_out"], w["norm_g"], w["ffn_w_in"][1], w["ffn_w_out"][1],
      w["final_norm"])


def _rot_half(w):
    half = ROPE // 2
    return jnp.concatenate([-w[..., half:], w[..., :half]], axis=-1)


def _head_pad(nope, rope):
    pad = jnp.zeros(nope.shape[:-1] + (HEAD_PAD - NOPE - ROPE,), nope.dtype)
    out = jnp.concatenate([nope, rope, pad], axis=-1)
    return out.reshape(out.shape[:-2] + (HEADS * HEAD_PAD,))


def _prep_weights(w_in, mla_q_norm, mla_w_uq, mla_kv_norm, mla_w_ukv, mla_w_o, ssm_conv_w, ssm_conv_b,
                  ssm_a_log, ssm_dt_bias, ssm_d, ssm_norm, ssm_w_o, gm_norm, gm_w_s, gm_b_s, gm_w_o, b_gate, w_out):
    w = w_in
    o = 0
    kv = w[..., o:o + KV_RANK]; o += KV_RANK
    kr = w[..., o:o + ROPE]; o += ROPE
    xbc = w[..., o:o + XBC]; o += XBC
    dtw = w[..., o:o + 2 * SSM_HEADS]; o += 2 * SSM_HEADS
    rest = w[..., o:]

    def lanes(cols, start):
        return jnp.pad(cols, ((0, 0), (0, 0), (start, 128 - start - cols.shape[-1])))

    pieces = [kv, lanes(kr, NOPE), lanes(_rot_half(kr), NOPE), lanes(dtw, 0), xbc]
    w_kv_side = jnp.concatenate([p.astype(BF16) for p in pieces], axis=-1)

    uq = mla_w_uq.astype(BF16).reshape(DEPTH, Q_RANK, HEADS, NOPE + ROPE)
    uq_n, uq_r = uq[..., :NOPE], uq[..., NOPE:]
    ukv = mla_w_ukv.astype(BF16).reshape(DEPTH, KV_RANK, HEADS, NOPE + VDIM)
    kn = _head_pad(ukv[..., :NOPE], jnp.zeros((DEPTH, KV_RANK, HEADS, ROPE), BF16))
    v_t = jnp.pad(jnp.transpose(ukv[..., NOPE:], (0, 2, 3, 1)), ((0, 0), (0, 0), (0, V_ROWS - VDIM), (0, 0)))
    v_ones = jnp.tile((jnp.arange(V_ROWS) == VDIM).astype(F32), HEADS).reshape(HEADS * V_ROWS, 1)

    ws = gm_w_s.astype(BF16)
    ws_pair = jnp.concatenate([ws[:, 0::2], ws[:, 1::2]], axis=3)

    def row128(vals, fill=0.0):
        vals = vals.reshape(DEPTH, 1, -1)
        return jnp.pad(vals, ((0, 0), (0, 0), (0, 128 - vals.shape[-1])), constant_values=fill)

    return {
        "w_in_kv": w_kv_side,
        "w_in_q": rest.astype(BF16),
        "kv_norm": mla_kv_norm.reshape(DEPTH, 1, KV_RANK),
        "w_kn": kn,
        "w_v_t": v_t.reshape(DEPTH, HEADS * V_ROWS, KV_RANK),
        "v_ones": v_ones,
        "q_norm": mla_q_norm.reshape(DEPTH, 1, Q_RANK),
        "w_q": _head_pad(uq_n, uq_r),
        "w_q_rot": _head_pad(jnp.zeros_like(uq_n), _rot_half(uq_r)),
        "conv_w": jnp.pad(jnp.swapaxes(ssm_conv_w, 1, 2), ((0, 0), (0, 8 - SSM_CONV), (0, 0))),
        "conv_b": ssm_conv_b.reshape(DEPTH, 1, XBC),
        "dt_bias_row": row128(ssm_dt_bias),
        "dt_bias_col": ssm_dt_bias.reshape(DEPTH, 2 * SSM_HEADS, 1),
        "a_log_row": row128(ssm_a_log, fill=-80.0),
        "a_log_col": ssm_a_log.reshape(DEPTH, 2 * SSM_HEADS, 1),
        "d_full": jnp.repeat(ssm_d, SSM_P, axis=1).reshape(DEPTH, 1, SSM_INNER),
        "w_mla_o": mla_w_o.astype(BF16),
        "ssm_norm": ssm_norm.reshape(DEPTH, 1, SSM_INNER),
        "w_ssm_o": ssm_w_o.astype(BF16),
        "gm_norm": gm_norm.reshape(DEPTH, 1, GM_WIDTH),
        "w_s": ws_pair,
        "b_s": jnp.repeat(jnp.swapaxes(gm_b_s, 1, 2), GM_GDIM, axis=2),
        "w_gm_o": gm_w_o.astype(BF16),
        "b_gate": b_gate.reshape(DEPTH, 1, N_BRANCH * D),
        "w_out": w_out.astype(BF16),
    }


def _rope_tables():
    rows = SEQ // GRID_W
    r = jnp.repeat(jnp.arange(rows, dtype=F32), GRID_W)
    c = jnp.tile(jnp.arange(GRID_W, dtype=F32), rows)
    n_freq = ROPE // 4
    inv = jnp.power(ROPE_BASE, -jnp.arange(n_freq, dtype=F32) / n_freq)
    ang = jnp.concatenate([r[:, None] * inv, c[:, None] * inv], axis=-1)
    cos, sin = jnp.cos(ang), jnp.sin(ang)
    cos = jnp.concatenate([jnp.ones((CTX, ROPE // 2), F32), cos], axis=0)
    sin = jnp.concatenate([jnp.zeros((CTX, ROPE // 2), F32), sin], axis=0)
    ones = jnp.ones((T, NOPE), F32)
    zeros_n = jnp.zeros((T, NOPE), F32)
    zeros_p = jnp.zeros((T, HEAD_PAD - NOPE - ROPE), F32)
    cos_t = jnp.concatenate([ones, cos, cos, zeros_p], axis=1)
    sin_t = jnp.concatenate([zeros_n, sin, sin, zeros_p], axis=1)
    return cos_t, sin_t


def kernel(x, c, ctx, c_ctx, w_ada, b_ada, norm_g, ffn1_w_in, ffn1_w_out, ffn2_w_in, ffn2_w_out, w_in, mla_q_norm,
           mla_w_uq, mla_kv_norm, mla_w_ukv, mla_w_o, ssm_conv_w, ssm_conv_b, ssm_a_log, ssm_dt_bias, ssm_d,
           ssm_norm, ssm_w_o, gm_norm, gm_w_s, gm_b_s, gm_w_o, b_gate, w_out, final_norm):
    c_all = jnp.concatenate([c, c_ctx[None, :], jnp.zeros((MOD_ROWS - BATCH - 1, D), F32)], axis=0)
    w = _prep_weights(w_in, mla_q_norm, mla_w_uq, mla_kv_norm, mla_w_ukv, mla_w_o, ssm_conv_w, ssm_conv_b, ssm_a_log,
                      ssm_dt_bias, ssm_d, ssm_norm, ssm_w_o, gm_norm, gm_w_s, gm_b_s, gm_w_o, b_gate, w_out)
    w["mods"] = _ada(c_all, w_ada, b_ada).reshape(DEPTH, MOD_ROWS, N_MOD, D)
    w["norm_g"] = norm_g.reshape(DEPTH * 3, 1, D)
    w["ffn_w_in"] = (ffn1_w_in.astype(BF16), ffn2_w_in.astype(BF16))
    w["ffn_w_out"] = (ffn1_w_out.astype(BF16), ffn2_w_out.astype(BF16))
    w["final_norm"] = final_norm.reshape(1, D)
    w["cos"], w["sin"] = _rope_tables()

    xs = (ctx, x)
    for l in range(DEPTH):
        last = l == DEPTH - 1
        t0 = 1 if last else 0
        xs = _ffn(xs, w, l, 0)
        q, k, vt, xbc, dt, dtt, z, uv, gate = _proj(xs, w, l, ctx_queries=not last)
        a = _attn(q, k, vt, t0=t0)
        y = _ssd(xbc, dt, dtt, w, l)
        xs = _mixout(xs, a, y, z, uv, gate, w, l, t0=t0, final=last)
    return xs
```

```python
import functools
import math

import jax
import jax.numpy as jnp
from jax import lax
from jax.experimental import pallas as pl
from jax.experimental.pallas import tpu as pltpu

F32 = jnp.float32
BF16 = jnp.bfloat16

D = 1024
BATCH = 8
SEQ = 2048
DEPTH = 2
CTX = 256
T = CTX + SEQ
GRID_W = 64
EPS = 1e-6

HEADS = 8
NOPE = 64
ROPE = 32
VDIM = 64
Q_RANK = 256
KV_RANK = 128
ROPE_BASE = 10000.0
ATTN_SCALE = (NOPE + ROPE) ** -0.5
EXP2_SCALE = ATTN_SCALE * math.log2(math.e)
HEAD_PAD = 128
V_ROWS = 80

SSM_HEADS = 8
SSM_P = 64
SSM_INNER = SSM_HEADS * SSM_P
SSM_GROUPS = 2
SSM_N = 128
SSM_CONV = 5
CHUNK = 128
XBC = SSM_INNER + 2 * SSM_GROUPS * SSM_N
N_CHUNKS = T // CHUNK
CTX_CHUNKS = CTX // CHUNK
HALO = 8

GM_GROUPS = 8
GM_WIDTH = 512
GM_GDIM = GM_WIDTH // GM_GROUPS

D_FF = 2816
N_BRANCH = 3
N_MOD = 9
MOD_ROWS = 16
ADA_COLS = 2304
CTX_MOD_ROW = BATCH

C_KV = 0
C_KR = 128
C_KRR = 256
C_DT = 384
C_XBC = 512
C_Q = C_XBC + XBC
C_Z = C_Q + Q_RANK
C_UV = C_Z + SSM_INNER
C_GATE = C_UV + 2 * GM_WIDTH
C_END = C_GATE + N_BRANCH * D

TM = 256
KEY_BLOCK = 128
CONV_ROWS = 64
PROJ_PIECE = 512
VMEM_LIMIT = 56 * 1024 * 1024


def _rms(x, g):
    y = x * lax.rsqrt(jnp.mean(x * x, axis=-1, keepdims=True) + EPS)
    return y * g


def _silu(x):
    return x * jax.nn.sigmoid(x)


def _bdot(a, b):
    return jnp.dot(a.astype(BF16), b.astype(BF16), preferred_element_type=F32)


def _split3(a):
    a1 = a.astype(BF16)
    r1 = a - a1.astype(F32)
    a2 = r1.astype(BF16)
    a3 = (r1 - a2.astype(F32)).astype(BF16)
    return a1, a2, a3


def _dot_right01(a, m01):
    return sum(jnp.dot(p, m01, preferred_element_type=F32) for p in _split3(a))


def _dot_left01(m01, a):
    return sum(jnp.dot(m01, p, preferred_element_type=F32) for p in _split3(a))


def _const_spec(shape, l=None):
    nd = len(shape)
    if l is None:
        return pl.BlockSpec(shape, lambda *_: (0,) * nd, pipeline_mode=pl.Buffered(1))
    return pl.BlockSpec((None,) + shape, lambda *_: (l,) + (0,) * nd, pipeline_mode=pl.Buffered(1))


def _mod_spec(l, t0):
    return pl.BlockSpec((None, 1, N_MOD, D), lambda b, t: (l, jnp.where(t + t0 == 0, CTX_MOD_ROW, b), 0, 0))


def _row_spec(width, t0):
    return pl.BlockSpec((1, TM, width), lambda b, t: (b, t + t0, 0))


def _params(sem):
    return pltpu.CompilerParams(dimension_semantics=sem, vmem_limit_bytes=VMEM_LIMIT)


def _ada_kernel(c_ref, w_ref, b_ref, o_ref):
    s = _silu(c_ref[...])
    o_ref[0] = _bdot(s, w_ref[0]) + b_ref[0]


def _ada(c_all, w_ada, b_ada):
    tn = ADA_COLS
    nn = N_MOD * D // tn
    return pl.pallas_call(
        _ada_kernel,
        grid=(DEPTH, nn),
        in_specs=[
            pl.BlockSpec((MOD_ROWS, D), lambda l, j: (0, 0)),
            pl.BlockSpec((1, D, tn), lambda l, j: (l, 0, j)),
            pl.BlockSpec((1, 1, tn), lambda l, j: (l, 0, j)),
        ],
        out_specs=pl.BlockSpec((1, MOD_ROWS, tn), lambda l, j: (l, 0, j)),
        out_shape=jax.ShapeDtypeStruct((DEPTH, MOD_ROWS, N_MOD * D), F32),
        compiler_params=_params(("arbitrary", "arbitrary")),
        name="ada_mod",
    )(c_all, w_ada, b_ada.reshape(DEPTH, 1, N_MOD * D))


def _ffn_rows(x, m, mod0, g_ref, wi_ref, wo_ref, fin_ref=None):
    shift, scale, gate = m[mod0:mod0 + 1], m[mod0 + 1:mod0 + 2], m[mod0 + 2:mod0 + 3]
    h = _rms(x, g_ref[...]) * (1.0 + scale) + shift
    gu = jnp.dot(h.astype(BF16), wi_ref[...], preferred_element_type=F32)
    a = _silu(gu[:, :D_FF]) * gu[:, D_FF:]
    y = jnp.dot(a.astype(BF16), wo_ref[...], preferred_element_type=F32)
    out = x + (0.5 * gate) * y
    if fin_ref is not None:
        out = _rms(out, fin_ref[...])
    return out


def _ffn_kernel(*refs, split_input):
    if split_input:
        ctx_ref, x_ref, mod_ref, g_ref, wi_ref, wo_ref, o_ref = refs
        x = jnp.where(pl.program_id(1) == 0, ctx_ref[0], x_ref[0])
    else:
        x_ref, mod_ref, g_ref, wi_ref, wo_ref, o_ref = refs
        x = x_ref[0]
    o_ref[0] = _ffn_rows(x, mod_ref[0], 0, g_ref, wi_ref, wo_ref)


def _ffn(xs, w, l):
    split_input = isinstance(xs, tuple)
    nt = T // TM
    if split_input:
        x_specs = [pl.BlockSpec((1, TM, D), lambda b, t: (b, 0, 0)),
                   pl.BlockSpec((1, TM, D), lambda b, t: (b, jnp.maximum(t - 1, 0), 0))]
        xs_args = list(xs)
    else:
        x_specs = [_row_spec(D, 0)]
        xs_args = [xs]
    return pl.pallas_call(
        functools.partial(_ffn_kernel, split_input=split_input),
        grid=(BATCH, nt),
        in_specs=x_specs + [
            _mod_spec(l, 0),
            _const_spec((1, D), 3 * l),
            _const_spec((D, 2 * D_FF), l),
            _const_spec((D_FF, D), l),
        ],
        out_specs=_row_spec(D, 0),
        out_shape=jax.ShapeDtypeStruct((BATCH, T, D), F32),
        compiler_params=_params(("parallel", "parallel")),
        name="ffn",
    )(*xs_args, w["mods"], w["norm_g"], w["ffn_w_in"][0], w["ffn_w_out"][0])


def _proj_kernel(x_ref, xprev_ref, xnext_ref, mod_ref, g_ref, wkv_ref, wq_side_ref, cw_ref, cb_ref, kvn_ref, wkn_ref,
                 wvt_ref, one_ref, qn_ref, wq_ref, wqr_ref, cos_ref, sin_ref,
                 q_out, k_out, vt_out, xc_out, dt_out, dtt_out, z_out, uv_out, gate_out, *, ctx_queries):
    t = pl.program_id(1)
    m = mod_ref[0]
    xe = jnp.concatenate([xprev_ref[0], x_ref[0], xnext_ref[0]], axis=0)
    he = _rms(xe, g_ref[...]) * (1.0 + m[4:5]) + m[3:4]
    hb = he[HALO:HALO + TM].astype(BF16)
    he = he.astype(BF16)
    cos = cos_ref[...]
    sin = sin_ref[...]

    small = jnp.dot(hb, wkv_ref[:, C_KV:C_XBC], preferred_element_type=F32)

    xbc = jnp.dot(he, wkv_ref[:, C_XBC:C_Q], preferred_element_type=F32)
    row = lax.broadcasted_iota(jnp.int32, (TM + 2 * HALO, 1), 0)
    has_prev = t >= 2
    has_next = (t >= 1) & (t < T // TM - 1)
    keep = ((row >= HALO) | has_prev) & ((row < HALO + TM) | has_next)
    def conv_lane_tile(c0):
        for r0 in range(0, TM, CONV_ROWS):
            win = xbc[r0:r0 + CONV_ROWS + 2 * HALO, c0:c0 + 128]
            if r0 == 0 or r0 + CONV_ROWS == TM:
                win = jnp.where(keep[r0:r0 + CONV_ROWS + 2 * HALO], win, 0.0)
            acc = cb_ref[:, c0:c0 + 128]
            for k in range(SSM_CONV):
                shift = (SSM_CONV // 2 - k) % (CONV_ROWS + 2 * HALO)
                tap = win if shift == 0 else pltpu.roll(win, shift, axis=0)
                acc = acc + cw_ref[k:k + 1, c0:c0 + 128] * tap[HALO:HALO + CONV_ROWS]
            xc_out[0, r0:r0 + CONV_ROWS, c0:c0 + 128] = _silu(acc)

    def query_side():
        def proj(c0, c1):
            return jnp.dot(hb, wq_side_ref[:, c0 - C_Q:c1 - C_Q], preferred_element_type=F32)

        q_lat = proj(C_Q, C_Z)
        conv_tiles = list(range(0, XBC, 128))
        for out, base, width in ((z_out, C_Z, SSM_INNER), (uv_out, C_UV, 2 * GM_WIDTH), (gate_out, C_GATE, N_BRANCH * D)):
            for c0 in range(0, width, PROJ_PIECE):
                out[0, :, c0:c0 + PROJ_PIECE] = proj(base + c0, base + c0 + PROJ_PIECE).astype(out.dtype)
                if conv_tiles:
                    conv_lane_tile(conv_tiles.pop(0))
        assert not conv_tiles
        qn = _rms(q_lat, qn_ref[...]).astype(BF16)
        qa = jnp.dot(qn, wq_ref[...], preferred_element_type=F32)
        qb = jnp.dot(qn, wqr_ref[...], preferred_element_type=F32)
        for hd in range(HEADS):
            sl = slice(hd * HEAD_PAD, (hd + 1) * HEAD_PAD)
            q_out[0, :, sl] = (qa[:, sl] * cos + qb[:, sl] * sin).astype(BF16)

    def query_side_unused():
        for out in (q_out, z_out, uv_out, gate_out):
            out[...] = jnp.zeros_like(out)
        for c0 in range(0, XBC, 128):
            conv_lane_tile(c0)

    if ctx_queries:
        query_side()
    else:
        pl.when(t != 0)(query_side)
        pl.when(t == 0)(query_side_unused)

    dt = small[:, C_DT:C_DT + 128]
    dt_out[0] = dt
    dtt_out[0] = dt.T[:2 * SSM_HEADS]

    kvn = _rms(small[:, C_KV:C_KV + 128], kvn_ref[...]).astype(BF16)
    kn = jnp.dot(kvn, wkn_ref[...], preferred_element_type=F32)
    kr = small[:, C_KR:C_KR + 128] * cos + small[:, C_KRR:C_KRR + 128] * sin
    for hd in range(HEADS):
        sl = slice(hd * HEAD_PAD, (hd + 1) * HEAD_PAD)
        k_out[0, :, sl] = (kn[:, sl] + kr).astype(BF16)
    vt = lax.dot_general(wvt_ref[...], kvn, (((1,), (1,)), ((), ())), preferred_element_type=F32)
    vt_out[0] = (vt + one_ref[...]).astype(BF16)


def _proj(xs, w, l, *, ctx_queries):
    nt = T // TM

    def rows(w, dt):
        return jax.ShapeDtypeStruct((BATCH, T, w), dt), _row_spec(w, 0)

    def cols(r, dt):
        return jax.ShapeDtypeStruct((BATCH, r, T), dt), pl.BlockSpec((1, r, TM), lambda b, t: (b, 0, t))

    outs = [rows(HEADS * HEAD_PAD, BF16), rows(HEADS * HEAD_PAD, BF16), cols(HEADS * V_ROWS, BF16),
            rows(XBC, F32), rows(128, F32), cols(2 * SSM_HEADS, F32),
            rows(SSM_INNER, BF16), rows(2 * GM_WIDTH, BF16), rows(N_BRANCH * D, BF16)]
    out_shape = [o[0] for o in outs]
    out_specs = [o[1] for o in outs]
    return pl.pallas_call(
        functools.partial(_proj_kernel, ctx_queries=ctx_queries),
        grid=(BATCH, nt),
        in_specs=[
            _row_spec(D, 0),
            pl.BlockSpec((1, HALO, D), lambda b, t: (b, jnp.maximum(t * (TM // HALO) - 1, 0), 0)),
            pl.BlockSpec((1, HALO, D), lambda b, t: (b, jnp.minimum((t + 1) * (TM // HALO), T // HALO - 1), 0)),
            _mod_spec(l, 0),
            _const_spec((1, D), 3 * l + 1),
            _const_spec((D, C_Q), l),
            _const_spec((D, C_END - C_Q), l),
            _const_spec((8, XBC), l),
            _const_spec((1, XBC), l),
            _const_spec((1, KV_RANK), l),
            _const_spec((KV_RANK, HEADS * HEAD_PAD), l),
            _const_spec((HEADS * V_ROWS, KV_RANK), l),
            _const_spec((HEADS * V_ROWS, 1)),
            _const_spec((1, Q_RANK), l),
            _const_spec((Q_RANK, HEADS * HEAD_PAD), l),
            _const_spec((Q_RANK, HEADS * HEAD_PAD), l),
            pl.BlockSpec((TM, HEAD_PAD), lambda b, t: (t, 0)),
            pl.BlockSpec((TM, HEAD_PAD), lambda b, t: (t, 0)),
        ],
        out_specs=out_specs,
        out_shape=out_shape,
        compiler_params=_params(("parallel", "parallel")),
        name="mixer_proj",
    )(xs, xs, xs, w["mods"], w["norm_g"], w["w_in_kv"], w["w_in_q"], w["conv_w"], w["conv_b"], w["kv_norm"], w["w_kn"],
      w["w_v_t"],
      w["v_ones"],
      w["q_norm"], w["w_q"], w["w_q_rot"], w["cos"], w["sin"])


def _attn_kernel(q_ref, k_ref, vt_ref, o_ref, *, t0):
    def run(nk):
        nblk = nk // KEY_BLOCK

        def score_block(hd, j):
            q = q_ref[0, :, hd * HEAD_PAD:(hd + 1) * HEAD_PAD]
            k = k_ref[0, j * KEY_BLOCK:(j + 1) * KEY_BLOCK, hd * HEAD_PAD:(hd + 1) * HEAD_PAD]
            return lax.dot_general(k, q, (((1,), (1,)), ((), ())), preferred_element_type=F32)

        def col_max(blocks):
            mx = functools.reduce(jnp.maximum, blocks)
            mx = jnp.max(mx.reshape(KEY_BLOCK // 8, 8, TM), axis=0)
            return jnp.max(mx, axis=0, keepdims=True)

        outs = []
        cur = [score_block(0, j) for j in range(nblk)]
        for hd in range(HEADS):
            mx = col_max(cur)
            nxt, ps = [], []
            for j in range(nblk):
                if hd + 1 < HEADS:
                    nxt.append(score_block(hd + 1, j))
                ps.append(jnp.exp2((cur[j] - mx) * EXP2_SCALE).astype(BF16))
            p = jnp.concatenate(ps, axis=0)
            ot = jnp.dot(vt_ref[0, hd * V_ROWS:(hd + 1) * V_ROWS, :nk], p, preferred_element_type=F32)
            outs.append(ot[:VDIM] / ot[VDIM:VDIM + 1])
            cur = nxt
        o_ref[0] = jnp.concatenate(outs, axis=0).T.astype(BF16)

    if t0 == 0:
        t = pl.program_id(1)
        pl.when(t == 0)(lambda: run(CTX))
        pl.when(t != 0)(lambda: run(T))
    else:
        run(T)


def _attn(q, k, v, *, t0):
    nt = T // TM - t0
    return pl.pallas_call(
        functools.partial(_attn_kernel, t0=t0),
        grid=(BATCH, nt),
        in_specs=[
            _row_spec(HEADS * HEAD_PAD, t0),
            pl.BlockSpec((1, T, HEADS * HEAD_PAD), lambda b, t: (b, 0, 0)),
            pl.BlockSpec((1, HEADS * V_ROWS, T), lambda b, t: (b, 0, 0)),
        ],
        out_specs=pl.BlockSpec((1, TM, HEADS * VDIM), lambda b, t: (b, t, 0)),
        out_shape=jax.ShapeDtypeStruct((BATCH, nt * TM, HEADS * VDIM), BF16),
        compiler_params=_params(("parallel", "arbitrary")),
        name="mla_attn",
    )(q, k, v)


def _ssd_kernel(xc_ref, dt_ref, dtt_ref, bias_row_ref, bias_col_ref, alog_row_ref, alog_col_ref, dfull_ref,
                y_ref, h_ref, dcol_ref, drow_ref, acol_ref, arow_ref):
    row = lax.broadcasted_iota(jnp.int32, (CHUNK, CHUNK), 0)
    col = lax.broadcasted_iota(jnp.int32, (CHUNK, CHUNK), 1)
    lower = (col <= row)
    upper = (col >= row)
    lower01 = jnp.where(lower, 1.0, 0.0).astype(BF16)
    upper01 = jnp.where(upper, 1.0, 0.0).astype(BF16)
    lo_half = lax.broadcasted_iota(jnp.int32, (CHUNK, 128), 1) < SSM_P

    chunks = range(N_CHUNKS)
    dcol_all = jax.nn.softplus(dt_ref[0] + bias_row_ref[...])
    drow_all = jax.nn.softplus(dtt_ref[0] + bias_col_ref[...])
    dcol_ref[...] = dcol_all
    drow_ref[...] = drow_all
    a_col_form = dcol_all * -jnp.exp(alog_row_ref[...])
    a_wide = jnp.concatenate([a_col_form[c * CHUNK:(c + 1) * CHUNK] for c in chunks], axis=1)
    causal_lane = lax.broadcasted_iota(jnp.int32, (CHUNK, N_CHUNKS * 128), 1) % 128 < SSM_HEADS
    acol_wide = jnp.where(causal_lane, _dot_left01(lower01, a_wide), _dot_left01(upper01, a_wide))
    a_row_form = drow_all * -jnp.exp(alog_col_ref[...])
    a_tall = jnp.concatenate([a_row_form[:, c * CHUNK:(c + 1) * CHUNK] for c in chunks], axis=0)
    causal_row = lax.broadcasted_iota(jnp.int32, (N_CHUNKS * 2 * SSM_HEADS, CHUNK), 0) % (2 * SSM_HEADS) < SSM_HEADS
    arow_tall = jnp.where(causal_row, _dot_right01(a_tall, upper01), _dot_right01(a_tall, lower01))
    for c in chunks:
        acol_ref[c] = acol_wide[:, c * 128:(c + 1) * 128]
        arow_ref[c] = arow_tall[c * 2 * SSM_HEADS:(c + 1) * 2 * SSM_HEADS]

    def chunk_step(c, d):
        r0 = pl.multiple_of(c * CHUNK, CHUNK)
        xc = xc_ref[0, pl.ds(r0, CHUNK), :]
        dcol = dcol_ref[pl.ds(r0, CHUNK), :]
        drow = drow_ref[:, pl.ds(r0, CHUNK)]
        mask = lower if d == 0 else upper
        acum_c = acol_ref[c]
        acum_r = arow_ref[c]
        atot_c = acum_c[CHUNK - 1:CHUNK] if d == 0 else acum_c[0:1]
        w_c = jnp.exp(atot_c - acum_c) * dcol
        etot_c = jnp.exp(atot_c)

        def head_col(arr, hd):
            lane = d * SSM_HEADS + hd
            return arr[:, lane:lane + 1]

        def pair_lanes(arr, p):
            return jnp.where(lo_half[:arr.shape[0]], head_col(arr, 2 * p), head_col(arr, 2 * p + 1))

        ys = []
        for g in range(SSM_GROUPS):
            bm = xc[:, SSM_INNER + g * SSM_N:SSM_INNER + (g + 1) * SSM_N]
            cm = xc[:, SSM_INNER + (SSM_GROUPS + g) * SSM_N:SSM_INNER + (SSM_GROUPS + g + 1) * SSM_N]
            bmb = bm.astype(BF16)
            cmb = cm.astype(BF16)
            cb = lax.dot_general(cmb, bmb, (((1,), (1,)), ((), ())), preferred_element_type=F32)
            bt = bm.T.astype(BF16)
            for pp in range(2):
                p = g * 2 + pp
                sc, a_bc = [], []
                for hh in range(2):
                    hd = 2 * p + hh
                    a_bc.append(jnp.broadcast_to(head_col(acum_c, hd), (CHUNK, CHUNK)))
                    diff = a_bc[hh] - acum_r[d * SSM_HEADS + hd:d * SSM_HEADS + hd + 1]
                    dec = jnp.exp(jnp.where(mask, diff, -jnp.inf))
                    sc.append(cb * dec * drow[d * SSM_HEADS + hd:d * SSM_HEADS + hd + 1])
                x_p = xc[:, p * 128:(p + 1) * 128]
                xw = (x_p * pair_lanes(w_c, p)).astype(BF16)
                s_t = jnp.dot(bt, xw, preferred_element_type=F32)
                h_prev = h_ref[d, p]
                e_p = jnp.exp(jnp.where(lo_half, a_bc[0], a_bc[1]))
                y_inter = jnp.dot(cmb, h_prev.astype(BF16), preferred_element_type=F32) * e_p
                pmat = jnp.concatenate(sc, axis=1).astype(BF16)
                xbd = jnp.concatenate([jnp.where(lo_half, x_p, 0.0), jnp.where(lo_half, 0.0, x_p)],
                                      axis=0).astype(BF16)
                ys.append(jnp.dot(pmat, xbd, preferred_element_type=F32) + y_inter)
                h_ref[d, p] = h_prev * pair_lanes(etot_c, p) + s_t
        y = jnp.concatenate(ys, axis=1)
        if d == 0:
            y = y + dfull_ref[...] * xc[:, :SSM_INNER]
        y_ref[0, pl.ds(r0, CHUNK), :] += y

    h_ref[...] = jnp.zeros_like(h_ref)
    y_ref[...] = jnp.zeros_like(y_ref)

    def scan_body(i, carry):
        chunk_step(i, 0)
        chunk_step(jnp.where(i < CTX_CHUNKS, CTX_CHUNKS - 1 - i, N_CHUNKS + CTX_CHUNKS - 1 - i), 1)
        return carry

    lax.fori_loop(0, N_CHUNKS, scan_body, 0, unroll=2)


def _ssd(xc, dt, dtt, w, l):
    return pl.pallas_call(
        _ssd_kernel,
        grid=(BATCH,),
        in_specs=[
            pl.BlockSpec((1, T, XBC), lambda b: (b, 0, 0)),
            pl.BlockSpec((1, T, 128), lambda b: (b, 0, 0)),
            pl.BlockSpec((1, 2 * SSM_HEADS, T), lambda b: (b, 0, 0)),
            _const_spec((1, 128), l),
            _const_spec((2 * SSM_HEADS, 1), l),
            _const_spec((1, 128), l),
            _const_spec((2 * SSM_HEADS, 1), l),
            _const_spec((1, SSM_INNER), l),
        ],
        out_specs=pl.BlockSpec((1, T, SSM_INNER), lambda b: (b, 0, 0)),
        out_shape=jax.ShapeDtypeStruct((BATCH, T, SSM_INNER), F32),
        scratch_shapes=[
            pltpu.VMEM((2, SSM_HEADS // 2, SSM_N, 2 * SSM_P), F32),
            pltpu.VMEM((T, 128), F32),
            pltpu.VMEM((2 * SSM_HEADS, T), F32),
            pltpu.VMEM((N_CHUNKS, CHUNK, 128), F32),
            pltpu.VMEM((N_CHUNKS, 2 * SSM_HEADS, CHUNK), F32),
        ],
        compiler_params=_params(("arbitrary",)),
        name="ssd_bidir",
    )(xc, dt, dtt, w["dt_bias_row"], w["dt_bias_col"], w["a_log_row"], w["a_log_col"], w["d_full"])


def _mixout_kernel(x_ref, mod_ref, a_ref, y_ref, z_ref, uv_ref, gate_ref,
                   wmo_ref, sn_ref, wso_ref, gn_ref, ws_ref, bs_ref, wgo_ref, bg_ref, wout_ref,
                   g2_ref, wi_ref, wo_ref, fin_ref, o_ref, *, final):
    x = x_ref[0]
    m = mod_ref[0]
    o_mla = jnp.dot(a_ref[0], wmo_ref[...], preferred_element_type=F32)

    ys = y_ref[0] * _silu(z_ref[0].astype(F32))
    o_ssm = _bdot(_rms(ys, sn_ref[...]), wso_ref[...])

    uv = jax.nn.gelu(uv_ref[0].astype(F32))
    u = uv[:, :GM_WIDTH]
    v = uv[:, GM_WIDTH:]
    v = v - jnp.mean(v, axis=-1, keepdims=True)
    v = v * lax.rsqrt(jnp.mean(v * v, axis=-1, keepdims=True) + EPS) * gn_ref[...]
    lo_half = lax.broadcasted_iota(jnp.int32, (CHUNK, 128), 1) < GM_GDIM
    mixed = []
    for ch in range(TM // CHUNK):
        parts = []
        for p in range(GM_GROUPS // 2):
            vp = v[ch * CHUNK:(ch + 1) * CHUNK, p * 128:(p + 1) * 128]
            vbd = jnp.concatenate([jnp.where(lo_half, vp, 0.0), jnp.where(lo_half, 0.0, vp)], axis=0)
            parts.append(jnp.dot(ws_ref[p], vbd.astype(BF16), preferred_element_type=F32))
        mixed.append(jnp.concatenate(parts, axis=1) + bs_ref[...])
    mixed = jnp.concatenate(mixed, axis=0)
    o_gm = _bdot(u * mixed, wgo_ref[...])

    gts = jax.nn.sigmoid(gate_ref[0].astype(F32) + bg_ref[...])
    merged = gts[:, :D] * o_mla + gts[:, D:2 * D] * o_ssm + gts[:, 2 * D:] * o_gm
    y = _bdot(merged, wout_ref[...])
    x = x + m[5:6] * y
    o_ref[0] = _ffn_rows(x, m, 6, g2_ref, wi_ref, wo_ref, fin_ref if final else None)


def _mixout(xs, a, y, z, uv, gate, w, l, *, t0, final):
    nt = T // TM - t0
    return pl.pallas_call(
        functools.partial(_mixout_kernel, final=final),
        grid=(BATCH, nt),
        in_specs=[
            _row_spec(D, t0),
            _mod_spec(l, t0),
            pl.BlockSpec((1, TM, HEADS * VDIM), lambda b, t: (b, t, 0)),
            _row_spec(SSM_INNER, t0),
            _row_spec(SSM_INNER, t0),
            _row_spec(2 * GM_WIDTH, t0),
            _row_spec(N_BRANCH * D, t0),
            _const_spec((HEADS * VDIM, D), l),
            _const_spec((1, SSM_INNER), l),
            _const_spec((SSM_INNER, D), l),
            _const_spec((1, GM_WIDTH), l),
            _const_spec((GM_GROUPS // 2, CHUNK, 2 * CHUNK), l),
            _const_spec((CHUNK, GM_WIDTH), l),
            _const_spec((GM_WIDTH, D), l),
            _const_spec((1, N_BRANCH * D), l),
            _const_spec((D, D), l),
            _const_spec((1, D), 3 * l + 2),
            _const_spec((D, 2 * D_FF), l),
            _const_spec((D_FF, D), l),
            _const_spec((1, D)),
        ],
        out_specs=pl.BlockSpec((1, TM, D), lambda b, t: (b, t, 0)),
        out_shape=jax.ShapeDtypeStruct((BATCH, nt * TM, D), F32),
        compiler_params=_params(("parallel", "parallel")),
        name="mixer_out_ffn",
    )(xs, w["mods"], a, y, z, uv, gate, w["w_mla_o"], w["ssm_norm"], w["w_ssm_o"], w["gm_norm"], w["w_s"],
      w["b_s"], w["w_gm_o"], w["b_gate"], w["w_out"], w["norm_g"], w["ffn_w_in"][1], w["ffn_w_out"][1],
      w["final_norm"])


def _rot_half(w):
    half = ROPE // 2
    return jnp.concatenate([-w[..., half:], w[..., :half]], axis=-1)


def _head_pad(nope, rope):
    pad = jnp.zeros(nope.shape[:-1] + (HEAD_PAD - NOPE - ROPE,), nope.dtype)
    out = jnp.concatenate([nope, rope, pad], axis=-1)
    return out.reshape(out.shape[:-2] + (HEADS * HEAD_PAD,))


def _prep_weights(w_in, mla_q_norm, mla_w_uq, mla_kv_norm, mla_w_ukv, mla_w_o, ssm_conv_w, ssm_conv_b,
                  ssm_a_log, ssm_dt_bias, ssm_d, ssm_norm, ssm_w_o, gm_norm, gm_w_s, gm_b_s, gm_w_o, b_gate, w_out):
    w = w_in
    o = 0
    kv = w[..., o:o + KV_RANK]; o += KV_RANK
    kr = w[..., o:o + ROPE]; o += ROPE
    xbc = w[..., o:o + XBC]; o += XBC
    dtw = w[..., o:o + 2 * SSM_HEADS]; o += 2 * SSM_HEADS
    rest = w[..., o:]

    def lanes(cols, start):
        return jnp.pad(cols, ((0, 0), (0, 0), (start, 128 - start - cols.shape[-1])))

    pieces = [kv, lanes(kr, NOPE), lanes(_rot_half(kr), NOPE), lanes(dtw, 0), xbc]
    w_kv_side = jnp.concatenate([p.astype(BF16) for p in pieces], axis=-1)

    uq = mla_w_uq.astype(BF16).reshape(DEPTH, Q_RANK, HEADS, NOPE + ROPE)
    uq_n, uq_r = uq[..., :NOPE], uq[..., NOPE:]
    ukv = mla_w_ukv.astype(BF16).reshape(DEPTH, KV_RANK, HEADS, NOPE + VDIM)
    kn = _head_pad(ukv[..., :NOPE], jnp.zeros((DEPTH, KV_RANK, HEADS, ROPE), BF16))
    v_t = jnp.pad(jnp.transpose(ukv[..., NOPE:], (0, 2, 3, 1)), ((0, 0), (0, 0), (0, V_ROWS - VDIM), (0, 0)))
    v_ones = jnp.tile((jnp.arange(V_ROWS) == VDIM).astype(F32), HEADS).reshape(HEADS * V_ROWS, 1)

    ws = gm_w_s.astype(BF16)
    ws_pair = jnp.concatenate([ws[:, 0::2], ws[:, 1::2]], axis=3)

    def row128(vals, fill=0.0):
        vals = vals.reshape(DEPTH, 1, -1)
        return jnp.pad(vals, ((0, 0), (0, 0), (0, 128 - vals.shape[-1])), constant_values=fill)

    return {
        "w_in_kv": w_kv_side,
        "w_in_q": rest.astype(BF16),
        "kv_norm": mla_kv_norm.reshape(DEPTH, 1, KV_RANK),
        "w_kn": kn,
        "w_v_t": v_t.reshape(DEPTH, HEADS * V_ROWS, KV_RANK),
        "v_ones": v_ones,
        "q_norm": mla_q_norm.reshape(DEPTH, 1, Q_RANK),
        "w_q": _head_pad(uq_n, uq_r),
        "w_q_rot": _head_pad(jnp.zeros_like(uq_n), _rot_half(uq_r)),
        "conv_w": jnp.pad(jnp.swapaxes(ssm_conv_w, 1, 2), ((0, 0), (0, 8 - SSM_CONV), (0, 0))),
        "conv_b": ssm_conv_b.reshape(DEPTH, 1, XBC),
        "dt_bias_row": row128(ssm_dt_bias),
        "dt_bias_col": ssm_dt_bias.reshape(DEPTH, 2 * SSM_HEADS, 1),
        "a_log_row": row128(ssm_a_log, fill=-80.0),
        "a_log_col": ssm_a_log.reshape(DEPTH, 2 * SSM_HEADS, 1),
        "d_full": jnp.repeat(ssm_d, SSM_P, axis=1).reshape(DEPTH, 1, SSM_INNER),
        "w_mla_o": mla_w_o.astype(BF16),
        "ssm_norm": ssm_norm.reshape(DEPTH, 1, SSM_INNER),
        "w_ssm_o": ssm_w_o.astype(BF16),
        "gm_norm": gm_norm.reshape(DEPTH, 1, GM_WIDTH),
        "w_s": ws_pair,
        "b_s": jnp.repeat(jnp.swapaxes(gm_b_s, 1, 2), GM_GDIM, axis=2),
        "w_gm_o": gm_w_o.astype(BF16),
        "b_gate": b_gate.reshape(DEPTH, 1, N_BRANCH * D),
        "w_out": w_out.astype(BF16),
    }


def _rope_tables():
    rows = SEQ // GRID_W
    r = jnp.repeat(jnp.arange(rows, dtype=F32), GRID_W)
    c = jnp.tile(jnp.arange(GRID_W, dtype=F32), rows)
    n_freq = ROPE // 4
    inv = jnp.power(ROPE_BASE, -jnp.arange(n_freq, dtype=F32) / n_freq)
    ang = jnp.concatenate([r[:, None] * inv, c[:, None] * inv], axis=-1)
    cos, sin = jnp.cos(ang), jnp.sin(ang)
    cos = jnp.concatenate([jnp.ones((CTX, ROPE // 2), F32), cos], axis=0)
    sin = jnp.concatenate([jnp.zeros((CTX, ROPE // 2), F32), sin], axis=0)
    ones = jnp.ones((T, NOPE), F32)
    zeros_n = jnp.zeros((T, NOPE), F32)
    zeros_p = jnp.zeros((T, HEAD_PAD - NOPE - ROPE), F32)
    cos_t = jnp.concatenate([ones, cos, cos, zeros_p], axis=1)
    sin_t = jnp.concatenate([zeros_n, sin, sin, zeros_p], axis=1)
    return cos_t, sin_t


def kernel(x, c, ctx, c_ctx, w_ada, b_ada, norm_g, ffn1_w_in, ffn1_w_out, ffn2_w_in, ffn2_w_out, w_in, mla_q_norm,
           mla_w_uq, mla_kv_norm, mla_w_ukv, mla_w_o, ssm_conv_w, ssm_conv_b, ssm_a_log, ssm_dt_bias, ssm_d,
           ssm_norm, ssm_w_o, gm_norm, gm_w_s, gm_b_s, gm_w_o, b_gate, w_out, final_norm):
    c_all = jnp.concatenate([c, c_ctx[None, :], jnp.zeros((MOD_ROWS - BATCH - 1, D), F32)], axis=0)
    w = _prep_weights(w_in, mla_q_norm, mla_w_uq, mla_kv_norm, mla_w_ukv, mla_w_o, ssm_conv_w, ssm_conv_b, ssm_a_log,
                      ssm_dt_bias, ssm_d, ssm_norm, ssm_w_o, gm_norm, gm_w_s, gm_b_s, gm_w_o, b_gate, w_out)
    w["mods"] = _ada(c_all, w_ada, b_ada).reshape(DEPTH, MOD_ROWS, N_MOD, D)
    w["norm_g"] = norm_g.reshape(DEPTH * 3, 1, D)
    w["ffn_w_in"] = (ffn1_w_in.astype(BF16), ffn2_w_in.astype(BF16))
    w["ffn_w_out"] = (ffn1_w_out.astype(BF16), ffn2_w_out.astype(BF16))
    w["final_norm"] = final_norm.reshape(1, D)
    w["cos"], w["sin"] = _rope_tables()

    xs = (ctx, x)
    for l in range(DEPTH):
        last = l == DEPTH - 1
        t0 = 1 if last else 0
        xs = _ffn(xs, w, l)
        q, k, vt, xbc, dt, dtt, z, uv, gate = _proj(xs, w, l, ctx_queries=not last)
        a = _attn(q, k, vt, t0=t0)
        y = _ssd(xbc, dt, dtt, w, l)
        xs = _mixout(xs, a, y, z, uv, gate, w, l, t0=t0, final=last)
    return xs
```

```python
import functools
import math

import jax
import jax.numpy as jnp
from jax import lax
from jax.experimental import pallas as pl
from jax.experimental.pallas import tpu as pltpu

F32 = jnp.float32
BF16 = jnp.bfloat16

D = 1024
BATCH = 8
SEQ = 2048
DEPTH = 2
CTX = 256
T = CTX + SEQ
GRID_W = 64
EPS = 1e-6

HEADS = 8
NOPE = 64
ROPE = 32
VDIM = 64
Q_RANK = 256
KV_RANK = 128
ROPE_BASE = 10000.0
ATTN_SCALE = (NOPE + ROPE) ** -0.5
EXP2_SCALE = ATTN_SCALE * math.log2(math.e)
HEAD_PAD = 128
V_ROWS = 80

SSM_HEADS = 8
SSM_P = 64
SSM_INNER = SSM_HEADS * SSM_P
SSM_GROUPS = 2
SSM_N = 128
SSM_CONV = 5
CHUNK = 128
XBC = SSM_INNER + 2 * SSM_GROUPS * SSM_N
N_CHUNKS = T // CHUNK
CTX_CHUNKS = CTX // CHUNK
HALO = 8

GM_GROUPS = 8
GM_WIDTH = 512
GM_GDIM = GM_WIDTH // GM_GROUPS

D_FF = 2816
N_BRANCH = 3
N_MOD = 9
MOD_ROWS = 16
ADA_COLS = 2304
CTX_MOD_ROW = BATCH

C_KV = 0
C_KR = 128
C_KRR = 256
C_DT = 384
C_XBC = 512
C_Q = C_XBC + XBC
C_Z = C_Q + Q_RANK
C_UV = C_Z + SSM_INNER
C_GATE = C_UV + 2 * GM_WIDTH
C_END = C_GATE + N_BRANCH * D

TM = 256
KEY_BLOCK = 128
CONV_ROWS = 64
PROJ_PIECE = 512
VMEM_LIMIT = 56 * 1024 * 1024


def _rms(x, g):
    y = x * lax.rsqrt(jnp.mean(x * x, axis=-1, keepdims=True) + EPS)
    return y * g


def _silu(x):
    return x * jax.nn.sigmoid(x)


def _bdot(a, b):
    return jnp.dot(a.astype(BF16), b.astype(BF16), preferred_element_type=F32)


def _split3(a):
    a1 = a.astype(BF16)
    r1 = a - a1.astype(F32)
    a2 = r1.astype(BF16)
    a3 = (r1 - a2.astype(F32)).astype(BF16)
    return a1, a2, a3


def _dot_right01(a, m01):
    return sum(jnp.dot(p, m01, preferred_element_type=F32) for p in _split3(a))


def _dot_left01(m01, a):
    return sum(jnp.dot(m01, p, preferred_element_type=F32) for p in _split3(a))


def _const_spec(shape, l=None):
    nd = len(shape)
    if l is None:
        return pl.BlockSpec(shape, lambda *_: (0,) * nd, pipeline_mode=pl.Buffered(1))
    return pl.BlockSpec((None,) + shape, lambda *_: (l,) + (0,) * nd, pipeline_mode=pl.Buffered(1))


def _mod_spec(l, t0):
    return pl.BlockSpec((None, 1, N_MOD, D), lambda b, t: (l, jnp.where(t + t0 == 0, CTX_MOD_ROW, b), 0, 0))


def _row_spec(width, t0):
    return pl.BlockSpec((1, TM, width), lambda b, t: (b, t + t0, 0))


def _params(sem):
    return pltpu.CompilerParams(dimension_semantics=sem, vmem_limit_bytes=VMEM_LIMIT)


def _ada_kernel(c_ref, w_ref, b_ref, o_ref):
    s = _silu(c_ref[...])
    o_ref[0] = _bdot(s, w_ref[0]) + b_ref[0]


def _ada(c_all, w_ada, b_ada):
    tn = ADA_COLS
    nn = N_MOD * D // tn
    return pl.pallas_call(
        _ada_kernel,
        grid=(DEPTH, nn),
        in_specs=[
            pl.BlockSpec((MOD_ROWS, D), lambda l, j: (0, 0)),
            pl.BlockSpec((1, D, tn), lambda l, j: (l, 0, j)),
            pl.BlockSpec((1, 1, tn), lambda l, j: (l, 0, j)),
        ],
        out_specs=pl.BlockSpec((1, MOD_ROWS, tn), lambda l, j: (l, 0, j)),
        out_shape=jax.ShapeDtypeStruct((DEPTH, MOD_ROWS, N_MOD * D), F32),
        compiler_params=_params(("arbitrary", "arbitrary")),
        name="ada_mod",
    )(c_all, w_ada, b_ada.reshape(DEPTH, 1, N_MOD * D))


def _ffn_rows(x, m, mod0, g_ref, wi_ref, wo_ref, fin_ref=None):
    shift, scale, gate = m[mod0:mod0 + 1], m[mod0 + 1:mod0 + 2], m[mod0 + 2:mod0 + 3]
    h = _rms(x, g_ref[...]) * (1.0 + scale) + shift
    gu = jnp.dot(h.astype(BF16), wi_ref[...], preferred_element_type=F32)
    a = _silu(gu[:, :D_FF]) * gu[:, D_FF:]
    y = jnp.dot(a.astype(BF16), wo_ref[...], preferred_element_type=F32)
    out = x + (0.5 * gate) * y
    if fin_ref is not None:
        out = _rms(out, fin_ref[...])
    return out


def _ffn_kernel(*refs, split_input):
    if split_input:
        ctx_ref, x_ref, mod_ref, g_ref, wi_ref, wo_ref, o_ref = refs
        x = jnp.where(pl.program_id(1) == 0, ctx_ref[0], x_ref[0])
    else:
        x_ref, mod_ref, g_ref, wi_ref, wo_ref, o_ref = refs
        x = x_ref[0]
    o_ref[0] = _ffn_rows(x, mod_ref[0], 0, g_ref, wi_ref, wo_ref)


def _ffn(xs, w, l):
    split_input = isinstance(xs, tuple)
    nt = T // TM
    if split_input:
        x_specs = [pl.BlockSpec((1, TM, D), lambda b, t: (b, 0, 0)),
                   pl.BlockSpec((1, TM, D), lambda b, t: (b, jnp.maximum(t - 1, 0), 0))]
        xs_args = list(xs)
    else:
        x_specs = [_row_spec(D, 0)]
        xs_args = [xs]
    return pl.pallas_call(
        functools.partial(_ffn_kernel, split_input=split_input),
        grid=(BATCH, nt),
        in_specs=x_specs + [
            _mod_spec(l, 0),
            _const_spec((1, D), 3 * l),
            _const_spec((D, 2 * D_FF), l),
            _const_spec((D_FF, D), l),
        ],
        out_specs=_row_spec(D, 0),
        out_shape=jax.ShapeDtypeStruct((BATCH, T, D), F32),
        compiler_params=_params(("parallel", "parallel")),
        name="ffn",
    )(*xs_args, w["mods"], w["norm_g"], w["ffn_w_in"][0], w["ffn_w_out"][0])


def _proj_kernel(x_ref, xprev_ref, xnext_ref, mod_ref, g_ref, wkv_ref, wq_side_ref, cw_ref, cb_ref, kvn_ref, wkn_ref,
                 wvt_ref, one_ref, qn_ref, wq_ref, wqr_ref, cos_ref, sin_ref,
                 q_out, k_out, vt_out, xc_out, dt_out, dtt_out, z_out, uv_out, gate_out, *, ctx_queries):
    t = pl.program_id(1)
    m = mod_ref[0]
    xe = jnp.concatenate([xprev_ref[0], x_ref[0], xnext_ref[0]], axis=0)
    he = _rms(xe, g_ref[...]) * (1.0 + m[4:5]) + m[3:4]
    hb = he[HALO:HALO + TM].astype(BF16)
    he = he.astype(BF16)
    cos = cos_ref[...]
    sin = sin_ref[...]

    small = jnp.dot(hb, wkv_ref[:, C_KV:C_XBC], preferred_element_type=F32)

    xbc = jnp.dot(he, wkv_ref[:, C_XBC:C_Q], preferred_element_type=F32)
    row = lax.broadcasted_iota(jnp.int32, (TM + 2 * HALO, 1), 0)
    has_prev = t >= 2
    has_next = (t >= 1) & (t < T // TM - 1)
    keep = ((row >= HALO) | has_prev) & ((row < HALO + TM) | has_next)
    def conv_lane_tile(c0):
        for r0 in range(0, TM, CONV_ROWS):
            win = xbc[r0:r0 + CONV_ROWS + 2 * HALO, c0:c0 + 128]
            if r0 == 0 or r0 + CONV_ROWS == TM:
                win = jnp.where(keep[r0:r0 + CONV_ROWS + 2 * HALO], win, 0.0)
            acc = cb_ref[:, c0:c0 + 128]
            for k in range(SSM_CONV):
                shift = (SSM_CONV // 2 - k) % (CONV_ROWS + 2 * HALO)
                tap = win if shift == 0 else pltpu.roll(win, shift, axis=0)
                acc = acc + cw_ref[k:k + 1, c0:c0 + 128] * tap[HALO:HALO + CONV_ROWS]
            xc_out[0, r0:r0 + CONV_ROWS, c0:c0 + 128] = _silu(acc)

    def query_side():
        def proj(c0, c1):
            return jnp.dot(hb, wq_side_ref[:, c0 - C_Q:c1 - C_Q], preferred_element_type=F32)

        q_lat = proj(C_Q, C_Z)
        conv_tiles = list(range(0, XBC, 128))
        for out, base, width in ((z_out, C_Z, SSM_INNER), (uv_out, C_UV, 2 * GM_WIDTH), (gate_out, C_GATE, N_BRANCH * D)):
            for c0 in range(0, width, PROJ_PIECE):
                out[0, :, c0:c0 + PROJ_PIECE] = proj(base + c0, base + c0 + PROJ_PIECE).astype(out.dtype)
                if conv_tiles:
                    conv_lane_tile(conv_tiles.pop(0))
        assert not conv_tiles
        qn = _rms(q_lat, qn_ref[...]).astype(BF16)
        qa = jnp.dot(qn, wq_ref[...], preferred_element_type=F32)
        qb = jnp.dot(qn, wqr_ref[...], preferred_element_type=F32)
        for hd in range(HEADS):
            sl = slice(hd * HEAD_PAD, (hd + 1) * HEAD_PAD)
            q_out[0, :, sl] = (qa[:, sl] * cos + qb[:, sl] * sin).astype(BF16)

    def query_side_unused():
        for out in (q_out, z_out, uv_out, gate_out):
            out[...] = jnp.zeros_like(out)
        for c0 in range(0, XBC, 128):
            conv_lane_tile(c0)

    if ctx_queries:
        query_side()
    else:
        pl.when(t != 0)(query_side)
        pl.when(t == 0)(query_side_unused)

    dt = small[:, C_DT:C_DT + 128]
    dt_out[0] = dt
    dtt_out[0] = dt.T[:2 * SSM_HEADS]

    kvn = _rms(small[:, C_KV:C_KV + 128], kvn_ref[...]).astype(BF16)
    kn = jnp.dot(kvn, wkn_ref[...], preferred_element_type=F32)
    kr = small[:, C_KR:C_KR + 128] * cos + small[:, C_KRR:C_KRR + 128] * sin
    for hd in range(HEADS):
        sl = slice(hd * HEAD_PAD, (hd + 1) * HEAD_PAD)
        k_out[0, :, sl] = (kn[:, sl] + kr).astype(BF16)
    vt = lax.dot_general(wvt_ref[...], kvn, (((1,), (1,)), ((), ())), preferred_element_type=F32)
    vt_out[0] = (vt + one_ref[...]).astype(BF16)


def _proj(xs, w, l, *, ctx_queries):
    nt = T // TM

    def rows(w, dt):
        return jax.ShapeDtypeStruct((BATCH, T, w), dt), _row_spec(w, 0)

    def cols(r, dt):
        return jax.ShapeDtypeStruct((BATCH, r, T), dt), pl.BlockSpec((1, r, TM), lambda b, t: (b, 0, t))

    outs = [rows(HEADS * HEAD_PAD, BF16), rows(HEADS * HEAD_PAD, BF16), cols(HEADS * V_ROWS, BF16),
            rows(XBC, F32), rows(128, F32), cols(2 * SSM_HEADS, F32),
            rows(SSM_INNER, F32), rows(2 * GM_WIDTH, F32), rows(N_BRANCH * D, F32)]
    out_shape = [o[0] for o in outs]
    out_specs = [o[1] for o in outs]
    return pl.pallas_call(
        functools.partial(_proj_kernel, ctx_queries=ctx_queries),
        grid=(BATCH, nt),
        in_specs=[
            _row_spec(D, 0),
            pl.BlockSpec((1, HALO, D), lambda b, t: (b, jnp.maximum(t * (TM // HALO) - 1, 0), 0)),
            pl.BlockSpec((1, HALO, D), lambda b, t: (b, jnp.minimum((t + 1) * (TM // HALO), T // HALO - 1), 0)),
            _mod_spec(l, 0),
            _const_spec((1, D), 3 * l + 1),
            _const_spec((D, C_Q), l),
            _const_spec((D, C_END - C_Q), l),
            _const_spec((8, XBC), l),
            _const_spec((1, XBC), l),
            _const_spec((1, KV_RANK), l),
            _const_spec((KV_RANK, HEADS * HEAD_PAD), l),
            _const_spec((HEADS * V_ROWS, KV_RANK), l),
            _const_spec((HEADS * V_ROWS, 1)),
            _const_spec((1, Q_RANK), l),
            _const_spec((Q_RANK, HEADS * HEAD_PAD), l),
            _const_spec((Q_RANK, HEADS * HEAD_PAD), l),
            pl.BlockSpec((TM, HEAD_PAD), lambda b, t: (t, 0)),
            pl.BlockSpec((TM, HEAD_PAD), lambda b, t: (t, 0)),
        ],
        out_specs=out_specs,
        out_shape=out_shape,
        compiler_params=_params(("parallel", "parallel")),
        name="mixer_proj",
    )(xs, xs, xs, w["mods"], w["norm_g"], w["w_in_kv"], w["w_in_q"], w["conv_w"], w["conv_b"], w["kv_norm"], w["w_kn"],
      w["w_v_t"],
      w["v_ones"],
      w["q_norm"], w["w_q"], w["w_q_rot"], w["cos"], w["sin"])


def _attn_kernel(q_ref, k_ref, vt_ref, o_ref, *, t0):
    def run(nk):
        nblk = nk // KEY_BLOCK

        def score_block(hd, j):
            q = q_ref[0, :, hd * HEAD_PAD:(hd + 1) * HEAD_PAD]
            k = k_ref[0, j * KEY_BLOCK:(j + 1) * KEY_BLOCK, hd * HEAD_PAD:(hd + 1) * HEAD_PAD]
            return lax.dot_general(k, q, (((1,), (1,)), ((), ())), preferred_element_type=F32)

        def col_max(blocks):
            mx = functools.reduce(jnp.maximum, blocks)
            mx = jnp.max(mx.reshape(KEY_BLOCK // 8, 8, TM), axis=0)
            return jnp.max(mx, axis=0, keepdims=True)

        outs = []
        cur = [score_block(0, j) for j in range(nblk)]
        for hd in range(HEADS):
            mx = col_max(cur)
            nxt, ps = [], []
            for j in range(nblk):
                if hd + 1 < HEADS:
                    nxt.append(score_block(hd + 1, j))
                ps.append(jnp.exp2((cur[j] - mx) * EXP2_SCALE).astype(BF16))
            p = jnp.concatenate(ps, axis=0)
            ot = jnp.dot(vt_ref[0, hd * V_ROWS:(hd + 1) * V_ROWS, :nk], p, preferred_element_type=F32)
            outs.append(ot[:VDIM] / ot[VDIM:VDIM + 1])
            cur = nxt
        o_ref[0] = jnp.concatenate(outs, axis=0).T.astype(BF16)

    if t0 == 0:
        t = pl.program_id(1)
        pl.when(t == 0)(lambda: run(CTX))
        pl.when(t != 0)(lambda: run(T))
    else:
        run(T)


def _attn(q, k, v, *, t0):
    nt = T // TM - t0
    return pl.pallas_call(
        functools.partial(_attn_kernel, t0=t0),
        grid=(BATCH, nt),
        in_specs=[
            _row_spec(HEADS * HEAD_PAD, t0),
            pl.BlockSpec((1, T, HEADS * HEAD_PAD), lambda b, t: (b, 0, 0)),
            pl.BlockSpec((1, HEADS * V_ROWS, T), lambda b, t: (b, 0, 0)),
        ],
        out_specs=pl.BlockSpec((1, TM, HEADS * VDIM), lambda b, t: (b, t, 0)),
        out_shape=jax.ShapeDtypeStruct((BATCH, nt * TM, HEADS * VDIM), BF16),
        compiler_params=_params(("parallel", "arbitrary")),
        name="mla_attn",
    )(q, k, v)


def _ssd_kernel(xc_ref, dt_ref, dtt_ref, bias_row_ref, bias_col_ref, alog_row_ref, alog_col_ref, dfull_ref,
                y_ref, h_ref, dcol_ref, drow_ref, acol_ref, arow_ref):
    row = lax.broadcasted_iota(jnp.int32, (CHUNK, CHUNK), 0)
    col = lax.broadcasted_iota(jnp.int32, (CHUNK, CHUNK), 1)
    lower = (col <= row)
    upper = (col >= row)
    lower01 = jnp.where(lower, 1.0, 0.0).astype(BF16)
    upper01 = jnp.where(upper, 1.0, 0.0).astype(BF16)
    lo_half = lax.broadcasted_iota(jnp.int32, (CHUNK, 128), 1) < SSM_P

    chunks = range(N_CHUNKS)
    dcol_all = jax.nn.softplus(dt_ref[0] + bias_row_ref[...])
    drow_all = jax.nn.softplus(dtt_ref[0] + bias_col_ref[...])
    dcol_ref[...] = dcol_all
    drow_ref[...] = drow_all
    a_col_form = dcol_all * -jnp.exp(alog_row_ref[...])
    a_wide = jnp.concatenate([a_col_form[c * CHUNK:(c + 1) * CHUNK] for c in chunks], axis=1)
    causal_lane = lax.broadcasted_iota(jnp.int32, (CHUNK, N_CHUNKS * 128), 1) % 128 < SSM_HEADS
    acol_wide = jnp.where(causal_lane, _dot_left01(lower01, a_wide), _dot_left01(upper01, a_wide))
    a_row_form = drow_all * -jnp.exp(alog_col_ref[...])
    a_tall = jnp.concatenate([a_row_form[:, c * CHUNK:(c + 1) * CHUNK] for c in chunks], axis=0)
    causal_row = lax.broadcasted_iota(jnp.int32, (N_CHUNKS * 2 * SSM_HEADS, CHUNK), 0) % (2 * SSM_HEADS) < SSM_HEADS
    arow_tall = jnp.where(causal_row, _dot_right01(a_tall, upper01), _dot_right01(a_tall, lower01))
    for c in chunks:
        acol_ref[c] = acol_wide[:, c * 128:(c + 1) * 128]
        arow_ref[c] = arow_tall[c * 2 * SSM_HEADS:(c + 1) * 2 * SSM_HEADS]

    def chunk_step(c, d):
        r0 = pl.multiple_of(c * CHUNK, CHUNK)
        xc = xc_ref[0, pl.ds(r0, CHUNK), :]
        dcol = dcol_ref[pl.ds(r0, CHUNK), :]
        drow = drow_ref[:, pl.ds(r0, CHUNK)]
        mask = lower if d == 0 else upper
        acum_c = acol_ref[c]
        acum_r = arow_ref[c]
        atot_c = acum_c[CHUNK - 1:CHUNK] if d == 0 else acum_c[0:1]
        w_c = jnp.exp(atot_c - acum_c) * dcol
        etot_c = jnp.exp(atot_c)

        def head_col(arr, hd):
            lane = d * SSM_HEADS + hd
            return arr[:, lane:lane + 1]

        def pair_lanes(arr, p):
            return jnp.where(lo_half[:arr.shape[0]], head_col(arr, 2 * p), head_col(arr, 2 * p + 1))

        ys = []
        for g in range(SSM_GROUPS):
            bm = xc[:, SSM_INNER + g * SSM_N:SSM_INNER + (g + 1) * SSM_N]
            cm = xc[:, SSM_INNER + (SSM_GROUPS + g) * SSM_N:SSM_INNER + (SSM_GROUPS + g + 1) * SSM_N]
            bmb = bm.astype(BF16)
            cmb = cm.astype(BF16)
            cb = lax.dot_general(cmb, bmb, (((1,), (1,)), ((), ())), preferred_element_type=F32)
            bt = bm.T.astype(BF16)
            for pp in range(2):
                p = g * 2 + pp
                sc, a_bc = [], []
                for hh in range(2):
                    hd = 2 * p + hh
                    a_bc.append(jnp.broadcast_to(head_col(acum_c, hd), (CHUNK, CHUNK)))
                    diff = a_bc[hh] - acum_r[d * SSM_HEADS + hd:d * SSM_HEADS + hd + 1]
                    dec = jnp.exp(jnp.where(mask, diff, -jnp.inf))
                    sc.append(cb * dec * drow[d * SSM_HEADS + hd:d * SSM_HEADS + hd + 1])
                x_p = xc[:, p * 128:(p + 1) * 128]
                xw = (x_p * pair_lanes(w_c, p)).astype(BF16)
                s_t = jnp.dot(bt, xw, preferred_element_type=F32)
                h_prev = h_ref[d, p]
                e_p = jnp.exp(jnp.where(lo_half, a_bc[0], a_bc[1]))
                y_inter = jnp.dot(cmb, h_prev.astype(BF16), preferred_element_type=F32) * e_p
                pmat = jnp.concatenate(sc, axis=1).astype(BF16)
                xbd = jnp.concatenate([jnp.where(lo_half, x_p, 0.0), jnp.where(lo_half, 0.0, x_p)],
                                      axis=0).astype(BF16)
                ys.append(jnp.dot(pmat, xbd, preferred_element_type=F32) + y_inter)
                h_ref[d, p] = h_prev * pair_lanes(etot_c, p) + s_t
        y = jnp.concatenate(ys, axis=1)
        if d == 0:
            y = y + dfull_ref[...] * xc[:, :SSM_INNER]
        y_ref[0, pl.ds(r0, CHUNK), :] += y

    h_ref[...] = jnp.zeros_like(h_ref)
    y_ref[...] = jnp.zeros_like(y_ref)

    def scan_body(i, carry):
        chunk_step(i, 0)
        chunk_step(jnp.where(i < CTX_CHUNKS, CTX_CHUNKS - 1 - i, N_CHUNKS + CTX_CHUNKS - 1 - i), 1)
        return carry

    lax.fori_loop(0, N_CHUNKS, scan_body, 0, unroll=2)


def _ssd(xc, dt, dtt, w, l):
    return pl.pallas_call(
        _ssd_kernel,
        grid=(BATCH,),
        in_specs=[
            pl.BlockSpec((1, T, XBC), lambda b: (b, 0, 0)),
            pl.BlockSpec((1, T, 128), lambda b: (b, 0, 0)),
            pl.BlockSpec((1, 2 * SSM_HEADS, T), lambda b: (b, 0, 0)),
            _const_spec((1, 128), l),
            _const_spec((2 * SSM_HEADS, 1), l),
            _const_spec((1, 128), l),
            _const_spec((2 * SSM_HEADS, 1), l),
            _const_spec((1, SSM_INNER), l),
        ],
        out_specs=pl.BlockSpec((1, T, SSM_INNER), lambda b: (b, 0, 0)),
        out_shape=jax.ShapeDtypeStruct((BATCH, T, SSM_INNER), F32),
        scratch_shapes=[
            pltpu.VMEM((2, SSM_HEADS // 2, SSM_N, 2 * SSM_P), F32),
            pltpu.VMEM((T, 128), F32),
            pltpu.VMEM((2 * SSM_HEADS, T), F32),
            pltpu.VMEM((N_CHUNKS, CHUNK, 128), F32),
            pltpu.VMEM((N_CHUNKS, 2 * SSM_HEADS, CHUNK), F32),
        ],
        compiler_params=_params(("arbitrary",)),
        name="ssd_bidir",
    )(xc, dt, dtt, w["dt_bias_row"], w["dt_bias_col"], w["a_log_row"], w["a_log_col"], w["d_full"])


def _mixout_kernel(x_ref, mod_ref, a_ref, y_ref, z_ref, uv_ref, gate_ref,
                   wmo_ref, sn_ref, wso_ref, gn_ref, ws_ref, bs_ref, wgo_ref, bg_ref, wout_ref,
                   g2_ref, wi_ref, wo_ref, fin_ref, o_ref, *, final):
    x = x_ref[0]
    m = mod_ref[0]
    o_mla = jnp.dot(a_ref[0], wmo_ref[...], preferred_element_type=F32)

    ys = y_ref[0] * _silu(z_ref[0].astype(F32))
    o_ssm = _bdot(_rms(ys, sn_ref[...]), wso_ref[...])

    uv = jax.nn.gelu(uv_ref[0].astype(F32))
    u = uv[:, :GM_WIDTH]
    v = uv[:, GM_WIDTH:]
    v = v - jnp.mean(v, axis=-1, keepdims=True)
    v = v * lax.rsqrt(jnp.mean(v * v, axis=-1, keepdims=True) + EPS) * gn_ref[...]
    lo_half = lax.broadcasted_iota(jnp.int32, (CHUNK, 128), 1) < GM_GDIM
    mixed = []
    for ch in range(TM // CHUNK):
        parts = []
        for p in range(GM_GROUPS // 2):
            vp = v[ch * CHUNK:(ch + 1) * CHUNK, p * 128:(p + 1) * 128]
            vbd = jnp.concatenate([jnp.where(lo_half, vp, 0.0), jnp.where(lo_half, 0.0, vp)], axis=0)
            parts.append(jnp.dot(ws_ref[p], vbd.astype(BF16), preferred_element_type=F32))
        mixed.append(jnp.concatenate(parts, axis=1) + bs_ref[...])
    mixed = jnp.concatenate(mixed, axis=0)
    o_gm = _bdot(u * mixed, wgo_ref[...])

    gts = jax.nn.sigmoid(gate_ref[0].astype(F32) + bg_ref[...])
    merged = gts[:, :D] * o_mla + gts[:, D:2 * D] * o_ssm + gts[:, 2 * D:] * o_gm
    y = _bdot(merged, wout_ref[...])
    x = x + m[5:6] * y
    o_ref[0] = _ffn_rows(x, m, 6, g2_ref, wi_ref, wo_ref, fin_ref if final else None)


def _mixout(xs, a, y, z, uv, gate, w, l, *, t0, final):
    nt = T // TM - t0
    return pl.pallas_call(
        functools.partial(_mixout_kernel, final=final),
        grid=(BATCH, nt),
        in_specs=[
            _row_spec(D, t0),
            _mod_spec(l, t0),
            pl.BlockSpec((1, TM, HEADS * VDIM), lambda b, t: (b, t, 0)),
            _row_spec(SSM_INNER, t0),
            _row_spec(SSM_INNER, t0),
            _row_spec(2 * GM_WIDTH, t0),
            _row_spec(N_BRANCH * D, t0),
            _const_spec((HEADS * VDIM, D), l),
            _const_spec((1, SSM_INNER), l),
            _const_spec((SSM_INNER, D), l),
            _const_spec((1, GM_WIDTH), l),
            _const_spec((GM_GROUPS // 2, CHUNK, 2 * CHUNK), l),
            _const_spec((CHUNK, GM_WIDTH), l),
            _const_spec((GM_WIDTH, D), l),
            _const_spec((1, N_BRANCH * D), l),
            _const_spec((D, D), l),
            _const_spec((1, D), 3 * l + 2),
            _const_spec((D, 2 * D_FF), l),
            _const_spec((D_FF, D), l),
            _const_spec((1, D)),
        ],
        out_specs=pl.BlockSpec((1, TM, D), lambda b, t: (b, t, 0)),
        out_shape=jax.ShapeDtypeStruct((BATCH, nt * TM, D), F32),
        compiler_params=_params(("parallel", "parallel")),
        name="mixer_out_ffn",
    )(xs, w["mods"], a, y, z, uv, gate, w["w_mla_o"], w["ssm_norm"], w["w_ssm_o"], w["gm_norm"], w["w_s"],
      w["b_s"], w["w_gm_o"], w["b_gate"], w["w_out"], w["norm_g"], w["ffn_w_in"][1], w["ffn_w_out"][1],
      w["final_norm"])


def _rot_half(w):
    half = ROPE // 2
    return jnp.concatenate([-w[..., half:], w[..., :half]], axis=-1)


def _head_pad(nope, rope):
    pad = jnp.zeros(nope.shape[:-1] + (HEAD_PAD - NOPE - ROPE,), nope.dtype)
    out = jnp.concatenate([nope, rope, pad], axis=-1)
    return out.reshape(out.shape[:-2] + (HEADS * HEAD_PAD,))


def _prep_weights(w_in, mla_q_norm, mla_w_uq, mla_kv_norm, mla_w_ukv, mla_w_o, ssm_conv_w, ssm_conv_b,
                  ssm_a_log, ssm_dt_bias, ssm_d, ssm_norm, ssm_w_o, gm_norm, gm_w_s, gm_b_s, gm_w_o, b_gate, w_out):
    w = w_in
    o = 0
    kv = w[..., o:o + KV_RANK]; o += KV_RANK
    kr = w[..., o:o + ROPE]; o += ROPE
    xbc = w[..., o:o + XBC]; o += XBC
    dtw = w[..., o:o + 2 * SSM_HEADS]; o += 2 * SSM_HEADS
    rest = w[..., o:]

    def lanes(cols, start):
        return jnp.pad(cols, ((0, 0), (0, 0), (start, 128 - start - cols.shape[-1])))

    pieces = [kv, lanes(kr, NOPE), lanes(_rot_half(kr), NOPE), lanes(dtw, 0), xbc]
    w_kv_side = jnp.concatenate([p.astype(BF16) for p in pieces], axis=-1)

    uq = mla_w_uq.astype(BF16).reshape(DEPTH, Q_RANK, HEADS, NOPE + ROPE)
    uq_n, uq_r = uq[..., :NOPE], uq[..., NOPE:]
    ukv = mla_w_ukv.astype(BF16).reshape(DEPTH, KV_RANK, HEADS, NOPE + VDIM)
    kn = _head_pad(ukv[..., :NOPE], jnp.zeros((DEPTH, KV_RANK, HEADS, ROPE), BF16))
    v_t = jnp.pad(jnp.transpose(ukv[..., NOPE:], (0, 2, 3, 1)), ((0, 0), (0, 0), (0, V_ROWS - VDIM), (0, 0)))
    v_ones = jnp.tile((jnp.arange(V_ROWS) == VDIM).astype(F32), HEADS).reshape(HEADS * V_ROWS, 1)

    ws = gm_w_s.astype(BF16)
    ws_pair = jnp.concatenate([ws[:, 0::2], ws[:, 1::2]], axis=3)

    def row128(vals, fill=0.0):
        vals = vals.reshape(DEPTH, 1, -1)
        return jnp.pad(vals, ((0, 0), (0, 0), (0, 128 - vals.shape[-1])), constant_values=fill)

    return {
        "w_in_kv": w_kv_side,
        "w_in_q": rest.astype(BF16),
        "kv_norm": mla_kv_norm.reshape(DEPTH, 1, KV_RANK),
        "w_kn": kn,
        "w_v_t": v_t.reshape(DEPTH, HEADS * V_ROWS, KV_RANK),
        "v_ones": v_ones,
        "q_norm": mla_q_norm.reshape(DEPTH, 1, Q_RANK),
        "w_q": _head_pad(uq_n, uq_r),
        "w_q_rot": _head_pad(jnp.zeros_like(uq_n), _rot_half(uq_r)),
        "conv_w": jnp.pad(jnp.swapaxes(ssm_conv_w, 1, 2), ((0, 0), (0, 8 - SSM_CONV), (0, 0))),
        "conv_b": ssm_conv_b.reshape(DEPTH, 1, XBC),
        "dt_bias_row": row128(ssm_dt_bias),
        "dt_bias_col": ssm_dt_bias.reshape(DEPTH, 2 * SSM_HEADS, 1),
        "a_log_row": row128(ssm_a_log, fill=-80.0),
        "a_log_col": ssm_a_log.reshape(DEPTH, 2 * SSM_HEADS, 1),
        "d_full": jnp.repeat(ssm_d, SSM_P, axis=1).reshape(DEPTH, 1, SSM_INNER),
        "w_mla_o": mla_w_o.astype(BF16),
        "ssm_norm": ssm_norm.reshape(DEPTH, 1, SSM_INNER),
        "w_ssm_o": ssm_w_o.astype(BF16),
        "gm_norm": gm_norm.reshape(DEPTH, 1, GM_WIDTH),
        "w_s": ws_pair,
        "b_s": jnp.repeat(jnp.swapaxes(gm_b_s, 1, 2), GM_GDIM, axis=2),
        "w_gm_o": gm_w_o.astype(BF16),
        "b_gate": b_gate.reshape(DEPTH, 1, N_BRANCH * D),
        "w_out": w_out.astype(BF16),
    }


def _rope_tables():
    rows = SEQ // GRID_W
    r = jnp.repeat(jnp.arange(rows, dtype=F32), GRID_W)
    c = jnp.tile(jnp.arange(GRID_W, dtype=F32), rows)
    n_freq = ROPE // 4
    inv = jnp.power(ROPE_BASE, -jnp.arange(n_freq, dtype=F32) / n_freq)
    ang = jnp.concatenate([r[:, None] * inv, c[:, None] * inv], axis=-1)
    cos, sin = jnp.cos(ang), jnp.sin(ang)
    cos = jnp.concatenate([jnp.ones((CTX, ROPE // 2), F32), cos], axis=0)
    sin = jnp.concatenate([jnp.zeros((CTX, ROPE // 2), F32), sin], axis=0)
    ones = jnp.ones((T, NOPE), F32)
    zeros_n = jnp.zeros((T, NOPE), F32)
    zeros_p = jnp.zeros((T, HEAD_PAD - NOPE - ROPE), F32)
    cos_t = jnp.concatenate([ones, cos, cos, zeros_p], axis=1)
    sin_t = jnp.concatenate([zeros_n, sin, sin, zeros_p], axis=1)
    return cos_t, sin_t


def kernel(x, c, ctx, c_ctx, w_ada, b_ada, norm_g, ffn1_w_in, ffn1_w_out, ffn2_w_in, ffn2_w_out, w_in, mla_q_norm,
           mla_w_uq, mla_kv_norm, mla_w_ukv, mla_w_o, ssm_conv_w, ssm_conv_b, ssm_a_log, ssm_dt_bias, ssm_d,
           ssm_norm, ssm_w_o, gm_norm, gm_w_s, gm_b_s, gm_w_o, b_gate, w_out, final_norm):
    c_all = jnp.concatenate([c, c_ctx[None, :], jnp.zeros((MOD_ROWS - BATCH - 1, D), F32)], axis=0)
    w = _prep_weights(w_in, mla_q_norm, mla_w_uq, mla_kv_norm, mla_w_ukv, mla_w_o, ssm_conv_w, ssm_conv_b, ssm_a_log,
                      ssm_dt_bias, ssm_d, ssm_norm, ssm_w_o, gm_norm, gm_w_s, gm_b_s, gm_w_o, b_gate, w_out)
    w["mods"] = _ada(c_all, w_ada, b_ada).reshape(DEPTH, MOD_ROWS, N_MOD, D)
    w["norm_g"] = norm_g.reshape(DEPTH * 3, 1, D)
    w["ffn_w_in"] = (ffn1_w_in.astype(BF16), ffn2_w_in.astype(BF16))
    w["ffn_w_out"] = (ffn1_w_out.astype(BF16), ffn2_w_out.astype(BF16))
    w["final_norm"] = final_norm.reshape(1, D)
    w["cos"], w["sin"] = _rope_tables()

    xs = (ctx, x)
    for l in range(DEPTH):
        last = l == DEPTH - 1
        t0 = 1 if last else 0
        xs = _ffn(xs, w, l)
        q, k, vt, xbc, dt, dtt, z, uv, gate = _proj(xs, w, l, ctx_queries=not last)
        a = _attn(q, k, vt, t0=t0)
        y = _ssd(xbc, dt, dtt, w, l)
        xs = _mixout(xs, a, y, z, uv, gate, w, l, t0=t0, final=last)
    return xs
```

```python
import functools
import math

import jax
import jax.numpy as jnp
from jax import lax
from jax.experimental import pallas as pl
from jax.experimental.pallas import tpu as pltpu

F32 = jnp.float32
BF16 = jnp.bfloat16

D = 1024
BATCH = 8
SEQ = 2048
DEPTH = 2
CTX = 256
T = CTX + SEQ
GRID_W = 64
EPS = 1e-6

HEADS = 8
NOPE = 64
ROPE = 32
VDIM = 64
Q_RANK = 256
KV_RANK = 128
ROPE_BASE = 10000.0
ATTN_SCALE = (NOPE + ROPE) ** -0.5
EXP2_SCALE = ATTN_SCALE * math.log2(math.e)
HEAD_PAD = 128
V_ROWS = 80

SSM_HEADS = 8
SSM_P = 64
SSM_INNER = SSM_HEADS * SSM_P
SSM_GROUPS = 2
SSM_N = 128
SSM_CONV = 5
CHUNK = 128
XBC = SSM_INNER + 2 * SSM_GROUPS * SSM_N
N_CHUNKS = T // CHUNK
CTX_CHUNKS = CTX // CHUNK
HALO = 8

GM_GROUPS = 8
GM_WIDTH = 512
GM_GDIM = GM_WIDTH // GM_GROUPS

D_FF = 2816
N_BRANCH = 3
N_MOD = 9
MOD_ROWS = 16
ADA_COLS = 2304
CTX_MOD_ROW = BATCH

C_KV = 0
C_KR = 128
C_KRR = 256
C_DT = 384
C_XBC = 512
C_Q = C_XBC + XBC
C_Z = C_Q + Q_RANK
C_UV = C_Z + SSM_INNER
C_GATE = C_UV + 2 * GM_WIDTH
C_END = C_GATE + N_BRANCH * D

TM = 256
KEY_BLOCK = 128
CONV_ROWS = 64
PROJ_PIECE = 512
VMEM_LIMIT = 56 * 1024 * 1024


def _rms(x, g):
    y = x * lax.rsqrt(jnp.mean(x * x, axis=-1, keepdims=True) + EPS)
    return y * g


def _silu(x):
    return x * jax.nn.sigmoid(x)


def _bdot(a, b):
    return jnp.dot(a.astype(BF16), b.astype(BF16), preferred_element_type=F32)


def _split3(a):
    a1 = a.astype(BF16)
    r1 = a - a1.astype(F32)
    a2 = r1.astype(BF16)
    a3 = (r1 - a2.astype(F32)).astype(BF16)
    return a1, a2, a3


def _dot_right01(a, m01):
    return sum(jnp.dot(p, m01, preferred_element_type=F32) for p in _split3(a))


def _const_spec(shape, l=None):
    nd = len(shape)
    if l is None:
        return pl.BlockSpec(shape, lambda *_: (0,) * nd, pipeline_mode=pl.Buffered(1))
    return pl.BlockSpec((None,) + shape, lambda *_: (l,) + (0,) * nd, pipeline_mode=pl.Buffered(1))


def _mod_spec(l, t0):
    return pl.BlockSpec((None, 1, N_MOD, D), lambda b, t: (l, jnp.where(t + t0 == 0, CTX_MOD_ROW, b), 0, 0))


def _row_spec(width, t0):
    return pl.BlockSpec((1, TM, width), lambda b, t: (b, t + t0, 0))


def _params(sem):
    return pltpu.CompilerParams(dimension_semantics=sem, vmem_limit_bytes=VMEM_LIMIT)


def _ada_kernel(c_ref, w_ref, b_ref, o_ref):
    s = _silu(c_ref[...])
    o_ref[0] = _bdot(s, w_ref[0]) + b_ref[0]


def _ada(c_all, w_ada, b_ada):
    tn = ADA_COLS
    nn = N_MOD * D // tn
    return pl.pallas_call(
        _ada_kernel,
        grid=(DEPTH, nn),
        in_specs=[
            pl.BlockSpec((MOD_ROWS, D), lambda l, j: (0, 0)),
            pl.BlockSpec((1, D, tn), lambda l, j: (l, 0, j)),
            pl.BlockSpec((1, 1, tn), lambda l, j: (l, 0, j)),
        ],
        out_specs=pl.BlockSpec((1, MOD_ROWS, tn), lambda l, j: (l, 0, j)),
        out_shape=jax.ShapeDtypeStruct((DEPTH, MOD_ROWS, N_MOD * D), F32),
        compiler_params=_params(("arbitrary", "arbitrary")),
        name="ada_mod",
    )(c_all, w_ada, b_ada.reshape(DEPTH, 1, N_MOD * D))


def _ffn_rows(x, m, mod0, g_ref, wi_ref, wo_ref, fin_ref=None):
    shift, scale, gate = m[mod0:mod0 + 1], m[mod0 + 1:mod0 + 2], m[mod0 + 2:mod0 + 3]
    h = _rms(x, g_ref[...]) * (1.0 + scale) + shift
    gu = jnp.dot(h.astype(BF16), wi_ref[...], preferred_element_type=F32)
    a = _silu(gu[:, :D_FF]) * gu[:, D_FF:]
    y = jnp.dot(a.astype(BF16), wo_ref[...], preferred_element_type=F32)
    out = x + (0.5 * gate) * y
    if fin_ref is not None:
        out = _rms(out, fin_ref[...])
    return out


def _ffn_kernel(*refs, split_input):
    if split_input:
        ctx_ref, x_ref, mod_ref, g_ref, wi_ref, wo_ref, o_ref = refs
        x = jnp.where(pl.program_id(1) == 0, ctx_ref[0], x_ref[0])
    else:
        x_ref, mod_ref, g_ref, wi_ref, wo_ref, o_ref = refs
        x = x_ref[0]
    o_ref[0] = _ffn_rows(x, mod_ref[0], 0, g_ref, wi_ref, wo_ref)


def _ffn(xs, w, l):
    split_input = isinstance(xs, tuple)
    nt = T // TM
    if split_input:
        x_specs = [pl.BlockSpec((1, TM, D), lambda b, t: (b, 0, 0)),
                   pl.BlockSpec((1, TM, D), lambda b, t: (b, jnp.maximum(t - 1, 0), 0))]
        xs_args = list(xs)
    else:
        x_specs = [_row_spec(D, 0)]
        xs_args = [xs]
    return pl.pallas_call(
        functools.partial(_ffn_kernel, split_input=split_input),
        grid=(BATCH, nt),
        in_specs=x_specs + [
            _mod_spec(l, 0),
            _const_spec((1, D), 3 * l),
            _const_spec((D, 2 * D_FF), l),
            _const_spec((D_FF, D), l),
        ],
        out_specs=_row_spec(D, 0),
        out_shape=jax.ShapeDtypeStruct((BATCH, T, D), F32),
        compiler_params=_params(("parallel", "parallel")),
        name="ffn",
    )(*xs_args, w["mods"], w["norm_g"], w["ffn_w_in"][0], w["ffn_w_out"][0])


def _proj_kernel(x_ref, xprev_ref, xnext_ref, mod_ref, g_ref, wkv_ref, wq_side_ref, cw_ref, cb_ref, kvn_ref, wkn_ref,
                 wvt_ref, one_ref, qn_ref, wq_ref, wqr_ref, cos_ref, sin_ref,
                 q_out, k_out, vt_out, xc_out, dt_out, dtt_out, z_out, uv_out, gate_out, *, ctx_queries):
    t = pl.program_id(1)
    m = mod_ref[0]
    xe = jnp.concatenate([xprev_ref[0], x_ref[0], xnext_ref[0]], axis=0)
    he = _rms(xe, g_ref[...]) * (1.0 + m[4:5]) + m[3:4]
    hb = he[HALO:HALO + TM].astype(BF16)
    he = he.astype(BF16)
    cos = cos_ref[...]
    sin = sin_ref[...]

    small = jnp.dot(hb, wkv_ref[:, C_KV:C_XBC], preferred_element_type=F32)

    xbc = jnp.dot(he, wkv_ref[:, C_XBC:C_Q], preferred_element_type=F32)
    row = lax.broadcasted_iota(jnp.int32, (TM + 2 * HALO, 1), 0)
    has_prev = t >= 2
    has_next = (t >= 1) & (t < T // TM - 1)
    keep = ((row >= HALO) | has_prev) & ((row < HALO + TM) | has_next)
    def conv_lane_tile(c0):
        for r0 in range(0, TM, CONV_ROWS):
            win = xbc[r0:r0 + CONV_ROWS + 2 * HALO, c0:c0 + 128]
            if r0 == 0 or r0 + CONV_ROWS == TM:
                win = jnp.where(keep[r0:r0 + CONV_ROWS + 2 * HALO], win, 0.0)
            acc = cb_ref[:, c0:c0 + 128]
            for k in range(SSM_CONV):
                shift = (SSM_CONV // 2 - k) % (CONV_ROWS + 2 * HALO)
                tap = win if shift == 0 else pltpu.roll(win, shift, axis=0)
                acc = acc + cw_ref[k:k + 1, c0:c0 + 128] * tap[HALO:HALO + CONV_ROWS]
            xc_out[0, r0:r0 + CONV_ROWS, c0:c0 + 128] = _silu(acc)

    def query_side():
        def proj(c0, c1):
            return jnp.dot(hb, wq_side_ref[:, c0 - C_Q:c1 - C_Q], preferred_element_type=F32)

        q_lat = proj(C_Q, C_Z)
        conv_tiles = list(range(0, XBC, 128))
        for out, base, width in ((z_out, C_Z, SSM_INNER), (uv_out, C_UV, 2 * GM_WIDTH), (gate_out, C_GATE, N_BRANCH * D)):
            for c0 in range(0, width, PROJ_PIECE):
                out[0, :, c0:c0 + PROJ_PIECE] = proj(base + c0, base + c0 + PROJ_PIECE).astype(out.dtype)
                if conv_tiles:
                    conv_lane_tile(conv_tiles.pop(0))
        assert not conv_tiles
        qn = _rms(q_lat, qn_ref[...]).astype(BF16)
        qa = jnp.dot(qn, wq_ref[...], preferred_element_type=F32)
        qb = jnp.dot(qn, wqr_ref[...], preferred_element_type=F32)
        for hd in range(HEADS):
            sl = slice(hd * HEAD_PAD, (hd + 1) * HEAD_PAD)
            q_out[0, :, sl] = (qa[:, sl] * cos + qb[:, sl] * sin).astype(BF16)

    def query_side_unused():
        for out in (q_out, z_out, uv_out, gate_out):
            out[...] = jnp.zeros_like(out)
        for c0 in range(0, XBC, 128):
            conv_lane_tile(c0)

    if ctx_queries:
        query_side()
    else:
        pl.when(t != 0)(query_side)
        pl.when(t == 0)(query_side_unused)

    dt = small[:, C_DT:C_DT + 128]
    dt_out[0] = dt
    dtt_out[0] = dt.T[:2 * SSM_HEADS]

    kvn = _rms(small[:, C_KV:C_KV + 128], kvn_ref[...]).astype(BF16)
    kn = jnp.dot(kvn, wkn_ref[...], preferred_element_type=F32)
    kr = small[:, C_KR:C_KR + 128] * cos + small[:, C_KRR:C_KRR + 128] * sin
    for hd in range(HEADS):
        sl = slice(hd * HEAD_PAD, (hd + 1) * HEAD_PAD)
        k_out[0, :, sl] = (kn[:, sl] + kr).astype(BF16)
    vt = lax.dot_general(wvt_ref[...], kvn, (((1,), (1,)), ((), ())), preferred_element_type=F32)
    vt_out[0] = (vt + one_ref[...]).astype(BF16)


def _proj(xs, w, l, *, ctx_queries):
    nt = T // TM

    def rows(w, dt):
        return jax.ShapeDtypeStruct((BATCH, T, w), dt), _row_spec(w, 0)

    def cols(r, dt):
        return jax.ShapeDtypeStruct((BATCH, r, T), dt), pl.BlockSpec((1, r, TM), lambda b, t: (b, 0, t))

    outs = [rows(HEADS * HEAD_PAD, BF16), rows(HEADS * HEAD_PAD, BF16), cols(HEADS * V_ROWS, BF16),
            rows(XBC, F32), rows(128, F32), cols(2 * SSM_HEADS, F32),
            rows(SSM_INNER, F32), rows(2 * GM_WIDTH, F32), rows(N_BRANCH * D, F32)]
    out_shape = [o[0] for o in outs]
    out_specs = [o[1] for o in outs]
    return pl.pallas_call(
        functools.partial(_proj_kernel, ctx_queries=ctx_queries),
        grid=(BATCH, nt),
        in_specs=[
            _row_spec(D, 0),
            pl.BlockSpec((1, HALO, D), lambda b, t: (b, jnp.maximum(t * (TM // HALO) - 1, 0), 0)),
            pl.BlockSpec((1, HALO, D), lambda b, t: (b, jnp.minimum((t + 1) * (TM // HALO), T // HALO - 1), 0)),
            _mod_spec(l, 0),
            _const_spec((1, D), 3 * l + 1),
            _const_spec((D, C_Q), l),
            _const_spec((D, C_END - C_Q), l),
            _const_spec((8, XBC), l),
            _const_spec((1, XBC), l),
            _const_spec((1, KV_RANK), l),
            _const_spec((KV_RANK, HEADS * HEAD_PAD), l),
            _const_spec((HEADS * V_ROWS, KV_RANK), l),
            _const_spec((HEADS * V_ROWS, 1)),
            _const_spec((1, Q_RANK), l),
            _const_spec((Q_RANK, HEADS * HEAD_PAD), l),
            _const_spec((Q_RANK, HEADS * HEAD_PAD), l),
            pl.BlockSpec((TM, HEAD_PAD), lambda b, t: (t, 0)),
            pl.BlockSpec((TM, HEAD_PAD), lambda b, t: (t, 0)),
        ],
        out_specs=out_specs,
        out_shape=out_shape,
        compiler_params=_params(("parallel", "parallel")),
        name="mixer_proj",
    )(xs, xs, xs, w["mods"], w["norm_g"], w["w_in_kv"], w["w_in_q"], w["conv_w"], w["conv_b"], w["kv_norm"], w["w_kn"],
      w["w_v_t"],
      w["v_ones"],
      w["q_norm"], w["w_q"], w["w_q_rot"], w["cos"], w["sin"])


def _attn_kernel(q_ref, k_ref, vt_ref, o_ref, *, t0):
    def run(nk):
        nblk = nk // KEY_BLOCK

        def score_block(hd, j):
            q = q_ref[0, :, hd * HEAD_PAD:(hd + 1) * HEAD_PAD]
            k = k_ref[0, j * KEY_BLOCK:(j + 1) * KEY_BLOCK, hd * HEAD_PAD:(hd + 1) * HEAD_PAD]
            return lax.dot_general(k, q, (((1,), (1,)), ((), ())), preferred_element_type=F32)

        def col_max(blocks):
            mx = functools.reduce(jnp.maximum, blocks)
            mx = jnp.max(mx.reshape(KEY_BLOCK // 8, 8, TM), axis=0)
            return jnp.max(mx, axis=0, keepdims=True)

        outs = []
        cur = [score_block(0, j) for j in range(nblk)]
        for hd in range(HEADS):
            mx = col_max(cur)
            nxt, ps = [], []
            for j in range(nblk):
                if hd + 1 < HEADS:
                    nxt.append(score_block(hd + 1, j))
                ps.append(jnp.exp2((cur[j] - mx) * EXP2_SCALE).astype(BF16))
            p = jnp.concatenate(ps, axis=0)
            ot = jnp.dot(vt_ref[0, hd * V_ROWS:(hd + 1) * V_ROWS, :nk], p, preferred_element_type=F32)
            outs.append(ot[:VDIM] / ot[VDIM:VDIM + 1])
            cur = nxt
        o_ref[0] = jnp.concatenate(outs, axis=0).T.astype(BF16)

    if t0 == 0:
        t = pl.program_id(1)
        pl.when(t == 0)(lambda: run(CTX))
        pl.when(t != 0)(lambda: run(T))
    else:
        run(T)


def _attn(q, k, v, *, t0):
    nt = T // TM - t0
    return pl.pallas_call(
        functools.partial(_attn_kernel, t0=t0),
        grid=(BATCH, nt),
        in_specs=[
            _row_spec(HEADS * HEAD_PAD, t0),
            pl.BlockSpec((1, T, HEADS * HEAD_PAD), lambda b, t: (b, 0, 0)),
            pl.BlockSpec((1, HEADS * V_ROWS, T), lambda b, t: (b, 0, 0)),
        ],
        out_specs=pl.BlockSpec((1, TM, HEADS * VDIM), lambda b, t: (b, t, 0)),
        out_shape=jax.ShapeDtypeStruct((BATCH, nt * TM, HEADS * VDIM), BF16),
        compiler_params=_params(("parallel", "arbitrary")),
        name="mla_attn",
    )(q, k, v)


def _ssd_kernel(xc_ref, dt_ref, dtt_ref, bias_row_ref, bias_col_ref, alog_col_ref, dfull_ref,
                y_ref, h_ref, dcol_ref, drow_ref, acol_ref, arow_ref):
    row = lax.broadcasted_iota(jnp.int32, (CHUNK, CHUNK), 0)
    col = lax.broadcasted_iota(jnp.int32, (CHUNK, CHUNK), 1)
    lower = (col <= row)
    upper = (col >= row)
    lower01 = jnp.where(lower, 1.0, 0.0).astype(BF16)
    upper01 = jnp.where(upper, 1.0, 0.0).astype(BF16)
    lo_half = lax.broadcasted_iota(jnp.int32, (CHUNK, 128), 1) < SSM_P

    chunks = range(N_CHUNKS)
    dcol_all = jax.nn.softplus(dt_ref[0] + bias_row_ref[...])
    drow_all = jax.nn.softplus(dtt_ref[0] + bias_col_ref[...])
    dcol_ref[...] = dcol_all
    drow_ref[...] = drow_all
    a_row_form = drow_all * -jnp.exp(alog_col_ref[...])
    a_tall = jnp.concatenate([a_row_form[:, c * CHUNK:(c + 1) * CHUNK] for c in chunks], axis=0)
    causal_row = lax.broadcasted_iota(jnp.int32, (N_CHUNKS * 2 * SSM_HEADS, CHUNK), 0) % (2 * SSM_HEADS) < SSM_HEADS
    arow_tall = jnp.where(causal_row, _dot_right01(a_tall, upper01), _dot_right01(a_tall, lower01))
    lane_pad = jnp.zeros((128 - 2 * SSM_HEADS, CHUNK), F32)
    for c in chunks:
        arow_c = arow_tall[c * 2 * SSM_HEADS:(c + 1) * 2 * SSM_HEADS]
        arow_ref[c] = arow_c
        acol_ref[c] = jnp.concatenate([arow_c, lane_pad], axis=0).T

    def chunk_step(c, d):
        r0 = pl.multiple_of(c * CHUNK, CHUNK)
        xc = xc_ref[0, pl.ds(r0, CHUNK), :]
        dcol = dcol_ref[pl.ds(r0, CHUNK), :]
        drow = drow_ref[:, pl.ds(r0, CHUNK)]
        mask = lower if d == 0 else upper
        acum_c = acol_ref[c]
        acum_r = arow_ref[c]
        atot_c = acum_c[CHUNK - 1:CHUNK] if d == 0 else acum_c[0:1]
        w_c = jnp.exp(atot_c - acum_c) * dcol
        etot_c = jnp.exp(atot_c)

        def head_col(arr, hd):
            lane = d * SSM_HEADS + hd
            return arr[:, lane:lane + 1]

        def pair_lanes(arr, p):
            return jnp.where(lo_half[:arr.shape[0]], head_col(arr, 2 * p), head_col(arr, 2 * p + 1))

        ys = []
        for g in range(SSM_GROUPS):
            bm = xc[:, SSM_INNER + g * SSM_N:SSM_INNER + (g + 1) * SSM_N]
            cm = xc[:, SSM_INNER + (SSM_GROUPS + g) * SSM_N:SSM_INNER + (SSM_GROUPS + g + 1) * SSM_N]
            bmb = bm.astype(BF16)
            cmb = cm.astype(BF16)
            cb = lax.dot_general(cmb, bmb, (((1,), (1,)), ((), ())), preferred_element_type=F32)
            bt = bm.T.astype(BF16)
            for pp in range(2):
                p = g * 2 + pp
                sc, a_bc = [], []
                for hh in range(2):
                    hd = 2 * p + hh
                    a_bc.append(jnp.broadcast_to(head_col(acum_c, hd), (CHUNK, CHUNK)))
                    diff = a_bc[hh] - acum_r[d * SSM_HEADS + hd:d * SSM_HEADS + hd + 1]
                    dec = jnp.exp(jnp.where(mask, diff, -jnp.inf))
                    sc.append(cb * dec * drow[d * SSM_HEADS + hd:d * SSM_HEADS + hd + 1])
                x_p = xc[:, p * 128:(p + 1) * 128]
                xw = (x_p * pair_lanes(w_c, p)).astype(BF16)
                s_t = jnp.dot(bt, xw, preferred_element_type=F32)
                h_prev = h_ref[d, p]
                e_p = jnp.exp(jnp.where(lo_half, a_bc[0], a_bc[1]))
                y_inter = jnp.dot(cmb, h_prev.astype(BF16), preferred_element_type=F32) * e_p
                pmat = jnp.concatenate(sc, axis=1).astype(BF16)
                xbd = jnp.concatenate([jnp.where(lo_half, x_p, 0.0), jnp.where(lo_half, 0.0, x_p)],
                                      axis=0).astype(BF16)
                ys.append(jnp.dot(pmat, xbd, preferred_element_type=F32) + y_inter)
                h_ref[d, p] = h_prev * pair_lanes(etot_c, p) + s_t
        y = jnp.concatenate(ys, axis=1)
        if d == 0:
            y = y + dfull_ref[...] * xc[:, :SSM_INNER]
        y_ref[0, pl.ds(r0, CHUNK), :] += y

    h_ref[...] = jnp.zeros_like(h_ref)
    y_ref[...] = jnp.zeros_like(y_ref)

    def scan_body(i, carry):
        chunk_step(i, 0)
        chunk_step(jnp.where(i < CTX_CHUNKS, CTX_CHUNKS - 1 - i, N_CHUNKS + CTX_CHUNKS - 1 - i), 1)
        return carry

    lax.fori_loop(0, N_CHUNKS, scan_body, 0, unroll=2)


def _ssd(xc, dt, dtt, w, l):
    return pl.pallas_call(
        _ssd_kernel,
        grid=(BATCH,),
        in_specs=[
            pl.BlockSpec((1, T, XBC), lambda b: (b, 0, 0)),
            pl.BlockSpec((1, T, 128), lambda b: (b, 0, 0)),
            pl.BlockSpec((1, 2 * SSM_HEADS, T), lambda b: (b, 0, 0)),
            _const_spec((1, 128), l),
            _const_spec((2 * SSM_HEADS, 1), l),
            _const_spec((2 * SSM_HEADS, 1), l),
            _const_spec((1, SSM_INNER), l),
        ],
        out_specs=pl.BlockSpec((1, T, SSM_INNER), lambda b: (b, 0, 0)),
        out_shape=jax.ShapeDtypeStruct((BATCH, T, SSM_INNER), F32),
        scratch_shapes=[
            pltpu.VMEM((2, SSM_HEADS // 2, SSM_N, 2 * SSM_P), F32),
            pltpu.VMEM((T, 128), F32),
            pltpu.VMEM((2 * SSM_HEADS, T), F32),
            pltpu.VMEM((N_CHUNKS, CHUNK, 128), F32),
            pltpu.VMEM((N_CHUNKS, 2 * SSM_HEADS, CHUNK), F32),
        ],
        compiler_params=_params(("arbitrary",)),
        name="ssd_bidir",
    )(xc, dt, dtt, w["dt_bias_row"], w["dt_bias_col"], w["a_log_col"], w["d_full"])


def _mixout_kernel(x_ref, mod_ref, a_ref, y_ref, z_ref, uv_ref, gate_ref,
                   wmo_ref, sn_ref, wso_ref, gn_ref, ws_ref, bs_ref, wgo_ref, bg_ref, wout_ref,
                   g2_ref, wi_ref, wo_ref, fin_ref, o_ref, *, final):
    x = x_ref[0]
    m = mod_ref[0]
    o_mla = jnp.dot(a_ref[0], wmo_ref[...], preferred_element_type=F32)

    ys = y_ref[0] * _silu(z_ref[0].astype(F32))
    o_ssm = _bdot(_rms(ys, sn_ref[...]), wso_ref[...])

    uv = jax.nn.gelu(uv_ref[0].astype(F32))
    u = uv[:, :GM_WIDTH]
    v = uv[:, GM_WIDTH:]
    v = v - jnp.mean(v, axis=-1, keepdims=True)
    v = v * lax.rsqrt(jnp.mean(v * v, axis=-1, keepdims=True) + EPS) * gn_ref[...]
    lo_half = lax.broadcasted_iota(jnp.int32, (CHUNK, 128), 1) < GM_GDIM
    mixed = []
    for ch in range(TM // CHUNK):
        parts = []
        for p in range(GM_GROUPS // 2):
            vp = v[ch * CHUNK:(ch + 1) * CHUNK, p * 128:(p + 1) * 128]
            vbd = jnp.concatenate([jnp.where(lo_half, vp, 0.0), jnp.where(lo_half, 0.0, vp)], axis=0)
            parts.append(jnp.dot(ws_ref[p], vbd.astype(BF16), preferred_element_type=F32))
        mixed.append(jnp.concatenate(parts, axis=1) + bs_ref[...])
    mixed = jnp.concatenate(mixed, axis=0)
    o_gm = _bdot(u * mixed, wgo_ref[...])

    gts = jax.nn.sigmoid(gate_ref[0].astype(F32) + bg_ref[...])
    merged = gts[:, :D] * o_mla + gts[:, D:2 * D] * o_ssm + gts[:, 2 * D:] * o_gm
    y = _bdot(merged, wout_ref[...])
    x = x + m[5:6] * y
    o_ref[0] = _ffn_rows(x, m, 6, g2_ref, wi_ref, wo_ref, fin_ref if final else None)


def _mixout(xs, a, y, z, uv, gate, w, l, *, t0, final):
    nt = T // TM - t0
    return pl.pallas_call(
        functools.partial(_mixout_kernel, final=final),
        grid=(BATCH, nt),
        in_specs=[
            _row_spec(D, t0),
            _mod_spec(l, t0),
            pl.BlockSpec((1, TM, HEADS * VDIM), lambda b, t: (b, t, 0)),
            _row_spec(SSM_INNER, t0),
            _row_spec(SSM_INNER, t0),
            _row_spec(2 * GM_WIDTH, t0),
            _row_spec(N_BRANCH * D, t0),
            _const_spec((HEADS * VDIM, D), l),
            _const_spec((1, SSM_INNER), l),
            _const_spec((SSM_INNER, D), l),
            _const_spec((1, GM_WIDTH), l),
            _const_spec((GM_GROUPS // 2, CHUNK, 2 * CHUNK), l),
            _const_spec((CHUNK, GM_WIDTH), l),
            _const_spec((GM_WIDTH, D), l),
            _const_spec((1, N_BRANCH * D), l),
            _const_spec((D, D), l),
            _const_spec((1, D), 3 * l + 2),
            _const_spec((D, 2 * D_FF), l),
            _const_spec((D_FF, D), l),
            _const_spec((1, D)),
        ],
        out_specs=pl.BlockSpec((1, TM, D), lambda b, t: (b, t, 0)),
        out_shape=jax.ShapeDtypeStruct((BATCH, nt * TM, D), F32),
        compiler_params=_params(("parallel", "parallel")),
        name="mixer_out_ffn",
    )(xs, w["mods"], a, y, z, uv, gate, w["w_mla_o"], w["ssm_norm"], w["w_ssm_o"], w["gm_norm"], w["w_s"],
      w["b_s"], w["w_gm_o"], w["b_gate"], w["w_out"], w["norm_g"], w["ffn_w_in"][1], w["ffn_w_out"][1],
      w["final_norm"])


def _rot_half(w):
    half = ROPE // 2
    return jnp.concatenate([-w[..., half:], w[..., :half]], axis=-1)


def _head_pad(nope, rope):
    pad = jnp.zeros(nope.shape[:-1] + (HEAD_PAD - NOPE - ROPE,), nope.dtype)
    out = jnp.concatenate([nope, rope, pad], axis=-1)
    return out.reshape(out.shape[:-2] + (HEADS * HEAD_PAD,))


def _prep_weights(w_in, mla_q_norm, mla_w_uq, mla_kv_norm, mla_w_ukv, mla_w_o, ssm_conv_w, ssm_conv_b,
                  ssm_a_log, ssm_dt_bias, ssm_d, ssm_norm, ssm_w_o, gm_norm, gm_w_s, gm_b_s, gm_w_o, b_gate, w_out):
    w = w_in
    o = 0
    kv = w[..., o:o + KV_RANK]; o += KV_RANK
    kr = w[..., o:o + ROPE]; o += ROPE
    xbc = w[..., o:o + XBC]; o += XBC
    dtw = w[..., o:o + 2 * SSM_HEADS]; o += 2 * SSM_HEADS
    rest = w[..., o:]

    def lanes(cols, start):
        return jnp.pad(cols, ((0, 0), (0, 0), (start, 128 - start - cols.shape[-1])))

    pieces = [kv, lanes(kr, NOPE), lanes(_rot_half(kr), NOPE), lanes(dtw, 0), xbc]
    w_kv_side = jnp.concatenate([p.astype(BF16) for p in pieces], axis=-1)

    uq = mla_w_uq.astype(BF16).reshape(DEPTH, Q_RANK, HEADS, NOPE + ROPE)
    uq_n, uq_r = uq[..., :NOPE], uq[..., NOPE:]
    ukv = mla_w_ukv.astype(BF16).reshape(DEPTH, KV_RANK, HEADS, NOPE + VDIM)
    kn = _head_pad(ukv[..., :NOPE], jnp.zeros((DEPTH, KV_RANK, HEADS, ROPE), BF16))
    v_t = jnp.pad(jnp.transpose(ukv[..., NOPE:], (0, 2, 3, 1)), ((0, 0), (0, 0), (0, V_ROWS - VDIM), (0, 0)))
    v_ones = jnp.tile((jnp.arange(V_ROWS) == VDIM).astype(F32), HEADS).reshape(HEADS * V_ROWS, 1)

    ws = gm_w_s.astype(BF16)
    ws_pair = jnp.concatenate([ws[:, 0::2], ws[:, 1::2]], axis=3)

    def row128(vals):
        vals = vals.reshape(DEPTH, 1, -1)
        return jnp.pad(vals, ((0, 0), (0, 0), (0, 128 - vals.shape[-1])))

    return {
        "w_in_kv": w_kv_side,
        "w_in_q": rest.astype(BF16),
        "kv_norm": mla_kv_norm.reshape(DEPTH, 1, KV_RANK),
        "w_kn": kn,
        "w_v_t": v_t.reshape(DEPTH, HEADS * V_ROWS, KV_RANK),
        "v_ones": v_ones,
        "q_norm": mla_q_norm.reshape(DEPTH, 1, Q_RANK),
        "w_q": _head_pad(uq_n, uq_r),
        "w_q_rot": _head_pad(jnp.zeros_like(uq_n), _rot_half(uq_r)),
        "conv_w": jnp.pad(jnp.swapaxes(ssm_conv_w, 1, 2), ((0, 0), (0, 8 - SSM_CONV), (0, 0))),
        "conv_b": ssm_conv_b.reshape(DEPTH, 1, XBC),
        "dt_bias_row": row128(ssm_dt_bias),
        "dt_bias_col": ssm_dt_bias.reshape(DEPTH, 2 * SSM_HEADS, 1),
        "a_log_col": ssm_a_log.reshape(DEPTH, 2 * SSM_HEADS, 1),
        "d_full": jnp.repeat(ssm_d, SSM_P, axis=1).reshape(DEPTH, 1, SSM_INNER),
        "w_mla_o": mla_w_o.astype(BF16),
        "ssm_norm": ssm_norm.reshape(DEPTH, 1, SSM_INNER),
        "w_ssm_o": ssm_w_o.astype(BF16),
        "gm_norm": gm_norm.reshape(DEPTH, 1, GM_WIDTH),
        "w_s": ws_pair,
        "b_s": jnp.repeat(jnp.swapaxes(gm_b_s, 1, 2), GM_GDIM, axis=2),
        "w_gm_o": gm_w_o.astype(BF16),
        "b_gate": b_gate.reshape(DEPTH, 1, N_BRANCH * D),
        "w_out": w_out.astype(BF16),
    }


def _rope_tables():
    rows = SEQ // GRID_W
    r = jnp.repeat(jnp.arange(rows, dtype=F32), GRID_W)
    c = jnp.tile(jnp.arange(GRID_W, dtype=F32), rows)
    n_freq = ROPE // 4
    inv = jnp.power(ROPE_BASE, -jnp.arange(n_freq, dtype=F32) / n_freq)
    ang = jnp.concatenate([r[:, None] * inv, c[:, None] * inv], axis=-1)
    cos, sin = jnp.cos(ang), jnp.sin(ang)
    cos = jnp.concatenate([jnp.ones((CTX, ROPE // 2), F32), cos], axis=0)
    sin = jnp.concatenate([jnp.zeros((CTX, ROPE // 2), F32), sin], axis=0)
    ones = jnp.ones((T, NOPE), F32)
    zeros_n = jnp.zeros((T, NOPE), F32)
    zeros_p = jnp.zeros((T, HEAD_PAD - NOPE - ROPE), F32)
    cos_t = jnp.concatenate([ones, cos, cos, zeros_p], axis=1)
    sin_t = jnp.concatenate([zeros_n, sin, sin, zeros_p], axis=1)
    return cos_t, sin_t


def kernel(x, c, ctx, c_ctx, w_ada, b_ada, norm_g, ffn1_w_in, ffn1_w_out, ffn2_w_in, ffn2_w_out, w_in, mla_q_norm,
           mla_w_uq, mla_kv_norm, mla_w_ukv, mla_w_o, ssm_conv_w, ssm_conv_b, ssm_a_log, ssm_dt_bias, ssm_d,
           ssm_norm, ssm_w_o, gm_norm, gm_w_s, gm_b_s, gm_w_o, b_gate, w_out, final_norm):
    c_all = jnp.concatenate([c, c_ctx[None, :], jnp.zeros((MOD_ROWS - BATCH - 1, D), F32)], axis=0)
    w = _prep_weights(w_in, mla_q_norm, mla_w_uq, mla_kv_norm, mla_w_ukv, mla_w_o, ssm_conv_w, ssm_conv_b, ssm_a_log,
                      ssm_dt_bias, ssm_d, ssm_norm, ssm_w_o, gm_norm, gm_w_s, gm_b_s, gm_w_o, b_gate, w_out)
    w["mods"] = _ada(c_all, w_ada, b_ada).reshape(DEPTH, MOD_ROWS, N_MOD, D)
    w["norm_g"] = norm_g.reshape(DEPTH * 3, 1, D)
    w["ffn_w_in"] = (ffn1_w_in.astype(BF16), ffn2_w_in.astype(BF16))
    w["ffn_w_out"] = (ffn1_w_out.astype(BF16), ffn2_w_out.astype(BF16))
    w["final_norm"] = final_norm.reshape(1, D)
    w["cos"], w["sin"] = _rope_tables()

    xs = (ctx, x)
    for l in range(DEPTH):
        last = l == DEPTH - 1
        t0 = 1 if last else 0
        xs = _ffn(xs, w, l)
        q, k, vt, xbc, dt, dtt, z, uv, gate = _proj(xs, w, l, ctx_queries=not last)
        a = _attn(q, k, vt, t0=t0)
        y = _ssd(xbc, dt, dtt, w, l)
        xs = _mixout(xs, a, y, z, uv, gate, w, l, t0=t0, final=last)
    return xs
```
